```python
import jax
import jax.numpy as jnp
from jax import lax
import numpy as np

D_MODEL = 1024
BATCH = 8
SEQ = 4096
DEPTH = 2
DEC_BATCH = 32
DEC_SEQ = 32
PAST_LEN = 2048

CHUNK = 64
Q_BLOCK = 128
HG_HEADS = 4
HG_KDIM = 128
HG_VDIM = 128
HG_WIDTH = HG_HEADS * HG_KDIM
FOX_HEADS = 8
FOX_HEAD_DIM = 64
FOX_WIDTH = FOX_HEADS * FOX_HEAD_DIM
MIX_WIDTH = HG_WIDTH + FOX_WIDTH
IN_SPLITS = [HG_WIDTH, 2 * HG_WIDTH, 3 * HG_WIDTH, 4 * HG_WIDTH,
             4 * HG_WIDTH + FOX_WIDTH, 4 * HG_WIDTH + 2 * FOX_WIDTH, 4 * HG_WIDTH + 3 * FOX_WIDTH]
IN_COLS = 4 * HG_WIDTH + 3 * FOX_WIDTH + FOX_HEADS
N_EXPERTS = 16
N_GROUPS = 4
EXPERTS_PER_GROUP = N_EXPERTS // N_GROUPS
TOP_K = 2
D_EXPERT = 512
N_MOD = 6
FOX_F_BIAS_INIT = 3.0
EPS = 1e-6

kernel_name = 'hybrid_hgrn2_fox_moe_stream_step'


def rms_norm(x, gain):
    xf = x.astype(jnp.float32)
    y = xf * lax.rsqrt(jnp.mean(xf * xf, axis=-1, keepdims=True) + EPS)
    return (y * gain.astype(jnp.float32)).astype(x.dtype)


def ada_mod(c, w_ada_l, b_ada_l):
    m = jax.nn.silu(c) @ w_ada_l + b_ada_l
    return jnp.split(m[:, None, :], N_MOD, axis=-1)


def hgrn_lower_bound(hg_lower_bounds, l):
    p = jax.nn.softmax(hg_lower_bounds.astype(jnp.float32), axis=0)
    cum = jnp.cumsum(p, axis=0)
    return cum[l] - cum[0]


def hgrn_chunk(S, q, k, v, logf):
    L = q.shape[1]
    b = jnp.cumsum(logf, axis=1)
    o_inter = jnp.einsum('blhk,bhkv->blhv', q * jnp.exp(b), S)
    causal = jnp.tril(jnp.ones((L, L), dtype=bool))[None, :, :, None, None]
    decay = jnp.exp(jnp.where(causal, b[:, :, None] - b[:, None, :], -jnp.inf))
    att = jnp.einsum('bthk,bshk,btshk->bhts', q, k, decay)
    o_intra = jnp.einsum('bhts,bshv->bthv', att, v)
    b_last = b[:, -1]
    S_new = jnp.exp(b_last)[..., None] * S + jnp.einsum(
        'bshk,bshv->bhkv', k * jnp.exp(b_last[:, None] - b), v)
    return S_new, o_intra + o_inter


def hgrn_scan(S0, q, k, v, logf):
    B, L = q.shape[0], q.shape[1]
    if L <= CHUNK:
        return hgrn_chunk(S0, q, k, v, logf)
    n = L // CHUNK

    def to_blocks(a):
        return jnp.moveaxis(a.reshape(B, n, CHUNK, *a.shape[2:]), 1, 0)

    def step(S, inp):
        return hgrn_chunk(S, *inp)

    S_last, o = lax.scan(step, S0, (to_blocks(q), to_blocks(k), to_blocks(v), to_blocks(logf)))
    o = jnp.moveaxis(o, 0, 1).reshape(B, L, HG_HEADS, HG_VDIM)
    return S_last, o


def hgrn_mixer(S0, hq, hf, hi, hg, lb, gain):
    B, L = hq.shape[0], hq.shape[1]
    f32 = jnp.float32
    q = jax.nn.silu(hq.astype(f32)).reshape(B, L, HG_HEADS, HG_KDIM) * (HG_KDIM ** -0.5)
    fr = hf.astype(f32).reshape(B, L, HG_HEADS, HG_KDIM)
    lbh = lb.reshape(HG_HEADS, HG_KDIM)
    logf = jnp.logaddexp(jnp.log(lbh), jnp.log1p(-lbh) + jax.nn.log_sigmoid(fr))
    k = (1.0 - lbh) * jax.nn.sigmoid(-fr)
    v = hi.astype(f32).reshape(B, L, HG_HEADS, HG_VDIM)
    S, o = hgrn_scan(S0.astype(f32), q, k, v, logf)
    o = o * lax.rsqrt(jnp.mean(o * o, axis=-1, keepdims=True) + EPS)
    o = o.reshape(B, L, HG_WIDTH) * gain.astype(f32) * jax.nn.silu(hg.astype(f32))
    return o.astype(hq.dtype), S


def fox_block(q, k, v, cq, ck, q_start):
    Lq, Lk = q.shape[1], k.shape[1]
    s = jnp.einsum('bqhd,bkhd->bhqk', q, k).astype(jnp.float32) * (FOX_HEAD_DIM ** -0.5)
    s = s + jnp.moveaxis(cq, 1, 2)[..., :, None] - jnp.moveaxis(ck, 1, 2)[..., None, :]
    q_pos = q_start + jnp.arange(Lq, dtype=jnp.int32)
    k_pos = jnp.arange(Lk, dtype=jnp.int32)
    s = jnp.where(k_pos[None, :] <= q_pos[:, None], s, -jnp.inf)
    p = jax.nn.softmax(s, axis=-1)
    return jnp.einsum('bhqk,bkhd->bqhd', p.astype(v.dtype), v)


def fox_mixer(fq, fk, fv, ff, f_bias, out_gain, past):
    B, L = fq.shape[0], fq.shape[1]
    q = fq.reshape(B, L, FOX_HEADS, FOX_HEAD_DIM)
    k = fk.reshape(B, L, FOX_HEADS, FOX_HEAD_DIM)
    v = fv.reshape(B, L, FOX_HEADS, FOX_HEAD_DIM)
    logf = jax.nn.log_sigmoid(ff.astype(jnp.float32) + f_bias.astype(jnp.float32))
    if past is None:
        c = jnp.cumsum(logf, axis=1)
        outs = [fox_block(q[:, t0:t0 + Q_BLOCK], k[:, :t0 + Q_BLOCK], v[:, :t0 + Q_BLOCK],
                          c[:, t0:t0 + Q_BLOCK], c[:, :t0 + Q_BLOCK], t0)
                for t0 in range(0, L, Q_BLOCK)]
        o = jnp.concatenate(outs, axis=1)
    else:
        pk, pv, plogf = past
        P = pk.shape[1]
        k_all = jnp.concatenate([pk.astype(k.dtype), k], axis=1)
        v_all = jnp.concatenate([pv.astype(v.dtype), v], axis=1)
        c = jnp.cumsum(jnp.concatenate([plogf.astype(jnp.float32), logf], axis=1), axis=1)
        o = fox_block(q, k_all, v_all, c[:, P:], c, P)
    o = rms_norm(o.reshape(B, L, FOX_WIDTH), out_gain)
    return o, k, v, logf


def moe(h, w_router, b_router, w_gate, w_up, w_down):
    B, L, D = h.shape
    t = h.reshape(B * L, D)
    probs = jax.nn.softmax((t @ w_router).astype(jnp.float32), axis=-1)
    sel = (probs + b_router.astype(jnp.float32)).reshape(-1, N_GROUPS, EXPERTS_PER_GROUP)
    group_score = jnp.sum(lax.top_k(sel, TOP_K)[0], axis=-1)
    g_idx = jnp.argmax(group_score, axis=-1)
    in_group = jnp.sum(sel * jax.nn.one_hot(g_idx, N_GROUPS, dtype=jnp.float32)[:, :, None], axis=1)
    e_local = lax.top_k(in_group, TOP_K)[1]
    e_idx = g_idx[:, None] * EXPERTS_PER_GROUP + e_local
    w = jnp.take_along_axis(probs, e_idx, axis=-1)
    w = w / jnp.sum(w, axis=-1, keepdims=True)
    gates = jnp.sum(jax.nn.one_hot(e_idx, N_EXPERTS, dtype=jnp.float32) * w[..., None], axis=1)
    y = jnp.zeros((B * L, D), jnp.float32)
    for e in range(N_EXPERTS):
        he = jax.nn.silu(t @ w_gate[e]) * (t @ w_up[e])
        y = y + gates[:, e:e + 1] * (he @ w_down[e]).astype(jnp.float32)
    return y.reshape(B, L, D).astype(h.dtype)


def trunk_layer(x, c, l, p, hg_state0, fox_past):
    shift_m, scale_m, gate_m, shift_f, scale_f, gate_f = ada_mod(c, p['w_ada'][l], p['b_ada'][l])
    h = rms_norm(x, p['norm_mix_gain'][l]) * (1.0 + scale_m) + shift_m
    hq, hf, hi, hg, fq, fk, fv, ff = jnp.split(h @ p['w_in'][l], IN_SPLITS, axis=-1)
    lb = hgrn_lower_bound(p['hg_lower_bounds'], l)
    hg_out, hg_state = hgrn_mixer(hg_state0, hq, hf, hi, hg, lb, p['hg_norm_gain'][l])
    fox_out, k_new, v_new, logf_new = fox_mixer(fq, fk, fv, ff, p['fox_f_bias'][l], p['fox_out_gain'][l], fox_past)
    mixed = jnp.concatenate([hg_out, fox_out.astype(hg_out.dtype)], axis=-1) @ p['w_out'][l]
    x = x + gate_m * mixed
    h = rms_norm(x, p['norm_ffn_gain'][l]) * (1.0 + scale_f) + shift_f
    x = x + gate_f * moe(h, p['w_router'], p['b_router'], p['w_exp_gate'][l], p['w_exp_up'][l], p['w_exp_down'][l])
    return x, hg_state, k_new, v_new, logf_new


def setup_inputs(seed: int = 0) -> dict:
    key = jax.random.key(seed)
    ks = jax.random.split(key, 24)
    f32 = jnp.float32

    def nrm(k, shape, scale=1.0):
        return jax.random.normal(k, shape, f32) * scale

    d_in = D_MODEL ** -0.5
    return {
        'x_prompt': nrm(ks[0], (BATCH, SEQ, D_MODEL)),
        'x_sample': nrm(ks[1], (DEC_BATCH, DEC_SEQ, D_MODEL)),
        'cache_fox_k': nrm(ks[2], (DEPTH, DEC_BATCH, PAST_LEN, FOX_HEADS, FOX_HEAD_DIM)),
        'cache_fox_v': nrm(ks[3], (DEPTH, DEC_BATCH, PAST_LEN, FOX_HEADS, FOX_HEAD_DIM)),
        'cache_fox_logf': jax.nn.log_sigmoid(FOX_F_BIAS_INIT + nrm(ks[4], (DEPTH, DEC_BATCH, PAST_LEN, FOX_HEADS))),
        'state_hgrn': nrm(ks[5], (DEPTH, DEC_BATCH, HG_HEADS, HG_KDIM, HG_VDIM), 0.5),
        'c_prompt': nrm(ks[6], (BATCH, D_MODEL)),
        'c_sample': nrm(ks[7], (DEC_BATCH, D_MODEL)),
        'norm_mix_gain': 1.0 + nrm(ks[8], (DEPTH, D_MODEL), 0.02),
        'norm_ffn_gain': 1.0 + nrm(ks[9], (DEPTH, D_MODEL), 0.02),
        'w_ada': nrm(ks[10], (DEPTH, D_MODEL, N_MOD * D_MODEL), 0.5 * d_in),
        'b_ada': nrm(ks[11], (DEPTH, N_MOD * D_MODEL), 0.02),
        'w_in': nrm(ks[12], (DEPTH, D_MODEL, IN_COLS), d_in),
        'hg_lower_bounds': nrm(ks[13], (DEPTH, HG_WIDTH), 0.1),
        'hg_norm_gain': 1.0 + nrm(ks[14], (DEPTH, HG_WIDTH), 0.02),
        'fox_f_bias': FOX_F_BIAS_INIT + nrm(ks[15], (DEPTH, FOX_HEADS), 0.1),
        'fox_out_gain': 1.0 + nrm(ks[16], (DEPTH, FOX_WIDTH), 0.02),
        'w_out': nrm(ks[17], (DEPTH, MIX_WIDTH, D_MODEL), MIX_WIDTH ** -0.5),
        'w_router': nrm(ks[18], (D_MODEL, N_EXPERTS), d_in),
        'b_router': nrm(ks[19], (N_EXPERTS,), 0.01),
        'w_exp_gate': nrm(ks[20], (DEPTH, N_EXPERTS, D_MODEL, D_EXPERT), d_in),
        'w_exp_up': nrm(ks[21], (DEPTH, N_EXPERTS, D_MODEL, D_EXPERT), d_in),
        'w_exp_down': nrm(ks[22], (DEPTH, N_EXPERTS, D_EXPERT, D_MODEL), D_EXPERT ** -0.5),
        'final_norm_gain': 1.0 + nrm(ks[23], (D_MODEL,), 0.02),
    }


def reference(x_prompt, x_sample, cache_fox_k, cache_fox_v, cache_fox_logf, state_hgrn, c_prompt, c_sample,
              norm_mix_gain, norm_ffn_gain, w_ada, b_ada, w_in, hg_lower_bounds, hg_norm_gain,
              fox_f_bias, fox_out_gain, w_out, w_router, b_router, w_exp_gate, w_exp_up, w_exp_down,
              final_norm_gain):
    p = {'norm_mix_gain': norm_mix_gain, 'norm_ffn_gain': norm_ffn_gain, 'w_ada': w_ada, 'b_ada': b_ada,
         'w_in': w_in, 'hg_lower_bounds': hg_lower_bounds, 'hg_norm_gain': hg_norm_gain,
         'fox_f_bias': fox_f_bias, 'fox_out_gain': fox_out_gain, 'w_out': w_out,
         'w_router': w_router, 'b_router': b_router, 'w_exp_gate': w_exp_gate, 'w_exp_up': w_exp_up,
         'w_exp_down': w_exp_down}
    xp, xs = x_prompt, x_sample
    sp_h, sp_k, sp_v, sp_f = [], [], [], []
    ss_h, ss_k, ss_v, ss_f = [], [], [], []
    for l in range(DEPTH):
        s0 = jnp.zeros((xp.shape[0], HG_HEADS, HG_KDIM, HG_VDIM), jnp.float32)
        xp, h_p, k_p, v_p, f_p = trunk_layer(xp, c_prompt, l, p, s0, None)
        xs, h_s, k_s, v_s, f_s = trunk_layer(xs, c_sample, l, p, state_hgrn[l],
                                             (cache_fox_k[l], cache_fox_v[l], cache_fox_logf[l]))
        sp_h.append(h_p); sp_k.append(k_p); sp_v.append(v_p); sp_f.append(f_p)
        ss_h.append(h_s); ss_k.append(k_s); ss_v.append(v_s); ss_f.append(f_s)
    y_prompt = rms_norm(xp, final_norm_gain)
    y_sample = rms_norm(xs, final_norm_gain)
    hgrn_state_prompt = jnp.stack(sp_h)
    fox_k_prompt = jnp.stack(sp_k)
    fox_v_prompt = jnp.stack(sp_v)
    fox_logf_prompt = jnp.stack(sp_f)
    hgrn_state_sample = jnp.stack(ss_h)
    fox_k_sample = jnp.stack(ss_k)
    fox_v_sample = jnp.stack(ss_v)
    fox_logf_sample = jnp.stack(ss_f)
    return (y_prompt, y_sample, hgrn_state_prompt, fox_k_prompt, fox_v_prompt, fox_logf_prompt,
            hgrn_state_sample, fox_k_sample, fox_v_sample, fox_logf_sample)
```

```python
import functools

import jax
import jax.numpy as jnp
from jax import lax
from jax.experimental import pallas as pl
from jax.experimental.pallas import tpu as pltpu

HG_HEADS = 4
HG_DIM = 128
HG_WIDTH = HG_HEADS * HG_DIM
FOX_HEADS = 8
FOX_DIM = 64
FOX_WIDTH = FOX_HEADS * FOX_DIM
N_EXPERTS = 16
N_GROUPS = 4
GROUP_SIZE = N_EXPERTS // N_GROUPS
TOP_K = 2
N_MOD = 6
EPS = 1e-6

LANES = 128
HG_CHUNK = 128
HG_SUB = 16
VMEM_LIMIT = 56 * 1024 * 1024

F32 = jnp.float32
BF16 = jnp.bfloat16
NEG = -1e30


def _params(sem, vmem=VMEM_LIMIT):
    return pltpu.CompilerParams(dimension_semantics=sem, vmem_limit_bytes=vmem)


def _split3(x):
    hi = x.astype(BF16)
    r1 = x - hi.astype(F32)
    mid = r1.astype(BF16)
    lo = (r1 - mid.astype(F32)).astype(BF16)
    return hi, mid, lo


def _log_sigmoid(x):
    return jnp.minimum(x, 0.0) - jnp.log1p(jnp.exp(-jnp.abs(x)))


def _silu(x):
    return x * jax.nn.sigmoid(x)


def _ada_kernel(c_ref, w_ref, b_ref, o_ref):
    s = _silu(c_ref[...]).astype(BF16)
    o_ref[0] = jnp.dot(s, w_ref[0].astype(BF16), preferred_element_type=F32) + b_ref[0]


def ada_mod(c, w_ada, b_ada):
    depth, d, n = w_ada.shape
    nb = c.shape[0]
    tn = 1536
    return pl.pallas_call(
        _ada_kernel,
        out_shape=jax.ShapeDtypeStruct((depth, nb, n), F32),
        grid=(depth, n // tn),
        in_specs=[pl.BlockSpec((nb, d), lambda l, j: (0, 0)),
                  pl.BlockSpec((1, d, tn), lambda l, j: (l, 0, j)),
                  pl.BlockSpec((1, 1, tn), lambda l, j: (l, 0, j))],
        out_specs=pl.BlockSpec((1, nb, tn), lambda l, j: (l, 0, j)),
        compiler_params=_params(("arbitrary", "arbitrary")),
        name="ada_mod",
    )(c, w_ada, b_ada.reshape(depth, 1, n))


def _inproj_kernel(x_ref, mod_ref, gain_ref, w_ref,
                   hg_ref, fq_ref, fk32_ref, fv32_ref, fk16_ref, fv16_ref, ff_ref):
    s, ls, d = x_ref.shape
    x = x_ref[...]
    y = x * lax.rsqrt(jnp.mean(x * x, axis=-1, keepdims=True) + EPS) * gain_ref[...]
    h = y * (1.0 + mod_ref[:, 1:2, :]) + mod_ref[:, 0:1, :]
    hb = h.reshape(s * ls, d).astype(BF16)

    def proj(lo, hi):
        return jnp.dot(hb, w_ref[:, lo:hi], preferred_element_type=F32)

    c0 = 4 * HG_WIDTH
    hg_ref[...] = proj(0, c0).reshape(s, ls, c0)
    fq = proj(c0, c0 + FOX_WIDTH) * (FOX_DIM ** -0.5)
    fq_ref[...] = fq.reshape(s, ls, FOX_WIDTH).astype(BF16)
    fk = proj(c0 + FOX_WIDTH, c0 + 2 * FOX_WIDTH).reshape(s, ls, FOX_WIDTH)
    fk32_ref[...] = fk
    fk16_ref[...] = fk.astype(BF16)
    fv = proj(c0 + 2 * FOX_WIDTH, c0 + 3 * FOX_WIDTH).reshape(s, ls, FOX_WIDTH)
    fv32_ref[...] = fv
    fv16_ref[...] = fv.astype(BF16)
    ff_ref[...] = proj(c0 + 3 * FOX_WIDTH, c0 + 3 * FOX_WIDTH + LANES).reshape(s, ls, LANES)


def inproj(x, mod, gain, w_pad, s, ls):
    b, l, d = x.shape
    n = w_pad.shape[1]
    row = lambda i, j: (i, j, 0)
    outs = [(4 * HG_WIDTH, F32), (FOX_WIDTH, BF16), (FOX_WIDTH, F32), (FOX_WIDTH, F32),
            (FOX_WIDTH, BF16), (FOX_WIDTH, BF16), (LANES, F32)]
    return pl.pallas_call(
        _inproj_kernel,
        out_shape=[jax.ShapeDtypeStruct((b, l, w), dt) for w, dt in outs],
        grid=(b // s, l // ls),
        in_specs=[pl.BlockSpec((s, ls, d), row),
                  pl.BlockSpec((s, N_MOD, d), lambda i, j: (i, 0, 0)),
                  pl.BlockSpec((1, d), lambda i, j: (0, 0)),
                  pl.BlockSpec((d, n), lambda i, j: (0, 0), pipeline_mode=pl.Buffered(1))],
        out_specs=[pl.BlockSpec((s, ls, w), row) for w, _ in outs],
        compiler_params=_params(("parallel", "parallel")),
        name="inproj",
    )(x, mod, gain, w_pad)


def _cumsum_kernel(x_ref, bias_ref, lf_ref, c_ref, carry_ref, *, apply_ls):
    @pl.when(pl.program_id(1) == 0)
    def _():
        carry_ref[...] = jnp.zeros_like(carry_ref)

    x = x_ref[0]
    tc = x.shape[-1]
    lf = _log_sigmoid(x + bias_ref[...]) if apply_ls else x
    lf_ref[0] = lf
    r = lax.broadcasted_iota(jnp.int32, (tc, tc), 0)
    c = lax.broadcasted_iota(jnp.int32, (tc, tc), 1)
    tri = jnp.where(r <= c, 1.0, 0.0).astype(BF16)
    hi, mid, lo = _split3(lf)
    tot = (jnp.dot(hi, tri, preferred_element_type=F32)
           + jnp.dot(mid, tri, preferred_element_type=F32)
           + jnp.dot(lo, tri, preferred_element_type=F32)) + carry_ref[:, 0:1]
    c_ref[0] = tot
    carry_ref[...] = jnp.broadcast_to(tot[:, tc - 1:tc], carry_ref.shape)


def forget_cumsum(xt, bias, apply_ls, tc):
    b, h, l = xt.shape
    blk = pl.BlockSpec((1, h, tc), lambda i, j: (i, 0, j))
    return pl.pallas_call(
        functools.partial(_cumsum_kernel, apply_ls=apply_ls),
        out_shape=[jax.ShapeDtypeStruct((b, h, l), F32)] * 2,
        grid=(b, l // tc),
        in_specs=[blk, pl.BlockSpec((h, 1), lambda i, j: (0, 0))],
        out_specs=[blk, blk],
        scratch_shapes=[pltpu.VMEM((h, LANES), F32)],
        compiler_params=_params(("parallel", "arbitrary")),
        name="forget_cumsum",
    )(xt, bias)


def _hgrn_kernel(hin_ref, s0_ref, lbp_ref, gain_ref, out_ref, sfin_ref,
                 st_ref, q_ref, k_ref, b_ref, v_ref, *, layer):
    t = pl.program_id(1)
    nt = pl.num_programs(1)
    c = HG_CHUNK
    nsub = c // HG_SUB
    rows_in = hin_ref.shape[1]

    def pad_rows(a):
        if rows_in == c:
            return a
        return jnp.concatenate([a, jnp.zeros((c - rows_in, a.shape[1]), a.dtype)], axis=0)

    @pl.when(t == 0)
    def _():
        for h in range(HG_HEADS):
            st_ref[h] = s0_ref[0, h].T

    lbp = lbp_ref[...]
    e = jnp.exp(lbp - jnp.max(lbp, axis=0, keepdims=True))
    p = e / jnp.sum(e, axis=0, keepdims=True)
    acc = p[0:1]
    first = acc
    for i in range(1, layer + 1):
        acc = acc + p[i:i + 1]
    lb_all = acc - first

    ri = lax.broadcasted_iota(jnp.int32, (c, c), 0)
    ci = lax.broadcasted_iota(jnp.int32, (c, c), 1)
    tril = jnp.where(ci <= ri, 1.0, 0.0).astype(BF16)
    ones = jnp.ones((HG_DIM, HG_DIM), BF16)
    srow = lax.broadcasted_iota(jnp.int32, (HG_SUB, HG_DIM), 0)

    for h in range(HG_HEADS):
        cols = slice(h * HG_DIM, (h + 1) * HG_DIM)
        lb = lb_all[:, cols]
        hq = hin_ref[0, :, h * HG_DIM:(h + 1) * HG_DIM]
        hf = hin_ref[0, :, HG_WIDTH + h * HG_DIM:HG_WIDTH + (h + 1) * HG_DIM]
        hi = hin_ref[0, :, 2 * HG_WIDTH + h * HG_DIM:2 * HG_WIDTH + (h + 1) * HG_DIM]
        hg = hin_ref[0, :, 3 * HG_WIDTH + h * HG_DIM:3 * HG_WIDTH + (h + 1) * HG_DIM]

        q = pad_rows(_silu(hq) * (HG_DIM ** -0.5))
        a1 = jnp.log(lb)
        a2 = jnp.log1p(-lb) + _log_sigmoid(hf)
        mx = jnp.maximum(a1, a2)
        lf = pad_rows(mx + jnp.log(jnp.exp(a1 - mx) + jnp.exp(a2 - mx)))
        kk = pad_rows((1.0 - lb) * jax.nn.sigmoid(-hf))
        v = pad_rows(hi)

        bh, bm, bl = _split3(lf)
        b = (jnp.dot(tril, bh, preferred_element_type=F32)
             + jnp.dot(tril, bm, preferred_element_type=F32)
             + jnp.dot(tril, bl, preferred_element_type=F32))
        q_ref[...] = q
        k_ref[...] = kk
        b_ref[...] = b
        v_ref[...] = v
        vb = v.astype(BF16)

        st = st_ref[h]
        o_blocks = []
        khat = None
        r_prev = jnp.zeros((1, HG_DIM), F32)
        qhat_blocks = []
        for i in range(nsub):
            r0 = i * HG_SUB
            qb = q_ref[r0:r0 + HG_SUB, :]
            kb = k_ref[r0:r0 + HG_SUB, :]
            bb = b_ref[r0:r0 + HG_SUB, :]
            r_next = b_ref[r0 + HG_SUB - 1:r0 + HG_SUB, :]

            slabs = []
            for s in range(HG_SUB):
                ks = k_ref[r0 + s:r0 + s + 1, :]
                bs = b_ref[r0 + s:r0 + s + 1, :]
                dec = jnp.exp(jnp.where(srow >= s, bb - bs, NEG))
                slabs.append((qb * ks) * dec)
            pm = jnp.concatenate(slabs, axis=0).astype(BF16)
            red = jnp.dot(pm, ones, preferred_element_type=F32)
            o_i = jnp.zeros((HG_SUB, HG_DIM), F32)
            for s in range(HG_SUB):
                o_i = o_i + red[s * HG_SUB:(s + 1) * HG_SUB, :] * v_ref[r0 + s:r0 + s + 1, :]

            qt = qb * jnp.exp(bb - r_prev)
            if khat is not None:
                att = lax.dot_general(qt.astype(BF16), khat.astype(BF16),
                                      (((1,), (1,)), ((), ())), preferred_element_type=F32)
                o_i = o_i + jnp.dot(att.astype(BF16), vb[:r0, :], preferred_element_type=F32)
            qhat_blocks.append(qt * jnp.exp(r_prev))
            o_blocks.append(o_i)

            kt = kb * jnp.exp(r_next - bb)
            if khat is None:
                khat = kt
            else:
                khat = jnp.concatenate([khat * jnp.exp(r_next - r_prev), kt], axis=0)
            r_prev = r_next

        qhat = jnp.concatenate(qhat_blocks, axis=0).astype(BF16)
        o = jnp.concatenate(o_blocks, axis=0)
        o = o + lax.dot_general(qhat, st.astype(BF16), (((1,), (1,)), ((), ())),
                                preferred_element_type=F32)
        st_new = st * jnp.exp(r_prev) + jnp.dot(v.T.astype(BF16), khat.astype(BF16),
                                                preferred_element_type=F32)
        st_ref[h] = st_new

        o = o[:rows_in]
        o = o * lax.rsqrt(jnp.mean(o * o, axis=-1, keepdims=True) + EPS)
        o = o * gain_ref[:, cols] * _silu(hg)
        out_ref[0, :, cols] = o.astype(out_ref.dtype)

    @pl.when(t == nt - 1)
    def _():
        for h in range(HG_HEADS):
            sfin_ref[0, h] = st_ref[h].T


def hgrn(hin, s0, lbp, gain, layer):
    b, l, w = hin.shape
    c = min(HG_CHUNK, l)
    assert l % c == 0 and c % HG_SUB == 0
    nt = l // c
    return pl.pallas_call(
        functools.partial(_hgrn_kernel, layer=layer),
        out_shape=[jax.ShapeDtypeStruct((b, l, HG_WIDTH), BF16),
                   jax.ShapeDtypeStruct((b, HG_HEADS, HG_DIM, HG_DIM), F32)],
        grid=(b, nt),
        in_specs=[pl.BlockSpec((1, c, w), lambda i, j: (i, j, 0)),
                  pl.BlockSpec((1, HG_HEADS, HG_DIM, HG_DIM), lambda i, j: (i, 0, 0, 0)),
                  pl.BlockSpec(lbp.shape, lambda i, j: (0, 0)),
                  pl.BlockSpec((1, HG_WIDTH), lambda i, j: (0, 0))],
        out_specs=[pl.BlockSpec((1, c, HG_WIDTH), lambda i, j: (i, j, 0)),
                   pl.BlockSpec((1, HG_HEADS, HG_DIM, HG_DIM), lambda i, j: (i, 0, 0, 0))],
        scratch_shapes=[pltpu.VMEM((HG_HEADS, HG_DIM, HG_DIM), F32)]
        + [pltpu.VMEM((HG_CHUNK, HG_DIM), F32)] * 4,
        compiler_params=_params(("parallel", "arbitrary")),
        name="hgrn",
    )(hin, s0, lbp, gain)


def _fox_prompt_kernel(q_ref, k_ref, v_ref, cq_ref, ck_ref, o_ref, m_ref, l_ref, acc_ref):
    i = pl.program_id(1)
    j = pl.program_id(2)
    tq = q_ref.shape[1]
    tk = k_ref.shape[1]

    @pl.when(j == 0)
    def _():
        m_ref[...] = jnp.full_like(m_ref, NEG)
        l_ref[...] = jnp.zeros_like(l_ref)
        acc_ref[...] = jnp.zeros_like(acc_ref)

    @pl.when(j <= i)
    def _():
        qpos = i * tq + lax.broadcasted_iota(jnp.int32, (tq, tk), 0)
        kpos = j * tk + lax.broadcasted_iota(jnp.int32, (tq, tk), 1)
        keep = kpos <= qpos
        for h in range(FOX_HEADS):
            cols = slice(h * FOX_DIM, (h + 1) * FOX_DIM)
            s = lax.dot_general(q_ref[0, :, cols], k_ref[0, :, cols],
                                (((1,), (1,)), ((), ())), preferred_element_type=F32)
            s = s + cq_ref[0, :, h:h + 1] - ck_ref[0, h:h + 1, :]
            s = jnp.where(keep, s, NEG)
            m_old = m_ref[h]
            m_new = jnp.maximum(m_old, jnp.max(s, axis=-1, keepdims=True))
            alpha = jnp.exp(m_old - m_new)
            p = jnp.exp(s - m_new)
            l_ref[h] = alpha * l_ref[h] + jnp.sum(p, axis=-1, keepdims=True)
            acc_ref[:, cols] = alpha * acc_ref[:, cols] + jnp.dot(
                p.astype(BF16), v_ref[0, :, cols], preferred_element_type=F32)
            m_ref[h] = m_new

    @pl.when(j == pl.num_programs(2) - 1)
    def _():
        for h in range(FOX_HEADS):
            cols = slice(h * FOX_DIM, (h + 1) * FOX_DIM)
            o_ref[0, :, cols] = (acc_ref[:, cols] / l_ref[h]).astype(o_ref.dtype)


def fox_prompt(q, k, v, cq, ckt, tq):
    b, l, w = q.shape
    nq = l // tq
    kv_map = lambda bi, i, j: (bi, jnp.minimum(i, j), 0)
    return pl.pallas_call(
        _fox_prompt_kernel,
        out_shape=jax.ShapeDtypeStruct((b, l, w), BF16),
        grid=(b, nq, nq),
        in_specs=[pl.BlockSpec((1, tq, w), lambda bi, i, j: (bi, i, 0)),
                  pl.BlockSpec((1, tq, w), kv_map),
                  pl.BlockSpec((1, tq, w), kv_map),
                  pl.BlockSpec((1, tq, FOX_HEADS), lambda bi, i, j: (bi, i, 0)),
                  pl.BlockSpec((1, FOX_HEADS, tq), lambda bi, i, j: (bi, 0, jnp.minimum(i, j)))],
        out_specs=pl.BlockSpec((1, tq, w), lambda bi, i, j: (bi, i, 0)),
        scratch_shapes=[pltpu.VMEM((FOX_HEADS, tq, 1), F32),
                        pltpu.VMEM((FOX_HEADS, tq, 1), F32),
                        pltpu.VMEM((tq, w), F32)],
        compiler_params=_params(("parallel", "parallel", "arbitrary")),
        name="fox_prompt",
    )(q, k, v, cq, ckt)


def _fox_sample_kernel(q_ref, kp_ref, vp_ref, kn_ref, vn_ref, cq_ref, ck_ref, o_ref):
    lq = q_ref.shape[1]
    past = kp_ref.shape[1]
    ri = lax.broadcasted_iota(jnp.int32, (lq, lq), 0)
    ci = lax.broadcasted_iota(jnp.int32, (lq, lq), 1)
    nt = (((1,), (1,)), ((), ()))
    for h in range(FOX_HEADS):
        cols = slice(h * FOX_DIM, (h + 1) * FOX_DIM)
        q = q_ref[0, :, cols]
        cq = cq_ref[0, :, h:h + 1]
        sp = lax.dot_general(q, kp_ref[0, :, cols].astype(BF16), nt, preferred_element_type=F32)
        sp = sp + cq - ck_ref[0, h:h + 1, 0:past]
        sn = lax.dot_general(q, kn_ref[0, :, cols], nt, preferred_element_type=F32)
        sn = sn + cq - ck_ref[0, h:h + 1, past:past + lq]
        sn = jnp.where(ci <= ri, sn, NEG)
        m = jnp.maximum(jnp.max(sp, axis=-1, keepdims=True), jnp.max(sn, axis=-1, keepdims=True))
        pp = jnp.exp(sp - m)
        pn = jnp.exp(sn - m)
        den = jnp.sum(pp, axis=-1, keepdims=True) + jnp.sum(pn, axis=-1, keepdims=True)
        o = (jnp.dot(pp.astype(BF16), vp_ref[0, :, cols].astype(BF16), preferred_element_type=F32)
             + jnp.dot(pn.astype(BF16), vn_ref[0, :, cols], preferred_element_type=F32))
        o_ref[0, :, cols] = (o / den).astype(o_ref.dtype)


def fox_sample(q, kp, vp, kn, vn, cq, ckt):
    b, lq, w = q.shape
    past = kp.shape[1]
    lc = ckt.shape[2]
    new = lambda i: (i, 0, 0)
    return pl.pallas_call(
        _fox_sample_kernel,
        out_shape=jax.ShapeDtypeStruct((b, lq, w), BF16),
        grid=(b,),
        in_specs=[pl.BlockSpec((1, lq, w), new),
                  pl.BlockSpec((1, past, w), new),
                  pl.BlockSpec((1, past, w), new),
                  pl.BlockSpec((1, lq, w), new),
                  pl.BlockSpec((1, lq, w), new),
                  pl.BlockSpec((1, lq, FOX_HEADS), new),
                  pl.BlockSpec((1, FOX_HEADS, lc), new)],
        out_specs=pl.BlockSpec((1, lq, w), new),
        compiler_params=_params(("parallel",)),
        name="fox_sample",
    )(q, kp, vp, kn, vn, cq, ckt)


def _outproj_kernel(x_ref, hg_ref, fo_ref, mod_ref, fgain_ref, w_ref, gain2_ref,
                    wrh_ref, wrl_ref, br_ref, xo_ref, h2_ref, gt_ref):
    s, ls, d = x_ref.shape
    tm = s * ls
    fo = fo_ref[...].astype(F32)
    fn = fo * lax.rsqrt(jnp.mean(fo * fo, axis=-1, keepdims=True) + EPS) * fgain_ref[...]
    mixed = (jnp.dot(hg_ref[...].reshape(tm, HG_WIDTH), w_ref[0:HG_WIDTH, :],
                     preferred_element_type=F32)
             + jnp.dot(fn.reshape(tm, FOX_WIDTH).astype(BF16), w_ref[HG_WIDTH:, :],
                       preferred_element_type=F32))
    x = x_ref[...] + mod_ref[:, 2:3, :] * mixed.reshape(s, ls, d)
    xo_ref[...] = x
    y = x * lax.rsqrt(jnp.mean(x * x, axis=-1, keepdims=True) + EPS) * gain2_ref[...]
    h2 = (y * (1.0 + mod_ref[:, 4:5, :]) + mod_ref[:, 3:4, :]).reshape(tm, d)
    h2b = h2.astype(BF16)
    h2_ref[...] = h2b.reshape(s, ls, d)

    nt = (((1,), (1,)), ((), ()))
    logits = lax.dot_general(wrh_ref[...], h2b, nt, preferred_element_type=F32) \
        + lax.dot_general(wrl_ref[...], h2b, nt, preferred_element_type=F32)
    z = jnp.exp(logits - jnp.max(logits, axis=0, keepdims=True))
    probs = z / jnp.sum(z, axis=0, keepdims=True)
    sel = probs + br_ref[...]
    rows = [sel[e:e + 1, :] for e in range(N_EXPERTS)]
    prow = [probs[e:e + 1, :] for e in range(N_EXPERTS)]

    def beats(a, ia, b_, ib):
        return jnp.where(a >= b_, 1.0, 0.0) if ia < ib else jnp.where(a > b_, 1.0, 0.0)

    top = []
    gscore = []
    for g in range(N_GROUPS):
        ids = range(g * GROUP_SIZE, (g + 1) * GROUP_SIZE)
        sc = jnp.zeros_like(rows[0])
        for e in ids:
            cnt = jnp.zeros_like(rows[0])
            for o in ids:
                if o != e:
                    cnt = cnt + beats(rows[o], o, rows[e], e)
            flag = jnp.where(cnt < TOP_K, 1.0, 0.0)
            top.append(flag)
            sc = sc + flag * rows[e]
        gscore.append(sc)
    chosen = []
    for g in range(N_GROUPS):
        cnt = jnp.zeros_like(rows[0])
        for o in range(N_GROUPS):
            if o != g:
                cnt = cnt + beats(gscore[o], o, gscore[g], g)
        chosen.append(jnp.where(cnt < 1.0, 1.0, 0.0))
    wts = [prow[e] * top[e] * chosen[e // GROUP_SIZE] for e in range(N_EXPERTS)]
    den = wts[0]
    for e in range(1, N_EXPERTS):
        den = den + wts[e]
    gt_ref[0] = jnp.concatenate([w / den for w in wts], axis=0)


def outproj(x, hg_out, fox_o, mod, fox_gain, w_out, gain2, wr_hi, wr_lo, b_router, s, ls):
    b, l, d = x.shape
    nblk = (b // s) * (l // ls)
    nl = l // ls
    tm = s * ls
    row = lambda i, j: (i, j, 0)
    const = lambda i, j: (0, 0)
    return pl.pallas_call(
        _outproj_kernel,
        out_shape=[jax.ShapeDtypeStruct((b, l, d), F32),
                   jax.ShapeDtypeStruct((b, l, d), BF16),
                   jax.ShapeDtypeStruct((nblk, N_EXPERTS, tm), F32)],
        grid=(b // s, nl),
        in_specs=[pl.BlockSpec((s, ls, d), row),
                  pl.BlockSpec((s, ls, HG_WIDTH), row),
                  pl.BlockSpec((s, ls, FOX_WIDTH), row),
                  pl.BlockSpec((s, N_MOD, d), lambda i, j: (i, 0, 0)),
                  pl.BlockSpec((1, FOX_WIDTH), const),
                  pl.BlockSpec(w_out.shape, const, pipeline_mode=pl.Buffered(1)),
                  pl.BlockSpec((1, d), const),
                  pl.BlockSpec((N_EXPERTS, d), const),
                  pl.BlockSpec((N_EXPERTS, d), const),
                  pl.BlockSpec((N_EXPERTS, 1), const)],
        out_specs=[pl.BlockSpec((s, ls, d), row),
                   pl.BlockSpec((s, ls, d), row),
                   pl.BlockSpec((1, N_EXPERTS, tm), lambda i, j: (i * nl + j, 0, 0))],
        compiler_params=_params(("parallel", "parallel")),
        name="outproj",
    )(x, hg_out, fox_o, mod, fox_gain, w_out, gain2, wr_hi, wr_lo, b_router)


def _moe_kernel(x_ref, h_ref, g_ref, mod_ref, wg_ref, wu_ref, wd_ref, fgain_ref, o_ref, acc_ref,
                *, final):
    e = pl.program_id(2)
    s, ls, d = x_ref.shape
    tm = s * ls

    @pl.when(e == 0)
    def _():
        acc_ref[...] = jnp.zeros_like(acc_ref)

    h = h_ref[...].reshape(tm, d)
    a = jnp.dot(h, wg_ref[0], preferred_element_type=F32)
    u = jnp.dot(h, wu_ref[0], preferred_element_type=F32)
    lane = lax.broadcasted_iota(jnp.int32, (tm, N_EXPERTS), 1)
    gate = jnp.sum(jnp.where(lane == e, g_ref[0], 0.0), axis=-1, keepdims=True)
    he = (_silu(a) * u * gate).astype(BF16)
    acc_ref[...] += jnp.dot(he, wd_ref[0], preferred_element_type=F32)

    @pl.when(e == pl.num_programs(2) - 1)
    def _():
        x = x_ref[...] + mod_ref[:, 5:6, :] * acc_ref[...].reshape(s, ls, d)
        if final:
            x = x * lax.rsqrt(jnp.mean(x * x, axis=-1, keepdims=True) + EPS) * fgain_ref[...]
        o_ref[...] = x


def moe(x, h2, gates, mod, wg, wu, wd, final_gain, s, ls, final):
    b, l, d = x.shape
    nl = l // ls
    tm = s * ls
    de = wg.shape[2]
    row = lambda i, j, e: (i, j, 0)
    return pl.pallas_call(
        functools.partial(_moe_kernel, final=final),
        out_shape=jax.ShapeDtypeStruct((b, l, d), F32),
        grid=(b // s, nl, N_EXPERTS),
        in_specs=[pl.BlockSpec((s, ls, d), row),
                  pl.BlockSpec((s, ls, d), row),
                  pl.BlockSpec((1, tm, N_EXPERTS), lambda i, j, e: (i * nl + j, 0, 0)),
                  pl.BlockSpec((s, N_MOD, d), lambda i, j, e: (i, 0, 0)),
                  pl.BlockSpec((1, d, de), lambda i, j, e: (e, 0, 0)),
                  pl.BlockSpec((1, d, de), lambda i, j, e: (e, 0, 0)),
                  pl.BlockSpec((1, de, d), lambda i, j, e: (e, 0, 0)),
                  pl.BlockSpec((1, d), lambda i, j, e: (0, 0))],
        out_specs=pl.BlockSpec((s, ls, d), row),
        scratch_shapes=[pltpu.VMEM((tm, d), F32)],
        compiler_params=_params(("parallel", "parallel", "arbitrary")),
        name="moe",
    )(x, h2, gates, mod, wg, wu, wd, final_gain)


def _tile(b, l):
    ls = min(l, 512)
    return max(1, 512 // ls), ls


def _layer(x, mod, l, p, hg_state0, fox_past, final):
    b, seq, d = x.shape
    s, ls = _tile(b, seq)
    hg_in, fq, fk32, fv32, fk16, fv16, ff = inproj(x, mod, p['norm_mix_gain'][l], p['w_in'][l], s, ls)

    hg_out, hg_state = hgrn(hg_in, hg_state0, p['hg_lower_bounds'], p['hg_norm_gain'][l], l)

    fft = jnp.swapaxes(ff[:, :, :FOX_HEADS], 1, 2)
    if fox_past is None:
        logft, ct = forget_cumsum(fft, p['fox_f_bias'][l], True, min(seq, 512))
        cq = jnp.swapaxes(ct, 1, 2)
        fox_o = fox_prompt(fq, fk16, fv16, cq, ct, min(seq, 512))
    else:
        pk, pv, plogf = fox_past
        past = pk.shape[1]
        logft, _ = forget_cumsum(fft, p['fox_f_bias'][l], True, seq)
        tot = past + seq
        pad = (-tot) % LANES
        allt = jnp.concatenate([jnp.swapaxes(plogf, 1, 2), logft,
                                jnp.zeros((b, FOX_HEADS, pad), F32)], axis=2)
        _, ct = forget_cumsum(allt, p['fox_f_bias'][l], False, LANES)
        cq = jnp.swapaxes(ct[:, :, past:tot], 1, 2)
        fox_o = fox_sample(fq, pk.reshape(b, past, FOX_WIDTH), pv.reshape(b, past, FOX_WIDTH),
                           fk16, fv16, cq, ct)
    logf = jnp.swapaxes(logft, 1, 2)

    x_mid, h2, gates_t = outproj(x, hg_out, fox_o, mod, p['fox_out_gain'][l], p['w_out'][l],
                                 p['norm_ffn_gain'][l], p['wr_hi'], p['wr_lo'], p['b_router'], s, ls)
    gates = jnp.swapaxes(gates_t, 1, 2)
    x_new = moe(x_mid, h2, gates, mod, p['w_exp_gate'][l], p['w_exp_up'][l], p['w_exp_down'][l],
                p['final_norm_gain'], s, ls, final)
    k_new = fk32.reshape(b, seq, FOX_HEADS, FOX_DIM)
    v_new = fv32.reshape(b, seq, FOX_HEADS, FOX_DIM)
    return x_new, hg_state, k_new, v_new, logf


def kernel(x_prompt, x_sample, cache_fox_k, cache_fox_v, cache_fox_logf, state_hgrn, c_prompt, c_sample,
           norm_mix_gain, norm_ffn_gain, w_ada, b_ada, w_in, hg_lower_bounds, hg_norm_gain,
           fox_f_bias, fox_out_gain, w_out, w_router, b_router, w_exp_gate, w_exp_up, w_exp_down,
           final_norm_gain):
    depth, d = norm_mix_gain.shape
    bp = x_prompt.shape[0]
    n_in = w_in.shape[2]
    n_pad = 4 * HG_WIDTH + 3 * FOX_WIDTH + LANES - n_in
    wr_t = w_router.T
    wr_hi = wr_t.astype(BF16)
    p = {
        'norm_mix_gain': norm_mix_gain.reshape(depth, 1, d),
        'norm_ffn_gain': norm_ffn_gain.reshape(depth, 1, d),
        'w_in': jnp.pad(w_in, ((0, 0), (0, 0), (0, n_pad))).astype(BF16),
        'hg_lower_bounds': hg_lower_bounds,
        'hg_norm_gain': hg_norm_gain.reshape(depth, 1, HG_WIDTH),
        'fox_f_bias': fox_f_bias.reshape(depth, FOX_HEADS, 1),
        'fox_out_gain': fox_out_gain.reshape(depth, 1, FOX_WIDTH),
        'w_out': w_out.astype(BF16),
        'wr_hi': wr_hi,
        'wr_lo': (wr_t - wr_hi.astype(F32)).astype(BF16),
        'b_router': b_router.reshape(N_EXPERTS, 1),
        'w_exp_gate': w_exp_gate.astype(BF16),
        'w_exp_up': w_exp_up.astype(BF16),
        'w_exp_down': w_exp_down.astype(BF16),
        'final_norm_gain': final_norm_gain.reshape(1, d),
    }
    mods = ada_mod(jnp.concatenate([c_prompt, c_sample], axis=0), w_ada, b_ada)
    mods = mods.reshape(depth, -1, N_MOD, d)

    xp, xs = x_prompt, x_sample
    outs_p, outs_s = [], []
    zero_state = jnp.zeros((bp, HG_HEADS, HG_DIM, HG_DIM), F32)
    for l in range(depth):
        final = l == depth - 1
        xp, *rest_p = _layer(xp, mods[l, :bp], l, p, zero_state, None, final)
        xs, *rest_s = _layer(xs, mods[l, bp:], l, p, state_hgrn[l],
                             (cache_fox_k[l], cache_fox_v[l], cache_fox_logf[l]), final)
        outs_p.append(rest_p)
        outs_s.append(rest_s)
    stack = lambda outs, i: jnp.stack([o[i] for o in outs])
    return (xp, xs,
            stack(outs_p, 0), stack(outs_p, 1), stack(outs_p, 2), stack(outs_p, 3),
            stack(outs_s, 0), stack(outs_s, 1), stack(outs_s, 2), stack(outs_s, 3))
```

```python
import functools
import math

import numpy as np
import jax
import jax.numpy as jnp
from jax import lax
from jax.experimental import pallas as pl
from jax.experimental.pallas import tpu as pltpu

HG_HEADS = 4
HG_DIM = 128
HG_WIDTH = HG_HEADS * HG_DIM
FOX_HEADS = 8
FOX_DIM = 64
FOX_WIDTH = FOX_HEADS * FOX_DIM
N_EXPERTS = 16
N_GROUPS = 4
GROUP_SIZE = N_EXPERTS // N_GROUPS
TOP_K = 2
N_MOD = 6
EPS = 1e-6

LANES = 128
HG_CHUNK = 128
HG_SUB = 16
VMEM_LIMIT = 56 * 1024 * 1024

F32 = jnp.float32
BF16 = jnp.bfloat16
NEG = -1e30
LOG2E = math.log2(math.e)


def _params(sem, vmem=VMEM_LIMIT):
    return pltpu.CompilerParams(dimension_semantics=sem, vmem_limit_bytes=vmem)


def _split3(x):
    hi = x.astype(BF16)
    r1 = x - hi.astype(F32)
    mid = r1.astype(BF16)
    lo = (r1 - mid.astype(F32)).astype(BF16)
    return hi, mid, lo


def _log_sigmoid(x):
    return jnp.minimum(x, 0.0) - jnp.log1p(jnp.exp(-jnp.abs(x)))


def _silu(x):
    return x * jax.nn.sigmoid(x)


def _ada_kernel(c_ref, w_ref, b_ref, o_ref):
    s = _silu(c_ref[...]).astype(BF16)
    o_ref[0] = jnp.dot(s, w_ref[0].astype(BF16), preferred_element_type=F32) + b_ref[0]


def ada_mod(c, w_ada, b_ada):
    depth, d, n = w_ada.shape
    nb = c.shape[0]
    tn = 1536
    return pl.pallas_call(
        _ada_kernel,
        out_shape=jax.ShapeDtypeStruct((depth, nb, n), F32),
        grid=(depth, n // tn),
        in_specs=[pl.BlockSpec((nb, d), lambda l, j: (0, 0)),
                  pl.BlockSpec((1, d, tn), lambda l, j: (l, 0, j)),
                  pl.BlockSpec((1, 1, tn), lambda l, j: (l, 0, j))],
        out_specs=pl.BlockSpec((1, nb, tn), lambda l, j: (l, 0, j)),
        compiler_params=_params(("arbitrary", "arbitrary")),
        name="ada_mod",
    )(c, w_ada, b_ada.reshape(depth, 1, n))


def _inproj_kernel(x_ref, mod_ref, gain_ref, w_ref,
                   hg_ref, fq_ref, fk32_ref, fv32_ref, fk16_ref, fv16_ref, ff_ref):
    s, ls, d = x_ref.shape
    x = x_ref[...]
    y = x * lax.rsqrt(jnp.mean(x * x, axis=-1, keepdims=True) + EPS) * gain_ref[...]
    h = y * (1.0 + mod_ref[:, 1:2, :]) + mod_ref[:, 0:1, :]
    hb = h.reshape(s * ls, d).astype(BF16)

    def proj(lo, hi):
        return jnp.dot(hb, w_ref[:, lo:hi], preferred_element_type=F32)

    c0 = 4 * HG_WIDTH
    hg_ref[...] = proj(0, c0).reshape(s, ls, c0)
    fq = proj(c0, c0 + FOX_WIDTH) * (FOX_DIM ** -0.5 * LOG2E)
    fq_ref[...] = fq.reshape(s, ls, FOX_WIDTH).astype(BF16)
    fk = proj(c0 + FOX_WIDTH, c0 + 2 * FOX_WIDTH).reshape(s, ls, FOX_WIDTH)
    fk32_ref[...] = fk
    fk16_ref[...] = fk.astype(BF16)
    fv = proj(c0 + 2 * FOX_WIDTH, c0 + 3 * FOX_WIDTH).reshape(s, ls, FOX_WIDTH)
    fv32_ref[...] = fv
    fv16_ref[...] = fv.astype(BF16)
    ff_ref[...] = proj(c0 + 3 * FOX_WIDTH, c0 + 3 * FOX_WIDTH + LANES).reshape(s, ls, LANES)


def inproj(x, mod, gain, w_pad, s, ls):
    b, l, d = x.shape
    n = w_pad.shape[1]
    row = lambda i, j: (i, j, 0)
    outs = [(4 * HG_WIDTH, F32), (FOX_WIDTH, BF16), (FOX_WIDTH, F32), (FOX_WIDTH, F32),
            (FOX_WIDTH, BF16), (FOX_WIDTH, BF16), (LANES, F32)]
    return pl.pallas_call(
        _inproj_kernel,
        out_shape=[jax.ShapeDtypeStruct((b, l, w), dt) for w, dt in outs],
        grid=(b // s, l // ls),
        in_specs=[pl.BlockSpec((s, ls, d), row),
                  pl.BlockSpec((s, N_MOD, d), lambda i, j: (i, 0, 0)),
                  pl.BlockSpec((1, d), lambda i, j: (0, 0)),
                  pl.BlockSpec((d, n), lambda i, j: (0, 0), pipeline_mode=pl.Buffered(1))],
        out_specs=[pl.BlockSpec((s, ls, w), row) for w, _ in outs],
        compiler_params=_params(("parallel", "parallel")),
        name="inproj",
    )(x, mod, gain, w_pad)


def _cumsum_kernel(x_ref, bias_ref, lf_ref, c_ref, carry_ref, *, apply_ls):
    @pl.when(pl.program_id(1) == 0)
    def _():
        carry_ref[...] = jnp.zeros_like(carry_ref)

    x = x_ref[0]
    tc = x.shape[-1]
    lf = _log_sigmoid(x + bias_ref[...]) if apply_ls else x
    lf_ref[0] = lf
    r = lax.broadcasted_iota(jnp.int32, (tc, tc), 0)
    c = lax.broadcasted_iota(jnp.int32, (tc, tc), 1)
    tri = jnp.where(r <= c, 1.0, 0.0).astype(BF16)
    hi, mid, lo = _split3(lf)
    tot = (jnp.dot(hi, tri, preferred_element_type=F32)
           + jnp.dot(mid, tri, preferred_element_type=F32)
           + jnp.dot(lo, tri, preferred_element_type=F32)) + carry_ref[:, 0:1]
    c_ref[0] = tot
    carry_ref[...] = jnp.broadcast_to(tot[:, tc - 1:tc], carry_ref.shape)


def forget_cumsum(xt, bias, apply_ls, tc):
    b, h, l = xt.shape
    blk = pl.BlockSpec((1, h, tc), lambda i, j: (i, 0, j))
    return pl.pallas_call(
        functools.partial(_cumsum_kernel, apply_ls=apply_ls),
        out_shape=[jax.ShapeDtypeStruct((b, h, l), F32)] * 2,
        grid=(b, l // tc),
        in_specs=[blk, pl.BlockSpec((h, 1), lambda i, j: (0, 0))],
        out_specs=[blk, blk],
        scratch_shapes=[pltpu.VMEM((h, LANES), F32)],
        compiler_params=_params(("parallel", "arbitrary")),
        name="forget_cumsum",
    )(xt, bias)


def _hgrn_kernel(hin_ref, s0_ref, lbp_ref, gain_ref, out_ref, sfin_ref,
                 st_ref, q_ref, k_ref, b_ref, v_ref, *, layer):
    t = pl.program_id(1)
    nt = pl.num_programs(1)
    c = HG_CHUNK
    nsub = c // HG_SUB
    rows_in = hin_ref.shape[1]

    def pad_rows(a):
        if rows_in == c:
            return a
        return jnp.concatenate([a, jnp.zeros((c - rows_in, a.shape[1]), a.dtype)], axis=0)

    @pl.when(t == 0)
    def _():
        for h in range(HG_HEADS):
            st_ref[h] = s0_ref[0, h].T

    lbp = lbp_ref[...]
    e = jnp.exp(lbp - jnp.max(lbp, axis=0, keepdims=True))
    p = e / jnp.sum(e, axis=0, keepdims=True)
    acc = p[0:1]
    first = acc
    for i in range(1, layer + 1):
        acc = acc + p[i:i + 1]
    lb_all = acc - first

    ri = lax.broadcasted_iota(jnp.int32, (c, c), 0)
    ci = lax.broadcasted_iota(jnp.int32, (c, c), 1)
    tril = jnp.where(ci <= ri, 1.0, 0.0).astype(BF16)
    ones = jnp.ones((HG_DIM, HG_DIM), BF16)
    srow = lax.broadcasted_iota(jnp.int32, (HG_SUB, HG_DIM), 0)

    for h in range(HG_HEADS):
        cols = slice(h * HG_DIM, (h + 1) * HG_DIM)
        lb = lb_all[:, cols]
        hq = hin_ref[0, :, h * HG_DIM:(h + 1) * HG_DIM]
        hf = hin_ref[0, :, HG_WIDTH + h * HG_DIM:HG_WIDTH + (h + 1) * HG_DIM]
        hi = hin_ref[0, :, 2 * HG_WIDTH + h * HG_DIM:2 * HG_WIDTH + (h + 1) * HG_DIM]
        hg = hin_ref[0, :, 3 * HG_WIDTH + h * HG_DIM:3 * HG_WIDTH + (h + 1) * HG_DIM]

        q = pad_rows(_silu(hq) * (HG_DIM ** -0.5))
        a1 = jnp.log(lb)
        a2 = jnp.log1p(-lb) + _log_sigmoid(hf)
        mx = jnp.maximum(a1, a2)
        lf = pad_rows(mx + jnp.log(jnp.exp(a1 - mx) + jnp.exp(a2 - mx)))
        kk = pad_rows((1.0 - lb) * jax.nn.sigmoid(-hf))
        v = pad_rows(hi)

        bh, bm, bl = _split3(lf)
        b = (jnp.dot(tril, bh, preferred_element_type=F32)
             + jnp.dot(tril, bm, preferred_element_type=F32)
             + jnp.dot(tril, bl, preferred_element_type=F32))
        q_ref[...] = q
        k_ref[...] = kk
        b_ref[...] = b
        v_ref[...] = v
        vb = v.astype(BF16)

        st = st_ref[h]
        o_blocks = []
        khat = None
        r_prev = jnp.zeros((1, HG_DIM), F32)
        qhat_blocks = []
        for i in range(nsub):
            r0 = i * HG_SUB
            qb = q_ref[r0:r0 + HG_SUB, :]
            kb = k_ref[r0:r0 + HG_SUB, :]
            bb = b_ref[r0:r0 + HG_SUB, :]
            r_next = b_ref[r0 + HG_SUB - 1:r0 + HG_SUB, :]

            slabs = []
            for s in range(HG_SUB):
                ks = k_ref[r0 + s:r0 + s + 1, :]
                bs = b_ref[r0 + s:r0 + s + 1, :]
                dec = jnp.exp(jnp.where(srow >= s, bb - bs, NEG))
                slabs.append((qb * ks) * dec)
            pm = jnp.concatenate(slabs, axis=0).astype(BF16)
            red = jnp.dot(pm, ones, preferred_element_type=F32)
            o_i = jnp.zeros((HG_SUB, HG_DIM), F32)
            for s in range(HG_SUB):
                o_i = o_i + red[s * HG_SUB:(s + 1) * HG_SUB, :] * v_ref[r0 + s:r0 + s + 1, :]

            qt = qb * jnp.exp(bb - r_prev)
            if khat is not None:
                att = lax.dot_general(qt.astype(BF16), khat.astype(BF16),
                                      (((1,), (1,)), ((), ())), preferred_element_type=F32)
                o_i = o_i + jnp.dot(att.astype(BF16), vb[:r0, :], preferred_element_type=F32)
            qhat_blocks.append(qt * jnp.exp(r_prev))
            o_blocks.append(o_i)

            kt = kb * jnp.exp(r_next - bb)
            if khat is None:
                khat = kt
            else:
                khat = jnp.concatenate([khat * jnp.exp(r_next - r_prev), kt], axis=0)
            r_prev = r_next

        qhat = jnp.concatenate(qhat_blocks, axis=0).astype(BF16)
        o = jnp.concatenate(o_blocks, axis=0)
        o = o + lax.dot_general(qhat, st.astype(BF16), (((1,), (1,)), ((), ())),
                                preferred_element_type=F32)
        st_new = st * jnp.exp(r_prev) + jnp.dot(v.T.astype(BF16), khat.astype(BF16),
                                                preferred_element_type=F32)
        st_ref[h] = st_new

        o = o[:rows_in]
        o = o * lax.rsqrt(jnp.mean(o * o, axis=-1, keepdims=True) + EPS)
        o = o * gain_ref[:, cols] * _silu(hg)
        out_ref[0, :, cols] = o.astype(out_ref.dtype)

    @pl.when(t == nt - 1)
    def _():
        for h in range(HG_HEADS):
            sfin_ref[0, h] = st_ref[h].T


def hgrn(hin, s0, lbp, gain, layer):
    b, l, w = hin.shape
    c = min(HG_CHUNK, l)
    assert l % c == 0 and c % HG_SUB == 0
    nt = l // c
    return pl.pallas_call(
        functools.partial(_hgrn_kernel, layer=layer),
        out_shape=[jax.ShapeDtypeStruct((b, l, HG_WIDTH), BF16),
                   jax.ShapeDtypeStruct((b, HG_HEADS, HG_DIM, HG_DIM), F32)],
        grid=(b, nt),
        in_specs=[pl.BlockSpec((1, c, w), lambda i, j: (i, j, 0)),
                  pl.BlockSpec((1, HG_HEADS, HG_DIM, HG_DIM), lambda i, j: (i, 0, 0, 0)),
                  pl.BlockSpec(lbp.shape, lambda i, j: (0, 0)),
                  pl.BlockSpec((1, HG_WIDTH), lambda i, j: (0, 0))],
        out_specs=[pl.BlockSpec((1, c, HG_WIDTH), lambda i, j: (i, j, 0)),
                   pl.BlockSpec((1, HG_HEADS, HG_DIM, HG_DIM), lambda i, j: (i, 0, 0, 0))],
        scratch_shapes=[pltpu.VMEM((HG_HEADS, HG_DIM, HG_DIM), F32)]
        + [pltpu.VMEM((HG_CHUNK, HG_DIM), F32)] * 4,
        compiler_params=_params(("parallel", "arbitrary")),
        name="hgrn",
    )(hin, s0, lbp, gain)


AUG_ONE = FOX_DIM
AUG_NEG = FOX_DIM + 3
ONE_LANE = FOX_HEADS


def _placement():
    pk = np.zeros((3 * LANES, FOX_HEADS * LANES), np.float32)
    pq = np.zeros((3 * LANES, FOX_HEADS * LANES), np.float32)
    for h in range(FOX_HEADS):
        for part in range(3):
            pk[part * LANES + h, h * LANES + AUG_NEG + part] = -1.0
            pq[part * LANES + h, h * LANES + AUG_ONE + part] = 1.0
            pk[ONE_LANE, h * LANES + AUG_ONE + part] = 1.0
            pq[ONE_LANE, h * LANES + AUG_NEG + part] = 1.0
    return jnp.asarray(pk, BF16), jnp.asarray(pq, BF16)


def _foxpack_kernel(q_ref, k_ref, v_ref, ff_ref, bias_ref, pk_ref, pq_ref,
                    qp_ref, kp_ref, vt_ref, lf_ref, carry_ref):
    @pl.when(pl.program_id(1) == 0)
    def _():
        carry_ref[...] = jnp.zeros_like(carry_ref)

    tm = q_ref.shape[1]
    lane = lax.broadcasted_iota(jnp.int32, (tm, LANES), 1)
    lf = jnp.where(lane < FOX_HEADS, _log_sigmoid(ff_ref[0] + bias_ref[...]), 0.0)
    lf_ref[0] = lf
    ri = lax.broadcasted_iota(jnp.int32, (tm, tm), 0)
    ci = lax.broadcasted_iota(jnp.int32, (tm, tm), 1)
    tril = jnp.where(ci <= ri, 1.0, 0.0).astype(BF16)
    hi, mid, lo = _split3(lf)
    c = (jnp.dot(tril, hi, preferred_element_type=F32)
         + jnp.dot(tril, mid, preferred_element_type=F32)
         + jnp.dot(tril, lo, preferred_element_type=F32)) + carry_ref[0:1, :]
    carry_ref[...] = jnp.broadcast_to(c[tm - 1:tm, :], carry_ref.shape)

    chi, cmid, clo = _split3(c * LOG2E)
    chi = jnp.where(lane == ONE_LANE, 1.0, chi.astype(F32)).astype(BF16)
    cterms = jnp.concatenate([chi, cmid, clo], axis=1)
    augk = jnp.dot(cterms, pk_ref[...], preferred_element_type=F32)
    augq = jnp.dot(cterms, pq_ref[...], preferred_element_type=F32)
    for h in range(FOX_HEADS):
        blk = slice((h // 2) * LANES, (h // 2 + 1) * LANES)
        hcols = slice(h * LANES, (h + 1) * LANES)
        kh = k_ref[0, :, blk]
        qh = q_ref[0, :, blk].astype(F32)
        if h % 2:
            kh = pltpu.roll(kh, FOX_DIM, axis=1)
            qh = pltpu.roll(qh, FOX_DIM, axis=1)
        kp_ref[0, h] = jnp.where(lane < FOX_DIM, kh, augk[:, hcols]).astype(BF16)
        qp_ref[0, h] = jnp.where(lane < FOX_DIM, qh, augq[:, hcols]).astype(BF16)
    vt_ref[0] = v_ref[0].T.astype(BF16)


def foxpack(fq, fk32, fv32, ff, bias_pad, tm):
    b, l, w = fq.shape
    pk, pq = _placement()
    row = lambda i, j: (i, j, 0)
    const = lambda i, j: (0, 0)
    head_blk = pl.BlockSpec((1, FOX_HEADS, tm, LANES), lambda i, j: (i, 0, j, 0))
    return pl.pallas_call(
        _foxpack_kernel,
        out_shape=[jax.ShapeDtypeStruct((b, FOX_HEADS, l, LANES), BF16),
                   jax.ShapeDtypeStruct((b, FOX_HEADS, l, LANES), BF16),
                   jax.ShapeDtypeStruct((b, w, l), BF16),
                   jax.ShapeDtypeStruct((b, l, LANES), F32)],
        grid=(b, l // tm),
        in_specs=[pl.BlockSpec((1, tm, w), row),
                  pl.BlockSpec((1, tm, w), row),
                  pl.BlockSpec((1, tm, w), row),
                  pl.BlockSpec((1, tm, LANES), row),
                  pl.BlockSpec((1, LANES), const),
                  pl.BlockSpec(pk.shape, const),
                  pl.BlockSpec(pq.shape, const)],
        out_specs=[head_blk, head_blk,
                   pl.BlockSpec((1, w, tm), lambda i, j: (i, 0, j)),
                   pl.BlockSpec((1, tm, LANES), row)],
        scratch_shapes=[pltpu.VMEM((8, LANES), F32)],
        compiler_params=_params(("parallel", "arbitrary")),
        name="foxpack",
    )(fq, fk32, fv32, ff, bias_pad, pk, pq)


def _fox_prompt_kernel(qi_ref, kj_ref, q_ref, k_ref, vt_ref, o_ref, m_ref, l_ref, acc_ref):
    t = pl.program_id(1)
    i = qi_ref[t]
    j = kj_ref[t]
    tq = q_ref.shape[2]
    tk = k_ref.shape[2]
    nt = (((1,), (1,)), ((), ()))

    @pl.when(j == 0)
    def _():
        m_ref[...] = jnp.full_like(m_ref, NEG)
        l_ref[...] = jnp.zeros_like(l_ref)
        acc_ref[...] = jnp.zeros_like(acc_ref)

    def step(diagonal):
        if diagonal:
            keep = (lax.broadcasted_iota(jnp.int32, (tk, tq), 0)
                    <= lax.broadcasted_iota(jnp.int32, (tk, tq), 1))
        for h in range(FOX_HEADS):
            rows = slice(h * FOX_DIM, (h + 1) * FOX_DIM)
            st = lax.dot_general(k_ref[0, h], q_ref[0, h], nt, preferred_element_type=F32)
            if diagonal:
                st = jnp.where(keep, st, NEG)
            m_old = m_ref[h:h + 1, :]
            m_new = jnp.maximum(m_old, jnp.max(st, axis=0, keepdims=True))
            alpha = jnp.exp2(m_old - m_new)
            p = jnp.exp2(st - m_new)
            l_ref[h:h + 1, :] = alpha * l_ref[h:h + 1, :] + jnp.sum(p, axis=0, keepdims=True)
            acc_ref[rows, :] = alpha * acc_ref[rows, :] + jnp.dot(
                vt_ref[0, rows, :], p.astype(BF16), preferred_element_type=F32)
            m_ref[h:h + 1, :] = m_new

    @pl.when(j < i)
    def _():
        step(False)

    @pl.when(j == i)
    def _():
        step(True)
        outs = [acc_ref[h * FOX_DIM:(h + 1) * FOX_DIM, :] / l_ref[h:h + 1, :] for h in range(FOX_HEADS)]
        o_ref[0] = jnp.concatenate(outs, axis=0).T.astype(o_ref.dtype)


def fox_prompt(qp, kp, vt, tq):
    b, nh, l, _ = qp.shape
    w = vt.shape[1]
    nq = l // tq
    qi = np.asarray([i for i in range(nq) for _ in range(i + 1)], np.int32)
    kj = np.asarray([j for i in range(nq) for j in range(i + 1)], np.int32)
    grid_spec = pltpu.PrefetchScalarGridSpec(
        num_scalar_prefetch=2,
        grid=(b, len(qi)),
        in_specs=[pl.BlockSpec((1, nh, tq, LANES), lambda bi, t, qi, kj: (bi, 0, qi[t], 0)),
                  pl.BlockSpec((1, nh, tq, LANES), lambda bi, t, qi, kj: (bi, 0, kj[t], 0)),
                  pl.BlockSpec((1, w, tq), lambda bi, t, qi, kj: (bi, 0, kj[t]))],
        out_specs=pl.BlockSpec((1, tq, w), lambda bi, t, qi, kj: (bi, qi[t], 0)),
        scratch_shapes=[pltpu.VMEM((FOX_HEADS, tq), F32),
                        pltpu.VMEM((FOX_HEADS, tq), F32),
                        pltpu.VMEM((w, tq), F32)])
    return pl.pallas_call(
        _fox_prompt_kernel,
        out_shape=jax.ShapeDtypeStruct((b, l, w), BF16),
        grid_spec=grid_spec,
        compiler_params=_params(("parallel", "arbitrary")),
        name="fox_prompt",
    )(jnp.asarray(qi), jnp.asarray(kj), qp, kp, vt)


def _fox_sample_kernel(q_ref, kp_ref, vp_ref, kn_ref, vn_ref, cq_ref, ck_ref, o_ref):
    lq = q_ref.shape[1]
    past = kp_ref.shape[1]
    ri = lax.broadcasted_iota(jnp.int32, (lq, lq), 0)
    ci = lax.broadcasted_iota(jnp.int32, (lq, lq), 1)
    nt = (((1,), (1,)), ((), ()))
    for h in range(FOX_HEADS):
        cols = slice(h * FOX_DIM, (h + 1) * FOX_DIM)
        q = q_ref[0, :, cols]
        cq = cq_ref[0, :, h:h + 1]
        sp = lax.dot_general(q, kp_ref[0, :, cols].astype(BF16), nt, preferred_element_type=F32)
        sp = sp + cq - ck_ref[0, h:h + 1, 0:past]
        sn = lax.dot_general(q, kn_ref[0, :, cols], nt, preferred_element_type=F32)
        sn = sn + cq - ck_ref[0, h:h + 1, past:past + lq]
        sn = jnp.where(ci <= ri, sn, NEG)
        m = jnp.maximum(jnp.max(sp, axis=-1, keepdims=True), jnp.max(sn, axis=-1, keepdims=True))
        pp = jnp.exp2(sp - m)
        pn = jnp.exp2(sn - m)
        den = jnp.sum(pp, axis=-1, keepdims=True) + jnp.sum(pn, axis=-1, keepdims=True)
        o = (jnp.dot(pp.astype(BF16), vp_ref[0, :, cols].astype(BF16), preferred_element_type=F32)
             + jnp.dot(pn.astype(BF16), vn_ref[0, :, cols], preferred_element_type=F32))
        o_ref[0, :, cols] = (o / den).astype(o_ref.dtype)


def fox_sample(q, kp, vp, kn, vn, cq, ckt):
    b, lq, w = q.shape
    past = kp.shape[1]
    lc = ckt.shape[2]
    new = lambda i: (i, 0, 0)
    return pl.pallas_call(
        _fox_sample_kernel,
        out_shape=jax.ShapeDtypeStruct((b, lq, w), BF16),
        grid=(b,),
        in_specs=[pl.BlockSpec((1, lq, w), new),
                  pl.BlockSpec((1, past, w), new),
                  pl.BlockSpec((1, past, w), new),
                  pl.BlockSpec((1, lq, w), new),
                  pl.BlockSpec((1, lq, w), new),
                  pl.BlockSpec((1, lq, FOX_HEADS), new),
                  pl.BlockSpec((1, FOX_HEADS, lc), new)],
        out_specs=pl.BlockSpec((1, lq, w), new),
        compiler_params=_params(("parallel",)),
        name="fox_sample",
    )(q, kp, vp, kn, vn, cq, ckt)


def _outproj_kernel(x_ref, hg_ref, fo_ref, mod_ref, fgain_ref, w_ref, gain2_ref,
                    wrh_ref, wrl_ref, br_ref, xo_ref, h2_ref, gt_ref):
    s, ls, d = x_ref.shape
    tm = s * ls
    fo = fo_ref[...].astype(F32)
    fn = fo * lax.rsqrt(jnp.mean(fo * fo, axis=-1, keepdims=True) + EPS) * fgain_ref[...]
    mixed = (jnp.dot(hg_ref[...].reshape(tm, HG_WIDTH), w_ref[0:HG_WIDTH, :],
                     preferred_element_type=F32)
             + jnp.dot(fn.reshape(tm, FOX_WIDTH).astype(BF16), w_ref[HG_WIDTH:, :],
                       preferred_element_type=F32))
    x = x_ref[...] + mod_ref[:, 2:3, :] * mixed.reshape(s, ls, d)
    xo_ref[...] = x
    y = x * lax.rsqrt(jnp.mean(x * x, axis=-1, keepdims=True) + EPS) * gain2_ref[...]
    h2 = (y * (1.0 + mod_ref[:, 4:5, :]) + mod_ref[:, 3:4, :]).reshape(tm, d)
    h2b = h2.astype(BF16)
    h2_ref[...] = h2b.reshape(s, ls, d)

    nt = (((1,), (1,)), ((), ()))
    logits = lax.dot_general(wrh_ref[...], h2b, nt, preferred_element_type=F32) \
        + lax.dot_general(wrl_ref[...], h2b, nt, preferred_element_type=F32)
    z = jnp.exp(logits - jnp.max(logits, axis=0, keepdims=True))
    probs = z / jnp.sum(z, axis=0, keepdims=True)
    sel = probs + br_ref[...]
    rows = [sel[e:e + 1, :] for e in range(N_EXPERTS)]
    prow = [probs[e:e + 1, :] for e in range(N_EXPERTS)]

    def beats(a, ia, b_, ib):
        return jnp.where(a >= b_, 1.0, 0.0) if ia < ib else jnp.where(a > b_, 1.0, 0.0)

    top = []
    gscore = []
    for g in range(N_GROUPS):
        ids = range(g * GROUP_SIZE, (g + 1) * GROUP_SIZE)
        sc = jnp.zeros_like(rows[0])
        for e in ids:
            cnt = jnp.zeros_like(rows[0])
            for o in ids:
                if o != e:
                    cnt = cnt + beats(rows[o], o, rows[e], e)
            flag = jnp.where(cnt < TOP_K, 1.0, 0.0)
            top.append(flag)
            sc = sc + flag * rows[e]
        gscore.append(sc)
    chosen = []
    for g in range(N_GROUPS):
        cnt = jnp.zeros_like(rows[0])
        for o in range(N_GROUPS):
            if o != g:
                cnt = cnt + beats(gscore[o], o, gscore[g], g)
        chosen.append(jnp.where(cnt < 1.0, 1.0, 0.0))
    wts = [prow[e] * top[e] * chosen[e // GROUP_SIZE] for e in range(N_EXPERTS)]
    den = wts[0]
    for e in range(1, N_EXPERTS):
        den = den + wts[e]
    gt_ref[0] = jnp.concatenate([w / den for w in wts], axis=0)


def outproj(x, hg_out, fox_o, mod, fox_gain, w_out, gain2, wr_hi, wr_lo, b_router, s, ls):
    b, l, d = x.shape
    nblk = (b // s) * (l // ls)
    nl = l // ls
    tm = s * ls
    row = lambda i, j: (i, j, 0)
    const = lambda i, j: (0, 0)
    return pl.pallas_call(
        _outproj_kernel,
        out_shape=[jax.ShapeDtypeStruct((b, l, d), F32),
                   jax.ShapeDtypeStruct((b, l, d), BF16),
                   jax.ShapeDtypeStruct((nblk, N_EXPERTS, tm), F32)],
        grid=(b // s, nl),
        in_specs=[pl.BlockSpec((s, ls, d), row),
                  pl.BlockSpec((s, ls, HG_WIDTH), row),
                  pl.BlockSpec((s, ls, FOX_WIDTH), row),
                  pl.BlockSpec((s, N_MOD, d), lambda i, j: (i, 0, 0)),
                  pl.BlockSpec((1, FOX_WIDTH), const),
                  pl.BlockSpec(w_out.shape, const, pipeline_mode=pl.Buffered(1)),
                  pl.BlockSpec((1, d), const),
                  pl.BlockSpec((N_EXPERTS, d), const),
                  pl.BlockSpec((N_EXPERTS, d), const),
                  pl.BlockSpec((N_EXPERTS, 1), const)],
        out_specs=[pl.BlockSpec((s, ls, d), row),
                   pl.BlockSpec((s, ls, d), row),
                   pl.BlockSpec((1, N_EXPERTS, tm), lambda i, j: (i * nl + j, 0, 0))],
        compiler_params=_params(("parallel", "parallel")),
        name="outproj",
    )(x, hg_out, fox_o, mod, fox_gain, w_out, gain2, wr_hi, wr_lo, b_router)


def _moe_kernel(x_ref, h_ref, g_ref, mod_ref, wg_ref, wu_ref, wd_ref, fgain_ref, o_ref, acc_ref,
                *, final):
    e = pl.program_id(2)
    s, ls, d = x_ref.shape
    tm = s * ls

    @pl.when(e == 0)
    def _():
        acc_ref[...] = jnp.zeros_like(acc_ref)

    h = h_ref[...].reshape(tm, d)
    a = jnp.dot(h, wg_ref[0], preferred_element_type=F32)
    u = jnp.dot(h, wu_ref[0], preferred_element_type=F32)
    lane = lax.broadcasted_iota(jnp.int32, (tm, N_EXPERTS), 1)
    gate = jnp.sum(jnp.where(lane == e, g_ref[0], 0.0), axis=-1, keepdims=True)
    he = (_silu(a) * u * gate).astype(BF16)
    acc_ref[...] += jnp.dot(he, wd_ref[0], preferred_element_type=F32)

    @pl.when(e == pl.num_programs(2) - 1)
    def _():
        x = x_ref[...] + mod_ref[:, 5:6, :] * acc_ref[...].reshape(s, ls, d)
        if final:
            x = x * lax.rsqrt(jnp.mean(x * x, axis=-1, keepdims=True) + EPS) * fgain_ref[...]
        o_ref[...] = x


def moe(x, h2, gates, mod, wg, wu, wd, final_gain, s, ls, final):
    b, l, d = x.shape
    nl = l // ls
    tm = s * ls
    de = wg.shape[2]
    row = lambda i, j, e: (i, j, 0)
    return pl.pallas_call(
        functools.partial(_moe_kernel, final=final),
        out_shape=jax.ShapeDtypeStruct((b, l, d), F32),
        grid=(b // s, nl, N_EXPERTS),
        in_specs=[pl.BlockSpec((s, ls, d), row),
                  pl.BlockSpec((s, ls, d), row),
                  pl.BlockSpec((1, tm, N_EXPERTS), lambda i, j, e: (i * nl + j, 0, 0)),
                  pl.BlockSpec((s, N_MOD, d), lambda i, j, e: (i, 0, 0)),
                  pl.BlockSpec((1, d, de), lambda i, j, e: (e, 0, 0)),
                  pl.BlockSpec((1, d, de), lambda i, j, e: (e, 0, 0)),
                  pl.BlockSpec((1, de, d), lambda i, j, e: (e, 0, 0)),
                  pl.BlockSpec((1, d), lambda i, j, e: (0, 0))],
        out_specs=pl.BlockSpec((s, ls, d), row),
        scratch_shapes=[pltpu.VMEM((tm, d), F32)],
        compiler_params=_params(("parallel", "parallel", "arbitrary")),
        name="moe",
    )(x, h2, gates, mod, wg, wu, wd, final_gain)


def _tile(b, l):
    ls = min(l, 512)
    return max(1, 512 // ls), ls


def _layer(x, mod, l, p, hg_state0, fox_past, final):
    b, seq, d = x.shape
    s, ls = _tile(b, seq)
    hg_in, fq, fk32, fv32, fk16, fv16, ff = inproj(x, mod, p['norm_mix_gain'][l], p['w_in'][l], s, ls)

    hg_out, hg_state = hgrn(hg_in, hg_state0, p['hg_lower_bounds'], p['hg_norm_gain'][l], l)

    if fox_past is None:
        tq = min(seq, 512)
        qp, kp, vt, lf_pad = foxpack(fq, fk32, fv32, ff, p['fox_f_bias_pad'][l], tq)
        fox_o = fox_prompt(qp, kp, vt, tq)
        logf = lf_pad[:, :, :FOX_HEADS]
    else:
        pk, pv, plogf = fox_past
        past = pk.shape[1]
        fft = jnp.swapaxes(ff[:, :, :FOX_HEADS], 1, 2)
        logft, _ = forget_cumsum(fft, p['fox_f_bias'][l], True, seq)
        tot = past + seq
        pad = (-tot) % LANES
        allt = jnp.concatenate([jnp.swapaxes(plogf, 1, 2), logft,
                                jnp.zeros((b, FOX_HEADS, pad), F32)], axis=2)
        _, ct = forget_cumsum(allt, p['fox_f_bias'][l], False, LANES)
        ct = ct * LOG2E
        cq = jnp.swapaxes(ct[:, :, past:tot], 1, 2)
        fox_o = fox_sample(fq, pk.reshape(b, past, FOX_WIDTH), pv.reshape(b, past, FOX_WIDTH),
                           fk16, fv16, cq, ct)
        logf = jnp.swapaxes(logft, 1, 2)

    x_mid, h2, gates_t = outproj(x, hg_out, fox_o, mod, p['fox_out_gain'][l], p['w_out'][l],
                                 p['norm_ffn_gain'][l], p['wr_hi'], p['wr_lo'], p['b_router'], s, ls)
    gates = jnp.swapaxes(gates_t, 1, 2)
    x_new = moe(x_mid, h2, gates, mod, p['w_exp_gate'][l], p['w_exp_up'][l], p['w_exp_down'][l],
                p['final_norm_gain'], s, ls, final)
    k_new = fk32.reshape(b, seq, FOX_HEADS, FOX_DIM)
    v_new = fv32.reshape(b, seq, FOX_HEADS, FOX_DIM)
    return x_new, hg_state, k_new, v_new, logf


def kernel(x_prompt, x_sample, cache_fox_k, cache_fox_v, cache_fox_logf, state_hgrn, c_prompt, c_sample,
           norm_mix_gain, norm_ffn_gain, w_ada, b_ada, w_in, hg_lower_bounds, hg_norm_gain,
           fox_f_bias, fox_out_gain, w_out, w_router, b_router, w_exp_gate, w_exp_up, w_exp_down,
           final_norm_gain):
    depth, d = norm_mix_gain.shape
    bp = x_prompt.shape[0]
    n_in = w_in.shape[2]
    n_pad = 4 * HG_WIDTH + 3 * FOX_WIDTH + LANES - n_in
    wr_t = w_router.T
    wr_hi = wr_t.astype(BF16)
    p = {
        'norm_mix_gain': norm_mix_gain.reshape(depth, 1, d),
        'norm_ffn_gain': norm_ffn_gain.reshape(depth, 1, d),
        'w_in': jnp.pad(w_in, ((0, 0), (0, 0), (0, n_pad))).astype(BF16),
        'hg_lower_bounds': hg_lower_bounds,
        'hg_norm_gain': hg_norm_gain.reshape(depth, 1, HG_WIDTH),
        'fox_f_bias': fox_f_bias.reshape(depth, FOX_HEADS, 1),
        'fox_f_bias_pad': jnp.pad(fox_f_bias, ((0, 0), (0, LANES - FOX_HEADS))).reshape(depth, 1, LANES),
        'fox_out_gain': fox_out_gain.reshape(depth, 1, FOX_WIDTH),
        'w_out': w_out.astype(BF16),
        'wr_hi': wr_hi,
        'wr_lo': (wr_t - wr_hi.astype(F32)).astype(BF16),
        'b_router': b_router.reshape(N_EXPERTS, 1),
        'w_exp_gate': w_exp_gate.astype(BF16),
        'w_exp_up': w_exp_up.astype(BF16),
        'w_exp_down': w_exp_down.astype(BF16),
        'final_norm_gain': final_norm_gain.reshape(1, d),
    }
    mods = ada_mod(jnp.concatenate([c_prompt, c_sample], axis=0), w_ada, b_ada)
    mods = mods.reshape(depth, -1, N_MOD, d)

    xp, xs = x_prompt, x_sample
    outs_p, outs_s = [], []
    zero_state = jnp.zeros((bp, HG_HEADS, HG_DIM, HG_DIM), F32)
    for l in range(depth):
        final = l == depth - 1
        xp, *rest_p = _layer(xp, mods[l, :bp], l, p, zero_state, None, final)
        xs, *rest_s = _layer(xs, mods[l, bp:], l, p, state_hgrn[l],
                             (cache_fox_k[l], cache_fox_v[l], cache_fox_logf[l]), final)
        outs_p.append(rest_p)
        outs_s.append(rest_s)
    stack = lambda outs, i: jnp.stack([o[i] for o in outs])
    return (xp, xs,
            stack(outs_p, 0), stack(outs_p, 1), stack(outs_p, 2), stack(outs_p, 3),
            stack(outs_s, 0), stack(outs_s, 1), stack(outs_s, 2), stack(outs_s, 3))
```

```python
import functools
import math

import numpy as np
import jax
import jax.numpy as jnp
from jax import lax
from jax.experimental import pallas as pl
from jax.experimental.pallas import tpu as pltpu

HG_HEADS = 4
HG_DIM = 128
HG_WIDTH = HG_HEADS * HG_DIM
FOX_HEADS = 8
FOX_DIM = 64
FOX_WIDTH = FOX_HEADS * FOX_DIM
N_EXPERTS = 16
N_GROUPS = 4
GROUP_SIZE = N_EXPERTS // N_GROUPS
TOP_K = 2
N_MOD = 6
EPS = 1e-6

LANES = 128
HG_CHUNK = 128
HG_SUB = 16
MOE_TILE = 16
BLOCK_TOKENS = 512
MOE_ROWS = 640
MOE_STEP_ROWS = 1024
VMEM_LIMIT = 56 * 1024 * 1024

F32 = jnp.float32
BF16 = jnp.bfloat16
NEG = -1e30
LOG2E = math.log2(math.e)


def _params(sem, vmem=VMEM_LIMIT):
    return pltpu.CompilerParams(dimension_semantics=sem, vmem_limit_bytes=vmem)


def _split3(x):
    hi = x.astype(BF16)
    r1 = x - hi.astype(F32)
    mid = r1.astype(BF16)
    lo = (r1 - mid.astype(F32)).astype(BF16)
    return hi, mid, lo


def _log_sigmoid(x):
    return jnp.minimum(x, 0.0) - jnp.log1p(jnp.exp(-jnp.abs(x)))


def _silu(x):
    return x * jax.nn.sigmoid(x)


def _ada_kernel(c_ref, w_ref, b_ref, o_ref):
    s = _silu(c_ref[...]).astype(BF16)
    o_ref[0] = jnp.dot(s, w_ref[0].astype(BF16), preferred_element_type=F32) + b_ref[0]


def ada_mod(c, w_ada, b_ada):
    depth, d, n = w_ada.shape
    nb = c.shape[0]
    tn = 1536
    return pl.pallas_call(
        _ada_kernel,
        out_shape=jax.ShapeDtypeStruct((depth, nb, n), F32),
        grid=(depth, n // tn),
        in_specs=[pl.BlockSpec((nb, d), lambda l, j: (0, 0)),
                  pl.BlockSpec((1, d, tn), lambda l, j: (l, 0, j)),
                  pl.BlockSpec((1, 1, tn), lambda l, j: (l, 0, j))],
        out_specs=pl.BlockSpec((1, nb, tn), lambda l, j: (l, 0, j)),
        compiler_params=_params(("arbitrary", "arbitrary")),
        name="ada_mod",
    )(c, w_ada, b_ada.reshape(depth, 1, n))


def _inproj_kernel(x_ref, mod_ref, gain_ref, w_ref,
                   hg_ref, fq_ref, fk32_ref, fv32_ref, fk16_ref, fv16_ref, ff_ref):
    s, ls, d = x_ref.shape
    x = x_ref[...]
    y = x * lax.rsqrt(jnp.mean(x * x, axis=-1, keepdims=True) + EPS) * gain_ref[...]
    h = y * (1.0 + mod_ref[:, 1:2, :]) + mod_ref[:, 0:1, :]
    hb = h.reshape(s * ls, d).astype(BF16)

    def proj(lo, hi):
        return jnp.dot(hb, w_ref[:, lo:hi], preferred_element_type=F32)

    c0 = 4 * HG_WIDTH
    hg_ref[...] = proj(0, c0).reshape(s, ls, c0)
    fq = proj(c0, c0 + FOX_WIDTH) * (FOX_DIM ** -0.5 * LOG2E)
    fq_ref[...] = fq.reshape(s, ls, FOX_WIDTH).astype(BF16)
    fk = proj(c0 + FOX_WIDTH, c0 + 2 * FOX_WIDTH).reshape(s, ls, FOX_WIDTH)
    fk32_ref[...] = fk
    fk16_ref[...] = fk.astype(BF16)
    fv = proj(c0 + 2 * FOX_WIDTH, c0 + 3 * FOX_WIDTH).reshape(s, ls, FOX_WIDTH)
    fv32_ref[...] = fv
    fv16_ref[...] = fv.astype(BF16)
    ff_ref[...] = proj(c0 + 3 * FOX_WIDTH, c0 + 3 * FOX_WIDTH + LANES).reshape(s, ls, LANES)


def inproj(x, mod, gain, w_pad, s, ls):
    b, l, d = x.shape
    n = w_pad.shape[1]
    row = lambda i, j: (i, j, 0)
    outs = [(4 * HG_WIDTH, F32), (FOX_WIDTH, BF16), (FOX_WIDTH, F32), (FOX_WIDTH, F32),
            (FOX_WIDTH, BF16), (FOX_WIDTH, BF16), (LANES, F32)]
    return pl.pallas_call(
        _inproj_kernel,
        out_shape=[jax.ShapeDtypeStruct((b, l, w), dt) for w, dt in outs],
        grid=(b // s, l // ls),
        in_specs=[pl.BlockSpec((s, ls, d), row),
                  pl.BlockSpec((s, N_MOD, d), lambda i, j: (i, 0, 0)),
                  pl.BlockSpec((1, d), lambda i, j: (0, 0)),
                  pl.BlockSpec((d, n), lambda i, j: (0, 0), pipeline_mode=pl.Buffered(1))],
        out_specs=[pl.BlockSpec((s, ls, w), row) for w, _ in outs],
        compiler_params=_params(("parallel", "parallel")),
        name="inproj",
    )(x, mod, gain, w_pad)


def _cumsum_kernel(x_ref, bias_ref, lf_ref, c_ref, carry_ref, *, apply_ls):
    @pl.when(pl.program_id(1) == 0)
    def _():
        carry_ref[...] = jnp.zeros_like(carry_ref)

    x = x_ref[0]
    tc = x.shape[-1]
    lf = _log_sigmoid(x + bias_ref[...]) if apply_ls else x
    lf_ref[0] = lf
    r = lax.broadcasted_iota(jnp.int32, (tc, tc), 0)
    c = lax.broadcasted_iota(jnp.int32, (tc, tc), 1)
    tri = jnp.where(r <= c, 1.0, 0.0).astype(BF16)
    hi, mid, lo = _split3(lf)
    tot = (jnp.dot(hi, tri, preferred_element_type=F32)
           + jnp.dot(mid, tri, preferred_element_type=F32)
           + jnp.dot(lo, tri, preferred_element_type=F32)) + carry_ref[:, 0:1]
    c_ref[0] = tot
    carry_ref[...] = jnp.broadcast_to(tot[:, tc - 1:tc], carry_ref.shape)


def forget_cumsum(xt, bias, apply_ls, tc):
    b, h, l = xt.shape
    blk = pl.BlockSpec((1, h, tc), lambda i, j: (i, 0, j))
    return pl.pallas_call(
        functools.partial(_cumsum_kernel, apply_ls=apply_ls),
        out_shape=[jax.ShapeDtypeStruct((b, h, l), F32)] * 2,
        grid=(b, l // tc),
        in_specs=[blk, pl.BlockSpec((h, 1), lambda i, j: (0, 0))],
        out_specs=[blk, blk],
        scratch_shapes=[pltpu.VMEM((h, LANES), F32)],
        compiler_params=_params(("parallel", "arbitrary")),
        name="forget_cumsum",
    )(xt, bias)


def _hgrn_kernel(hin_ref, s0_ref, lbp_ref, gain_ref, out_ref, sfin_ref,
                 st_ref, q_ref, k_ref, b_ref, v_ref, *, layer):
    t = pl.program_id(1)
    nt = pl.num_programs(1)
    c = HG_CHUNK
    nsub = c // HG_SUB
    rows_in = hin_ref.shape[1]

    def pad_rows(a):
        if rows_in == c:
            return a
        return jnp.concatenate([a, jnp.zeros((c - rows_in, a.shape[1]), a.dtype)], axis=0)

    @pl.when(t == 0)
    def _():
        for h in range(HG_HEADS):
            st_ref[h] = s0_ref[0, h].T

    lbp = lbp_ref[...]
    e = jnp.exp(lbp - jnp.max(lbp, axis=0, keepdims=True))
    p = e / jnp.sum(e, axis=0, keepdims=True)
    acc = p[0:1]
    first = acc
    for i in range(1, layer + 1):
        acc = acc + p[i:i + 1]
    lb_all = acc - first

    ri = lax.broadcasted_iota(jnp.int32, (c, c), 0)
    ci = lax.broadcasted_iota(jnp.int32, (c, c), 1)
    tril = jnp.where(ci <= ri, 1.0, 0.0).astype(BF16)
    ones = jnp.ones((HG_DIM, HG_DIM), BF16)
    srow = lax.broadcasted_iota(jnp.int32, (HG_SUB, HG_DIM), 0)

    for h in range(HG_HEADS):
        cols = slice(h * HG_DIM, (h + 1) * HG_DIM)
        lb = lb_all[:, cols]
        hq = hin_ref[0, :, h * HG_DIM:(h + 1) * HG_DIM]
        hf = hin_ref[0, :, HG_WIDTH + h * HG_DIM:HG_WIDTH + (h + 1) * HG_DIM]
        hi = hin_ref[0, :, 2 * HG_WIDTH + h * HG_DIM:2 * HG_WIDTH + (h + 1) * HG_DIM]
        hg = hin_ref[0, :, 3 * HG_WIDTH + h * HG_DIM:3 * HG_WIDTH + (h + 1) * HG_DIM]

        q = pad_rows(_silu(hq) * (HG_DIM ** -0.5))
        a1 = jnp.log(lb)
        a2 = jnp.log1p(-lb) + _log_sigmoid(hf)
        mx = jnp.maximum(a1, a2)
        lf = pad_rows(mx + jnp.log(jnp.exp(a1 - mx) + jnp.exp(a2 - mx)))
        kk = pad_rows((1.0 - lb) * jax.nn.sigmoid(-hf))
        v = pad_rows(hi)

        bh, bm, bl = _split3(lf)
        b = (jnp.dot(tril, bh, preferred_element_type=F32)
             + jnp.dot(tril, bm, preferred_element_type=F32)
             + jnp.dot(tril, bl, preferred_element_type=F32))
        q_ref[...] = q
        k_ref[...] = kk
        b_ref[...] = b
        v_ref[...] = v
        vb = v.astype(BF16)

        st = st_ref[h]
        o_blocks = []
        khat = None
        r_prev = jnp.zeros((1, HG_DIM), F32)
        qhat_blocks = []
        for i in range(nsub):
            r0 = i * HG_SUB
            qb = q_ref[r0:r0 + HG_SUB, :]
            kb = k_ref[r0:r0 + HG_SUB, :]
            bb = b_ref[r0:r0 + HG_SUB, :]
            r_next = b_ref[r0 + HG_SUB - 1:r0 + HG_SUB, :]

            slabs = []
            for s in range(HG_SUB):
                ks = k_ref[r0 + s:r0 + s + 1, :]
                bs = b_ref[r0 + s:r0 + s + 1, :]
                dec = jnp.exp(jnp.where(srow >= s, bb - bs, NEG))
                slabs.append((qb * ks) * dec)
            pm = jnp.concatenate(slabs, axis=0).astype(BF16)
            red = jnp.dot(pm, ones, preferred_element_type=F32)
            o_i = jnp.zeros((HG_SUB, HG_DIM), F32)
            for s in range(HG_SUB):
                o_i = o_i + red[s * HG_SUB:(s + 1) * HG_SUB, :] * v_ref[r0 + s:r0 + s + 1, :]

            qt = qb * jnp.exp(bb - r_prev)
            if khat is not None:
                att = lax.dot_general(qt.astype(BF16), khat.astype(BF16),
                                      (((1,), (1,)), ((), ())), preferred_element_type=F32)
                o_i = o_i + jnp.dot(att.astype(BF16), vb[:r0, :], preferred_element_type=F32)
            qhat_blocks.append(qt * jnp.exp(r_prev))
            o_blocks.append(o_i)

            kt = kb * jnp.exp(r_next - bb)
            if khat is None:
                khat = kt
            else:
                khat = jnp.concatenate([khat * jnp.exp(r_next - r_prev), kt], axis=0)
            r_prev = r_next

        qhat = jnp.concatenate(qhat_blocks, axis=0).astype(BF16)
        o = jnp.concatenate(o_blocks, axis=0)
        o = o + lax.dot_general(qhat, st.astype(BF16), (((1,), (1,)), ((), ())),
                                preferred_element_type=F32)
        st_new = st * jnp.exp(r_prev) + jnp.dot(v.T.astype(BF16), khat.astype(BF16),
                                                preferred_element_type=F32)
        st_ref[h] = st_new

        o = o[:rows_in]
        o = o * lax.rsqrt(jnp.mean(o * o, axis=-1, keepdims=True) + EPS)
        o = o * gain_ref[:, cols] * _silu(hg)
        out_ref[0, :, cols] = o.astype(out_ref.dtype)

    @pl.when(t == nt - 1)
    def _():
        for h in range(HG_HEADS):
            sfin_ref[0, h] = st_ref[h].T


def hgrn(hin, s0, lbp, gain, layer):
    b, l, w = hin.shape
    c = min(HG_CHUNK, l)
    assert l % c == 0 and c % HG_SUB == 0
    nt = l // c
    return pl.pallas_call(
        functools.partial(_hgrn_kernel, layer=layer),
        out_shape=[jax.ShapeDtypeStruct((b, l, HG_WIDTH), BF16),
                   jax.ShapeDtypeStruct((b, HG_HEADS, HG_DIM, HG_DIM), F32)],
        grid=(b, nt),
        in_specs=[pl.BlockSpec((1, c, w), lambda i, j: (i, j, 0)),
                  pl.BlockSpec((1, HG_HEADS, HG_DIM, HG_DIM), lambda i, j: (i, 0, 0, 0)),
                  pl.BlockSpec(lbp.shape, lambda i, j: (0, 0)),
                  pl.BlockSpec((1, HG_WIDTH), lambda i, j: (0, 0))],
        out_specs=[pl.BlockSpec((1, c, HG_WIDTH), lambda i, j: (i, j, 0)),
                   pl.BlockSpec((1, HG_HEADS, HG_DIM, HG_DIM), lambda i, j: (i, 0, 0, 0))],
        scratch_shapes=[pltpu.VMEM((HG_HEADS, HG_DIM, HG_DIM), F32)]
        + [pltpu.VMEM((HG_CHUNK, HG_DIM), F32)] * 4,
        compiler_params=_params(("parallel", "arbitrary")),
        name="hgrn",
    )(hin, s0, lbp, gain)


AUG_ONE = FOX_DIM
AUG_NEG = FOX_DIM + 3
ONE_LANE = FOX_HEADS


def _placement():
    pk = np.zeros((3 * LANES, FOX_HEADS * LANES), np.float32)
    pq = np.zeros((3 * LANES, FOX_HEADS * LANES), np.float32)
    for h in range(FOX_HEADS):
        for part in range(3):
            pk[part * LANES + h, h * LANES + AUG_NEG + part] = -1.0
            pq[part * LANES + h, h * LANES + AUG_ONE + part] = 1.0
            pk[ONE_LANE, h * LANES + AUG_ONE + part] = 1.0
            pq[ONE_LANE, h * LANES + AUG_NEG + part] = 1.0
    return jnp.asarray(pk, BF16), jnp.asarray(pq, BF16)


def _foxpack_kernel(q_ref, k_ref, v_ref, ff_ref, bias_ref, pk_ref, pq_ref,
                    qp_ref, kp_ref, vt_ref, lf_ref, carry_ref):
    @pl.when(pl.program_id(1) == 0)
    def _():
        carry_ref[...] = jnp.zeros_like(carry_ref)

    tm = q_ref.shape[1]
    lane = lax.broadcasted_iota(jnp.int32, (tm, LANES), 1)
    lf = jnp.where(lane < FOX_HEADS, _log_sigmoid(ff_ref[0] + bias_ref[...]), 0.0)
    lf_ref[0] = lf
    ri = lax.broadcasted_iota(jnp.int32, (tm, tm), 0)
    ci = lax.broadcasted_iota(jnp.int32, (tm, tm), 1)
    tril = jnp.where(ci <= ri, 1.0, 0.0).astype(BF16)
    hi, mid, lo = _split3(lf)
    c = (jnp.dot(tril, hi, preferred_element_type=F32)
         + jnp.dot(tril, mid, preferred_element_type=F32)
         + jnp.dot(tril, lo, preferred_element_type=F32)) + carry_ref[0:1, :]
    carry_ref[...] = jnp.broadcast_to(c[tm - 1:tm, :], carry_ref.shape)

    chi, cmid, clo = _split3(c * LOG2E)
    chi = jnp.where(lane == ONE_LANE, 1.0, chi.astype(F32)).astype(BF16)
    cterms = jnp.concatenate([chi, cmid, clo], axis=1)
    augk = jnp.dot(cterms, pk_ref[...], preferred_element_type=F32)
    augq = jnp.dot(cterms, pq_ref[...], preferred_element_type=F32)
    for h in range(FOX_HEADS):
        blk = slice((h // 2) * LANES, (h // 2 + 1) * LANES)
        hcols = slice(h * LANES, (h + 1) * LANES)
        kh = k_ref[0, :, blk]
        qh = q_ref[0, :, blk].astype(F32)
        if h % 2:
            kh = pltpu.roll(kh, FOX_DIM, axis=1)
            qh = pltpu.roll(qh, FOX_DIM, axis=1)
        kp_ref[0, h] = jnp.where(lane < FOX_DIM, kh, augk[:, hcols]).astype(BF16)
        qp_ref[0, h] = jnp.where(lane < FOX_DIM, qh, augq[:, hcols]).astype(BF16)
    vt_ref[0] = v_ref[0].T.astype(BF16)


def foxpack(fq, fk32, fv32, ff, bias_pad, tm):
    b, l, w = fq.shape
    pk, pq = _placement()
    row = lambda i, j: (i, j, 0)
    const = lambda i, j: (0, 0)
    head_blk = pl.BlockSpec((1, FOX_HEADS, tm, LANES), lambda i, j: (i, 0, j, 0))
    return pl.pallas_call(
        _foxpack_kernel,
        out_shape=[jax.ShapeDtypeStruct((b, FOX_HEADS, l, LANES), BF16),
                   jax.ShapeDtypeStruct((b, FOX_HEADS, l, LANES), BF16),
                   jax.ShapeDtypeStruct((b, w, l), BF16),
                   jax.ShapeDtypeStruct((b, l, LANES), F32)],
        grid=(b, l // tm),
        in_specs=[pl.BlockSpec((1, tm, w), row),
                  pl.BlockSpec((1, tm, w), row),
                  pl.BlockSpec((1, tm, w), row),
                  pl.BlockSpec((1, tm, LANES), row),
                  pl.BlockSpec((1, LANES), const),
                  pl.BlockSpec(pk.shape, const),
                  pl.BlockSpec(pq.shape, const)],
        out_specs=[head_blk, head_blk,
                   pl.BlockSpec((1, w, tm), lambda i, j: (i, 0, j)),
                   pl.BlockSpec((1, tm, LANES), row)],
        scratch_shapes=[pltpu.VMEM((8, LANES), F32)],
        compiler_params=_params(("parallel", "arbitrary")),
        name="foxpack",
    )(fq, fk32, fv32, ff, bias_pad, pk, pq)


def _fox_prompt_kernel(qi_ref, kj_ref, q_ref, k_ref, vt_ref, o_ref, m_ref, l_ref, acc_ref):
    t = pl.program_id(1)
    i = qi_ref[t]
    j = kj_ref[t]
    tq = q_ref.shape[2]
    tk = k_ref.shape[2]
    nt = (((1,), (1,)), ((), ()))

    @pl.when(j == 0)
    def _():
        m_ref[...] = jnp.full_like(m_ref, NEG)
        l_ref[...] = jnp.zeros_like(l_ref)
        acc_ref[...] = jnp.zeros_like(acc_ref)

    def step(diagonal):
        if diagonal:
            keep = (lax.broadcasted_iota(jnp.int32, (tk, tq), 0)
                    <= lax.broadcasted_iota(jnp.int32, (tk, tq), 1))
        for h in range(FOX_HEADS):
            rows = slice(h * FOX_DIM, (h + 1) * FOX_DIM)
            st = lax.dot_general(k_ref[0, h], q_ref[0, h], nt, preferred_element_type=F32)
            if diagonal:
                st = jnp.where(keep, st, NEG)
            m_old = m_ref[h:h + 1, :]
            m_new = jnp.maximum(m_old, jnp.max(st, axis=0, keepdims=True))
            alpha = jnp.exp2(m_old - m_new)
            p = jnp.exp2(st - m_new)
            l_ref[h:h + 1, :] = alpha * l_ref[h:h + 1, :] + jnp.sum(p, axis=0, keepdims=True)
            acc_ref[rows, :] = alpha * acc_ref[rows, :] + jnp.dot(
                vt_ref[0, rows, :], p.astype(BF16), preferred_element_type=F32)
            m_ref[h:h + 1, :] = m_new

    @pl.when(j < i)
    def _():
        step(False)

    @pl.when(j == i)
    def _():
        step(True)
        outs = [acc_ref[h * FOX_DIM:(h + 1) * FOX_DIM, :] / l_ref[h:h + 1, :] for h in range(FOX_HEADS)]
        o_ref[0] = jnp.concatenate(outs, axis=0).T.astype(o_ref.dtype)


def fox_prompt(qp, kp, vt, tq):
    b, nh, l, _ = qp.shape
    w = vt.shape[1]
    nq = l // tq
    qi = np.asarray([i for i in range(nq) for _ in range(i + 1)], np.int32)
    kj = np.asarray([j for i in range(nq) for j in range(i + 1)], np.int32)
    grid_spec = pltpu.PrefetchScalarGridSpec(
        num_scalar_prefetch=2,
        grid=(b, len(qi)),
        in_specs=[pl.BlockSpec((1, nh, tq, LANES), lambda bi, t, qi, kj: (bi, 0, qi[t], 0)),
                  pl.BlockSpec((1, nh, tq, LANES), lambda bi, t, qi, kj: (bi, 0, kj[t], 0)),
                  pl.BlockSpec((1, w, tq), lambda bi, t, qi, kj: (bi, 0, kj[t]))],
        out_specs=pl.BlockSpec((1, tq, w), lambda bi, t, qi, kj: (bi, qi[t], 0)),
        scratch_shapes=[pltpu.VMEM((FOX_HEADS, tq), F32),
                        pltpu.VMEM((FOX_HEADS, tq), F32),
                        pltpu.VMEM((w, tq), F32)])
    return pl.pallas_call(
        _fox_prompt_kernel,
        out_shape=jax.ShapeDtypeStruct((b, l, w), BF16),
        grid_spec=grid_spec,
        compiler_params=_params(("parallel", "arbitrary")),
        name="fox_prompt",
    )(jnp.asarray(qi), jnp.asarray(kj), qp, kp, vt)


def _fox_sample_kernel(q_ref, kp_ref, vp_ref, kn_ref, vn_ref, cq_ref, ck_ref, o_ref):
    lq = q_ref.shape[1]
    past = kp_ref.shape[1]
    ri = lax.broadcasted_iota(jnp.int32, (lq, lq), 0)
    ci = lax.broadcasted_iota(jnp.int32, (lq, lq), 1)
    nt = (((1,), (1,)), ((), ()))
    for h in range(FOX_HEADS):
        cols = slice(h * FOX_DIM, (h + 1) * FOX_DIM)
        q = q_ref[0, :, cols]
        cq = cq_ref[0, :, h:h + 1]
        sp = lax.dot_general(q, kp_ref[0, :, cols].astype(BF16), nt, preferred_element_type=F32)
        sp = sp + cq - ck_ref[0, h:h + 1, 0:past]
        sn = lax.dot_general(q, kn_ref[0, :, cols], nt, preferred_element_type=F32)
        sn = sn + cq - ck_ref[0, h:h + 1, past:past + lq]
        sn = jnp.where(ci <= ri, sn, NEG)
        m = jnp.maximum(jnp.max(sp, axis=-1, keepdims=True), jnp.max(sn, axis=-1, keepdims=True))
        pp = jnp.exp2(sp - m)
        pn = jnp.exp2(sn - m)
        den = jnp.sum(pp, axis=-1, keepdims=True) + jnp.sum(pn, axis=-1, keepdims=True)
        o = (jnp.dot(pp.astype(BF16), vp_ref[0, :, cols].astype(BF16), preferred_element_type=F32)
             + jnp.dot(pn.astype(BF16), vn_ref[0, :, cols], preferred_element_type=F32))
        o_ref[0, :, cols] = (o / den).astype(o_ref.dtype)


def fox_sample(q, kp, vp, kn, vn, cq, ckt):
    b, lq, w = q.shape
    past = kp.shape[1]
    lc = ckt.shape[2]
    new = lambda i: (i, 0, 0)
    return pl.pallas_call(
        _fox_sample_kernel,
        out_shape=jax.ShapeDtypeStruct((b, lq, w), BF16),
        grid=(b,),
        in_specs=[pl.BlockSpec((1, lq, w), new),
                  pl.BlockSpec((1, past, w), new),
                  pl.BlockSpec((1, past, w), new),
                  pl.BlockSpec((1, lq, w), new),
                  pl.BlockSpec((1, lq, w), new),
                  pl.BlockSpec((1, lq, FOX_HEADS), new),
                  pl.BlockSpec((1, FOX_HEADS, lc), new)],
        out_specs=pl.BlockSpec((1, lq, w), new),
        compiler_params=_params(("parallel",)),
        name="fox_sample",
    )(q, kp, vp, kn, vn, cq, ckt)


def _outproj_kernel(x_ref, hg_ref, fo_ref, mod_ref, fgain_ref, w_ref, gain2_ref,
                    wrh_ref, wrl_ref, br_ref, xo_ref, hs_ref, gs_ref, dest_ref, cnt_ref):
    s, ls, d = x_ref.shape
    tm = s * ls
    fo = fo_ref[...].astype(F32)
    fn = fo * lax.rsqrt(jnp.mean(fo * fo, axis=-1, keepdims=True) + EPS) * fgain_ref[...]
    mixed = (jnp.dot(hg_ref[...].reshape(tm, HG_WIDTH), w_ref[0:HG_WIDTH, :],
                     preferred_element_type=F32)
             + jnp.dot(fn.reshape(tm, FOX_WIDTH).astype(BF16), w_ref[HG_WIDTH:, :],
                       preferred_element_type=F32))
    x = x_ref[...] + mod_ref[:, 2:3, :] * mixed.reshape(s, ls, d)
    xo_ref[...] = x
    y = x * lax.rsqrt(jnp.mean(x * x, axis=-1, keepdims=True) + EPS) * gain2_ref[...]
    h2 = (y * (1.0 + mod_ref[:, 4:5, :]) + mod_ref[:, 3:4, :]).reshape(tm, d)
    h2b = h2.astype(BF16)

    nt = (((1,), (1,)), ((), ()))
    logits = lax.dot_general(wrh_ref[...], h2b, nt, preferred_element_type=F32) \
        + lax.dot_general(wrl_ref[...], h2b, nt, preferred_element_type=F32)
    z = jnp.exp(logits - jnp.max(logits, axis=0, keepdims=True))
    probs = z / jnp.sum(z, axis=0, keepdims=True)
    sel = probs + br_ref[...]
    rows = [sel[e:e + 1, :] for e in range(N_EXPERTS)]
    prow = [probs[e:e + 1, :] for e in range(N_EXPERTS)]

    def beats(a, ia, b_, ib):
        return jnp.where(a >= b_, 1.0, 0.0) if ia < ib else jnp.where(a > b_, 1.0, 0.0)

    top = []
    gscore = []
    for g in range(N_GROUPS):
        ids = range(g * GROUP_SIZE, (g + 1) * GROUP_SIZE)
        sc = jnp.zeros_like(rows[0])
        for e in ids:
            cnt = jnp.zeros_like(rows[0])
            for o in ids:
                if o != e:
                    cnt = cnt + beats(rows[o], o, rows[e], e)
            flag = jnp.where(cnt < TOP_K, 1.0, 0.0)
            top.append(flag)
            sc = sc + flag * rows[e]
        gscore.append(sc)
    chosen = []
    for g in range(N_GROUPS):
        cnt = jnp.zeros_like(rows[0])
        for o in range(N_GROUPS):
            if o != g:
                cnt = cnt + beats(gscore[o], o, gscore[g], g)
        chosen.append(jnp.where(cnt < 1.0, 1.0, 0.0))
    wts = [prow[e] * top[e] * chosen[e // GROUP_SIZE] for e in range(N_EXPERTS)]
    den = wts[0]
    for e in range(1, N_EXPERTS):
        den = den + wts[e]
    g4 = []
    for k in range(GROUP_SIZE):
        gk = wts[k]
        for g in range(1, N_GROUPS):
            gk = gk + wts[g * GROUP_SIZE + k]
        g4.append(gk / den)

    rr = hs_ref.shape[1]
    ki = lax.broadcasted_iota(jnp.int32, (tm, tm), 0)
    ji = lax.broadcasted_iota(jnp.int32, (tm, tm), 1)
    before = jnp.where(ki < ji, 1.0, 0.0).astype(BF16)
    chosen4 = jnp.concatenate(chosen, axis=0)
    rank = jnp.dot(chosen4.astype(BF16), before, preferred_element_type=F32)
    lane = lax.broadcasted_iota(jnp.int32, (1, LANES), 1)
    start = jnp.zeros((1, 1), F32)
    dest = jnp.zeros((1, tm), F32)
    cnt_row = jnp.zeros((1, LANES), F32)
    for g in range(N_GROUPS):
        n_g = jnp.sum(chosen[g], axis=1, keepdims=True)
        tiles_g = jnp.floor((n_g + (MOE_TILE - 1)) * (1.0 / MOE_TILE))
        dest = dest + chosen[g] * (start + rank[g:g + 1, :])
        cnt_row = cnt_row + jnp.where(lane == g, tiles_g, 0.0)
        start = start + tiles_g * MOE_TILE
    dest_i = dest.astype(jnp.int32)
    dest_ref[0] = dest_i
    cnt_ref[0] = cnt_row.astype(jnp.int32)
    perm = jnp.where(lax.broadcasted_iota(jnp.int32, (rr, tm), 0) == dest_i, 1.0, 0.0).astype(BF16)
    hs_ref[0] = jnp.dot(perm, h2b, preferred_element_type=F32).astype(BF16)
    g128 = jnp.concatenate(g4 + [jnp.zeros((LANES - GROUP_SIZE, tm), F32)], axis=0)
    ghi = g128.astype(BF16)
    glo = (g128 - ghi.astype(F32)).astype(BF16)
    gs_ref[0] = (lax.dot_general(perm, ghi, nt, preferred_element_type=F32)
                 + lax.dot_general(perm, glo, nt, preferred_element_type=F32))


def outproj(x, hg_out, fox_o, mod, fox_gain, w_out, gain2, wr_hi, wr_lo, b_router, s, ls):
    b, l, d = x.shape
    nblk = (b // s) * (l // ls)
    nl = l // ls
    tm = s * ls
    row = lambda i, j: (i, j, 0)
    const = lambda i, j: (0, 0)
    blk = lambda i, j: (i * nl + j, 0, 0)
    return pl.pallas_call(
        _outproj_kernel,
        out_shape=[jax.ShapeDtypeStruct((b, l, d), F32),
                   jax.ShapeDtypeStruct((nblk, MOE_ROWS, d), BF16),
                   jax.ShapeDtypeStruct((nblk, MOE_ROWS, LANES), F32),
                   jax.ShapeDtypeStruct((nblk, 1, tm), jnp.int32),
                   jax.ShapeDtypeStruct((nblk, 1, LANES), jnp.int32)],
        grid=(b // s, nl),
        in_specs=[pl.BlockSpec((s, ls, d), row),
                  pl.BlockSpec((s, ls, HG_WIDTH), row),
                  pl.BlockSpec((s, ls, FOX_WIDTH), row),
                  pl.BlockSpec((s, N_MOD, d), lambda i, j: (i, 0, 0)),
                  pl.BlockSpec((1, FOX_WIDTH), const),
                  pl.BlockSpec(w_out.shape, const, pipeline_mode=pl.Buffered(1)),
                  pl.BlockSpec((1, d), const),
                  pl.BlockSpec((N_EXPERTS, d), const),
                  pl.BlockSpec((N_EXPERTS, d), const),
                  pl.BlockSpec((N_EXPERTS, 1), const)],
        out_specs=[pl.BlockSpec((s, ls, d), row),
                   pl.BlockSpec((1, MOE_ROWS, d), blk),
                   pl.BlockSpec((1, MOE_ROWS, LANES), blk),
                   pl.BlockSpec((1, 1, tm), blk),
                   pl.BlockSpec((1, 1, LANES), blk)],
        compiler_params=_params(("parallel", "parallel")),
        name="outproj",
    )(x, hg_out, fox_o, mod, fox_gain, w_out, gain2, wr_hi, wr_lo, b_router)


def _moe_tables(cnt, rows_per_step):
    nblk = cnt.shape[0]
    tiles_per_step = rows_per_step // MOE_TILE
    max_tiles = nblk * (BLOCK_TOKENS // MOE_TILE + N_GROUPS)
    n_steps = -(-max_tiles // tiles_per_step) + N_GROUPS
    start_tile = jnp.cumsum(cnt, axis=1) - cnt
    cnt_gm = cnt.T.reshape(-1)
    csum = jnp.cumsum(cnt_gm)
    tau = jnp.arange(max_tiles, dtype=jnp.int32)
    pair = jnp.minimum(jnp.searchsorted(csum, tau, side='right'), N_GROUPS * nblk - 1).astype(jnp.int32)
    k = tau - (csum[pair] - cnt_gm[pair])
    grp = pair // nblk
    blk = pair % nblk
    valid = tau < csum[-1]
    tile_blk = jnp.where(valid, blk, 0).astype(jnp.int32)
    tile_row = jnp.where(valid, (start_tile[blk, grp] + k) * MOE_TILE, 0).astype(jnp.int32)
    tot = jnp.sum(cnt, axis=0)
    gstart = jnp.cumsum(tot) - tot
    nsteps = (tot + tiles_per_step - 1) // tiles_per_step
    send = jnp.cumsum(nsteps)
    s = jnp.arange(n_steps, dtype=jnp.int32)
    sg = jnp.minimum(jnp.searchsorted(send, s, side='right'), N_GROUPS - 1).astype(jnp.int32)
    first = gstart[sg] + (s - (send[sg] - nsteps[sg])) * tiles_per_step
    num = jnp.clip(gstart[sg] + tot[sg] - first, 0, tiles_per_step)
    first = jnp.where(num > 0, first, 0)
    return sg, first.astype(jnp.int32), num.astype(jnp.int32), tile_blk, tile_row


def _moe_kernel(sg_ref, first_ref, num_ref, tblk_ref, trow_ref,
                hs_hbm, gs_hbm, wg_ref, wu_ref, wd_ref, yinit_hbm, ys_hbm,
                hbuf, gbuf, ybuf, sem_in, sem_out):
    del sg_ref, yinit_hbm
    s = pl.program_id(0)
    first = first_ref[s]
    num = num_ref[s]

    @pl.when(s == 0)
    def _():
        hbuf[...] = jnp.zeros_like(hbuf)
        gbuf[...] = jnp.zeros_like(gbuf)

    def h_copy(k):
        t = first + k
        row = pl.multiple_of(trow_ref[t], MOE_TILE)
        return pltpu.make_async_copy(hs_hbm.at[tblk_ref[t], pl.ds(row, MOE_TILE), :],
                                     hbuf.at[pl.ds(pl.multiple_of(k * MOE_TILE, MOE_TILE), MOE_TILE), :],
                                     sem_in.at[0])

    def g_copy(k):
        t = first + k
        row = pl.multiple_of(trow_ref[t], MOE_TILE)
        return pltpu.make_async_copy(gs_hbm.at[tblk_ref[t], pl.ds(row, MOE_TILE), :],
                                     gbuf.at[pl.ds(pl.multiple_of(k * MOE_TILE, MOE_TILE), MOE_TILE), :],
                                     sem_in.at[1])

    def y_copy(k):
        t = first + k
        row = pl.multiple_of(trow_ref[t], MOE_TILE)
        return pltpu.make_async_copy(ybuf.at[pl.ds(pl.multiple_of(k * MOE_TILE, MOE_TILE), MOE_TILE), :],
                                     ys_hbm.at[tblk_ref[t], pl.ds(row, MOE_TILE), :],
                                     sem_out.at[0])

    def start_in(k, c):
        h_copy(k).start()
        g_copy(k).start()
        return c

    def wait_in(k, c):
        h_copy(k).wait()
        g_copy(k).wait()
        return c

    lax.fori_loop(0, num, start_in, 0)
    lax.fori_loop(0, num, wait_in, 0)

    @pl.when(num > 0)
    def _():
        h = hbuf[...]
        acc = jnp.zeros((hbuf.shape[0], wd_ref.shape[2]), F32)
        for k in range(GROUP_SIZE):
            a = jnp.dot(h, wg_ref[k], preferred_element_type=F32)
            u = jnp.dot(h, wu_ref[k], preferred_element_type=F32)
            he = (_silu(a) * u * gbuf[:, k:k + 1]).astype(BF16)
            acc = acc + jnp.dot(he, wd_ref[k], preferred_element_type=F32)
        ybuf[...] = acc.astype(ybuf.dtype)

    def start_out(k, c):
        y_copy(k).start()
        return c

    def wait_out(k, c):
        y_copy(k).wait()
        return c

    lax.fori_loop(0, num, start_out, 0)
    lax.fori_loop(0, num, wait_out, 0)


def moe_sparse(hs, gs, cnt, wg, wu, wd):
    nblk, rr, d = hs.shape
    de = wg.shape[2]
    rows = MOE_STEP_ROWS
    sg, first, num, tile_blk, tile_row = _moe_tables(cnt, rows)
    n_steps = sg.shape[0]
    wmap = lambda s, sg, *_: (sg[s], 0, 0)
    grid_spec = pltpu.PrefetchScalarGridSpec(
        num_scalar_prefetch=5,
        grid=(n_steps,),
        in_specs=[pl.BlockSpec(memory_space=pl.ANY),
                  pl.BlockSpec(memory_space=pl.ANY),
                  pl.BlockSpec((GROUP_SIZE, d, de), wmap),
                  pl.BlockSpec((GROUP_SIZE, d, de), wmap),
                  pl.BlockSpec((GROUP_SIZE, de, d), wmap),
                  pl.BlockSpec(memory_space=pl.ANY)],
        out_specs=pl.BlockSpec(memory_space=pl.ANY),
        scratch_shapes=[pltpu.VMEM((rows, d), BF16),
                        pltpu.VMEM((rows, LANES), F32),
                        pltpu.VMEM((rows, d), BF16),
                        pltpu.SemaphoreType.DMA((2,)),
                        pltpu.SemaphoreType.DMA((1,))])
    return pl.pallas_call(
        _moe_kernel,
        out_shape=jax.ShapeDtypeStruct((nblk, rr, d), BF16),
        grid_spec=grid_spec,
        input_output_aliases={10: 0},
        compiler_params=_params(("arbitrary",)),
        name="moe",
    )(sg, first, num, tile_blk, tile_row, hs, gs, wg, wu, wd, jnp.zeros((nblk, rr, d), BF16))


def _combine_kernel(x_ref, ys_ref, dest_ref, mod_ref, fgain_ref, o_ref, *, final):
    s, ls, d = x_ref.shape
    tm = s * ls
    rr = ys_ref.shape[1]
    dcol = jnp.broadcast_to(dest_ref[0].astype(F32), (LANES, tm)).T
    lane = lax.broadcasted_iota(jnp.int32, (tm, LANES), 1).astype(F32)
    unperm = jnp.concatenate(
        [jnp.where(dcol == lane + float(c * LANES), 1.0, 0.0).astype(BF16) for c in range(rr // LANES)],
        axis=1)
    y = jnp.dot(unperm, ys_ref[0], preferred_element_type=F32)
    x = x_ref[...] + mod_ref[:, 5:6, :] * y.reshape(s, ls, d)
    if final:
        x = x * lax.rsqrt(jnp.mean(x * x, axis=-1, keepdims=True) + EPS) * fgain_ref[...]
    o_ref[...] = x


def combine(x, ys, dest, mod, final_gain, s, ls, final):
    b, l, d = x.shape
    nl = l // ls
    tm = s * ls
    row = lambda i, j: (i, j, 0)
    blk = lambda i, j: (i * nl + j, 0, 0)
    return pl.pallas_call(
        functools.partial(_combine_kernel, final=final),
        out_shape=jax.ShapeDtypeStruct((b, l, d), F32),
        grid=(b // s, nl),
        in_specs=[pl.BlockSpec((s, ls, d), row),
                  pl.BlockSpec((1, ys.shape[1], d), blk),
                  pl.BlockSpec((1, 1, tm), blk),
                  pl.BlockSpec((s, N_MOD, d), lambda i, j: (i, 0, 0)),
                  pl.BlockSpec((1, d), lambda i, j: (0, 0))],
        out_specs=pl.BlockSpec((s, ls, d), row),
        compiler_params=_params(("parallel", "parallel")),
        name="combine",
    )(x, ys, dest, mod, final_gain)


def _tile(b, l):
    ls = min(l, BLOCK_TOKENS)
    s = BLOCK_TOKENS // ls
    assert s * ls == BLOCK_TOKENS and b % s == 0 and l % ls == 0
    assert BLOCK_TOKENS + N_GROUPS * MOE_TILE <= MOE_ROWS
    return s, ls


def _layer(x, mod, l, p, hg_state0, fox_past, final):
    b, seq, d = x.shape
    s, ls = _tile(b, seq)
    hg_in, fq, fk32, fv32, fk16, fv16, ff = inproj(x, mod, p['norm_mix_gain'][l], p['w_in'][l], s, ls)

    hg_out, hg_state = hgrn(hg_in, hg_state0, p['hg_lower_bounds'], p['hg_norm_gain'][l], l)

    if fox_past is None:
        tq = min(seq, 512)
        qp, kp, vt, lf_pad = foxpack(fq, fk32, fv32, ff, p['fox_f_bias_pad'][l], tq)
        fox_o = fox_prompt(qp, kp, vt, tq)
        logf = lf_pad[:, :, :FOX_HEADS]
    else:
        pk, pv, plogf = fox_past
        past = pk.shape[1]
        fft = jnp.swapaxes(ff[:, :, :FOX_HEADS], 1, 2)
        logft, _ = forget_cumsum(fft, p['fox_f_bias'][l], True, seq)
        tot = past + seq
        pad = (-tot) % LANES
        allt = jnp.concatenate([jnp.swapaxes(plogf, 1, 2), logft,
                                jnp.zeros((b, FOX_HEADS, pad), F32)], axis=2)
        _, ct = forget_cumsum(allt, p['fox_f_bias'][l], False, LANES)
        ct = ct * LOG2E
        cq = jnp.swapaxes(ct[:, :, past:tot], 1, 2)
        fox_o = fox_sample(fq, pk.reshape(b, past, FOX_WIDTH), pv.reshape(b, past, FOX_WIDTH),
                           fk16, fv16, cq, ct)
        logf = jnp.swapaxes(logft, 1, 2)

    x_mid, hs, gs, dest, cnt = outproj(x, hg_out, fox_o, mod, p['fox_out_gain'][l], p['w_out'][l],
                                       p['norm_ffn_gain'][l], p['wr_hi'], p['wr_lo'], p['b_router'], s, ls)
    ys = moe_sparse(hs, gs, cnt[:, 0, :N_GROUPS], p['w_exp_gate'][l], p['w_exp_up'][l], p['w_exp_down'][l])
    x_new = combine(x_mid, ys, dest, mod, p['final_norm_gain'], s, ls, final)
    k_new = fk32.reshape(b, seq, FOX_HEADS, FOX_DIM)
    v_new = fv32.reshape(b, seq, FOX_HEADS, FOX_DIM)
    return x_new, hg_state, k_new, v_new, logf


def kernel(x_prompt, x_sample, cache_fox_k, cache_fox_v, cache_fox_logf, state_hgrn, c_prompt, c_sample,
           norm_mix_gain, norm_ffn_gain, w_ada, b_ada, w_in, hg_lower_bounds, hg_norm_gain,
           fox_f_bias, fox_out_gain, w_out, w_router, b_router, w_exp_gate, w_exp_up, w_exp_down,
           final_norm_gain):
    depth, d = norm_mix_gain.shape
    bp = x_prompt.shape[0]
    n_in = w_in.shape[2]
    n_pad = 4 * HG_WIDTH + 3 * FOX_WIDTH + LANES - n_in
    wr_t = w_router.T
    wr_hi = wr_t.astype(BF16)
    p = {
        'norm_mix_gain': norm_mix_gain.reshape(depth, 1, d),
        'norm_ffn_gain': norm_ffn_gain.reshape(depth, 1, d),
        'w_in': jnp.pad(w_in, ((0, 0), (0, 0), (0, n_pad))).astype(BF16),
        'hg_lower_bounds': hg_lower_bounds,
        'hg_norm_gain': hg_norm_gain.reshape(depth, 1, HG_WIDTH),
        'fox_f_bias': fox_f_bias.reshape(depth, FOX_HEADS, 1),
        'fox_f_bias_pad': jnp.pad(fox_f_bias, ((0, 0), (0, LANES - FOX_HEADS))).reshape(depth, 1, LANES),
        'fox_out_gain': fox_out_gain.reshape(depth, 1, FOX_WIDTH),
        'w_out': w_out.astype(BF16),
        'wr_hi': wr_hi,
        'wr_lo': (wr_t - wr_hi.astype(F32)).astype(BF16),
        'b_router': b_router.reshape(N_EXPERTS, 1),
        'w_exp_gate': w_exp_gate.astype(BF16),
        'w_exp_up': w_exp_up.astype(BF16),
        'w_exp_down': w_exp_down.astype(BF16),
        'final_norm_gain': final_norm_gain.reshape(1, d),
    }
    mods = ada_mod(jnp.concatenate([c_prompt, c_sample], axis=0), w_ada, b_ada)
    mods = mods.reshape(depth, -1, N_MOD, d)

    xp, xs = x_prompt, x_sample
    outs_p, outs_s = [], []
    zero_state = jnp.zeros((bp, HG_HEADS, HG_DIM, HG_DIM), F32)
    for l in range(depth):
        final = l == depth - 1
        xp, *rest_p = _layer(xp, mods[l, :bp], l, p, zero_state, None, final)
        xs, *rest_s = _layer(xs, mods[l, bp:], l, p, state_hgrn[l],
                             (cache_fox_k[l], cache_fox_v[l], cache_fox_logf[l]), final)
        outs_p.append(rest_p)
        outs_s.append(rest_s)
    stack = lambda outs, i: jnp.stack([o[i] for o in outs])
    return (xp, xs,
            stack(outs_p, 0), stack(outs_p, 1), stack(outs_p, 2), stack(outs_p, 3),
            stack(outs_s, 0), stack(outs_s, 1), stack(outs_s, 2), stack(outs_s, 3))
```

```python
import functools
import math

import numpy as np
import jax
import jax.numpy as jnp
from jax import lax
from jax.experimental import pallas as pl
from jax.experimental.pallas import tpu as pltpu

HG_HEADS = 4
HG_DIM = 128
HG_WIDTH = HG_HEADS * HG_DIM
FOX_HEADS = 8
FOX_DIM = 64
FOX_WIDTH = FOX_HEADS * FOX_DIM
N_EXPERTS = 16
N_GROUPS = 4
GROUP_SIZE = N_EXPERTS // N_GROUPS
TOP_K = 2
N_MOD = 6
EPS = 1e-6

LANES = 128
HG_CHUNK = 128
HG_SUB = 16
MOE_TILE = 16
BLOCK_TOKENS = 512
MOE_ROWS = 640
MOE_STEP_ROWS = 1024
VMEM_LIMIT = 56 * 1024 * 1024

F32 = jnp.float32
BF16 = jnp.bfloat16
NEG = -1e30
LOG2E = math.log2(math.e)


def _params(sem, vmem=VMEM_LIMIT):
    return pltpu.CompilerParams(dimension_semantics=sem, vmem_limit_bytes=vmem)


def _split3(x):
    hi = x.astype(BF16)
    r1 = x - hi.astype(F32)
    mid = r1.astype(BF16)
    lo = (r1 - mid.astype(F32)).astype(BF16)
    return hi, mid, lo


def _log_sigmoid(x):
    return jnp.minimum(x, 0.0) - jnp.log1p(jnp.exp(-jnp.abs(x)))


def _silu(x):
    return x * jax.nn.sigmoid(x)


def _ada_kernel(c_ref, w_ref, b_ref, o_ref):
    s = _silu(c_ref[...]).astype(BF16)
    o_ref[0] = jnp.dot(s, w_ref[0].astype(BF16), preferred_element_type=F32) + b_ref[0]


def ada_mod(c, w_ada, b_ada):
    depth, d, n = w_ada.shape
    nb = c.shape[0]
    tn = 1536
    return pl.pallas_call(
        _ada_kernel,
        out_shape=jax.ShapeDtypeStruct((depth, nb, n), F32),
        grid=(depth, n // tn),
        in_specs=[pl.BlockSpec((nb, d), lambda l, j: (0, 0)),
                  pl.BlockSpec((1, d, tn), lambda l, j: (l, 0, j)),
                  pl.BlockSpec((1, 1, tn), lambda l, j: (l, 0, j))],
        out_specs=pl.BlockSpec((1, nb, tn), lambda l, j: (l, 0, j)),
        compiler_params=_params(("arbitrary", "arbitrary")),
        name="ada_mod",
    )(c, w_ada, b_ada.reshape(depth, 1, n))


def _inproj_kernel(x_ref, mod_ref, gain_ref, w_ref, *refs, has_prev, emit16):
    outs = refs[2:] if has_prev else refs
    hg_ref, fq_ref, fk32_ref, fv32_ref = outs[:4]
    ff_ref = outs[-1]
    s, ls, d = x_ref.shape
    x = x_ref[...]
    y = x * lax.rsqrt(jnp.mean(x * x, axis=-1, keepdims=True) + EPS) * gain_ref[...]
    h = y * (1.0 + mod_ref[:, 1:2, :]) + mod_ref[:, 0:1, :]
    hb = h.reshape(s * ls, d).astype(BF16)

    def proj(lo, hi):
        return jnp.dot(hb, w_ref[:, lo:hi], preferred_element_type=F32)

    c0 = 4 * HG_WIDTH
    hg_ref[...] = proj(0, c0).reshape(s, ls, c0)
    fq = proj(c0, c0 + FOX_WIDTH) * (FOX_DIM ** -0.5 * LOG2E)
    fq_ref[...] = fq.reshape(s, ls, FOX_WIDTH).astype(BF16)
    fk = proj(c0 + FOX_WIDTH, c0 + 2 * FOX_WIDTH).reshape(s, ls, FOX_WIDTH)
    fv = proj(c0 + 2 * FOX_WIDTH, c0 + 3 * FOX_WIDTH).reshape(s, ls, FOX_WIDTH)
    for slot in range(fk32_ref.shape[0]):
        fk32_ref[slot] = fk
        fv32_ref[slot] = fv
    if emit16:
        outs[4][...] = fk.astype(BF16)
        outs[5][...] = fv.astype(BF16)
    ff_ref[...] = proj(c0 + 3 * FOX_WIDTH, c0 + 3 * FOX_WIDTH + LANES).reshape(s, ls, LANES)


def inproj(x, mod, gain, w_pad, s, ls, layer, depth, kv_prev, emit16):
    b, l, d = x.shape
    n = w_pad.shape[1]
    row = lambda i, j: (i, j, 0)
    lrow = lambda i, j: (layer, i, j, 0)
    shapes = [jax.ShapeDtypeStruct((b, l, 4 * HG_WIDTH), F32),
              jax.ShapeDtypeStruct((b, l, FOX_WIDTH), BF16),
              jax.ShapeDtypeStruct((depth, b, l, FOX_WIDTH), F32),
              jax.ShapeDtypeStruct((depth, b, l, FOX_WIDTH), F32)]
    slots = depth if kv_prev is None else 1
    if kv_prev is None:
        assert layer == 0
    specs = [pl.BlockSpec((s, ls, 4 * HG_WIDTH), row),
             pl.BlockSpec((s, ls, FOX_WIDTH), row),
             pl.BlockSpec((slots, s, ls, FOX_WIDTH), lrow),
             pl.BlockSpec((slots, s, ls, FOX_WIDTH), lrow)]
    if emit16:
        shapes += [jax.ShapeDtypeStruct((b, l, FOX_WIDTH), BF16)] * 2
        specs += [pl.BlockSpec((s, ls, FOX_WIDTH), row)] * 2
    shapes.append(jax.ShapeDtypeStruct((b, l, LANES), F32))
    specs.append(pl.BlockSpec((s, ls, LANES), row))
    in_specs = [pl.BlockSpec((s, ls, d), row),
                pl.BlockSpec((s, N_MOD, d), lambda i, j: (i, 0, 0)),
                pl.BlockSpec((1, d), lambda i, j: (0, 0)),
                pl.BlockSpec((d, n), lambda i, j: (0, 0), pipeline_mode=pl.Buffered(1))]
    args = [x, mod, gain, w_pad]
    aliases = {}
    if kv_prev is not None:
        in_specs += [pl.BlockSpec(memory_space=pl.ANY)] * 2
        args += list(kv_prev)
        aliases = {4: 2, 5: 3}
    return pl.pallas_call(
        functools.partial(_inproj_kernel, has_prev=kv_prev is not None, emit16=emit16),
        out_shape=shapes,
        grid=(b // s, l // ls),
        in_specs=in_specs,
        out_specs=specs,
        input_output_aliases=aliases,
        compiler_params=_params(("parallel", "parallel")),
        name="inproj",
    )(*args)


def _cumsum_kernel(x_ref, bias_ref, lf_ref, c_ref, carry_ref, *, apply_ls):
    @pl.when(pl.program_id(1) == 0)
    def _():
        carry_ref[...] = jnp.zeros_like(carry_ref)

    bb, nh, tc = x_ref.shape
    x = x_ref[...].reshape(bb * nh, tc)
    lf = _log_sigmoid(x + bias_ref[...]) if apply_ls else x
    lf_ref[...] = lf.reshape(bb, nh, tc)
    r = lax.broadcasted_iota(jnp.int32, (tc, tc), 0)
    c = lax.broadcasted_iota(jnp.int32, (tc, tc), 1)
    tri = jnp.where(r <= c, 1.0, 0.0).astype(BF16)
    hi, mid, lo = _split3(lf)
    tot = (jnp.dot(hi, tri, preferred_element_type=F32)
           + jnp.dot(mid, tri, preferred_element_type=F32)
           + jnp.dot(lo, tri, preferred_element_type=F32)) + carry_ref[:, 0:1]
    c_ref[...] = tot.reshape(bb, nh, tc)
    carry_ref[...] = jnp.broadcast_to(tot[:, tc - 1:tc], carry_ref.shape)


def forget_cumsum(xt, bias, apply_ls, tc):
    b, h, l = xt.shape
    blk = pl.BlockSpec((b, h, tc), lambda i, j: (0, 0, j))
    return pl.pallas_call(
        functools.partial(_cumsum_kernel, apply_ls=apply_ls),
        out_shape=[jax.ShapeDtypeStruct((b, h, l), F32)] * 2,
        grid=(1, l // tc),
        in_specs=[blk, pl.BlockSpec((b * h, 1), lambda i, j: (0, 0))],
        out_specs=[blk, blk],
        scratch_shapes=[pltpu.VMEM((b * h, LANES), F32)],
        compiler_params=_params(("arbitrary", "arbitrary")),
        name="forget_cumsum",
    )(xt, jnp.tile(bias, (b, 1)))


HG_HALVES = (8, 4, 2, 1)


def _hgrn_sum_matrix():
    c = HG_CHUNK
    t = np.arange(c)[:, None]
    u = np.arange(c)[None, :]
    mats = [(u <= t)]
    for w in HG_HALVES:
        pos = t % (2 * w)
        mid = t - pos + w - 1
        upper = pos >= w
        mats.append(np.where(upper, (u > mid) & (u <= t), (u > t) & (u <= mid)))
    return jnp.asarray(np.concatenate(mats, axis=0).astype(np.float32), BF16)


def _hgrn_kernel(hin_ref, s0_ref, lbp_ref, gain_ref, sums_ref, out_ref, sfin_ref, st_ref, *, layer):
    t = pl.program_id(1)
    nt = pl.num_programs(1)
    c = HG_CHUNK
    nsub = c // HG_SUB
    rows_in = hin_ref.shape[1]

    def pad_rows(a):
        if rows_in == c:
            return a
        return jnp.concatenate([a, jnp.zeros((c - rows_in, a.shape[1]), a.dtype)], axis=0)

    @pl.when(t == 0)
    def _():
        for h in range(HG_HEADS):
            st_ref[h] = s0_ref[0, h].T

    lbp = lbp_ref[...]
    e = jnp.exp(lbp - jnp.max(lbp, axis=0, keepdims=True))
    p = e / jnp.sum(e, axis=0, keepdims=True)
    acc = p[0:1]
    first = acc
    for i in range(1, layer + 1):
        acc = acc + p[i:i + 1]
    lb_all = acc - first

    ri = lax.broadcasted_iota(jnp.int32, (c, c), 0)
    ci = lax.broadcasted_iota(jnp.int32, (c, c), 1)
    ones = jnp.ones((HG_DIM, HG_DIM), BF16)
    nt_dims = (((1,), (1,)), ((), ()))

    for h in range(HG_HEADS):
        cols = slice(h * HG_DIM, (h + 1) * HG_DIM)
        lb = lb_all[:, cols]
        hq = hin_ref[0, :, h * HG_DIM:(h + 1) * HG_DIM]
        hf = hin_ref[0, :, HG_WIDTH + h * HG_DIM:HG_WIDTH + (h + 1) * HG_DIM]
        hi = hin_ref[0, :, 2 * HG_WIDTH + h * HG_DIM:2 * HG_WIDTH + (h + 1) * HG_DIM]
        hg = hin_ref[0, :, 3 * HG_WIDTH + h * HG_DIM:3 * HG_WIDTH + (h + 1) * HG_DIM]

        q = pad_rows(_silu(hq) * (HG_DIM ** -0.5))
        a1 = jnp.log(lb)
        a2 = jnp.log1p(-lb) + _log_sigmoid(hf)
        mx = jnp.maximum(a1, a2)
        lf = pad_rows(mx + jnp.log(jnp.exp(a1 - mx) + jnp.exp(a2 - mx)))
        kk = pad_rows((1.0 - lb) * jax.nn.sigmoid(-hf))
        v = pad_rows(hi)

        lhi, lmid, llo = _split3(lf)
        sums = (jnp.dot(sums_ref[...], lhi, preferred_element_type=F32)
                + jnp.dot(sums_ref[...], lmid, preferred_element_type=F32)
                + jnp.dot(sums_ref[...], llo, preferred_element_type=F32))
        b = sums[0:c]

        att = jnp.where(ri == ci, jnp.dot((q * kk).astype(BF16), ones, preferred_element_type=F32), 0.0)
        for lvl, w in enumerate(HG_HALVES):
            dec = jnp.exp(sums[(lvl + 1) * c:(lvl + 2) * c])
            upper = (ri & (2 * w - 1)) >= w
            qs = jnp.where(upper, q * dec, 0.0).astype(BF16)
            ks = jnp.where(upper, 0.0, kk * dec).astype(BF16)
            sc = lax.dot_general(qs, ks, nt_dims, preferred_element_type=F32)
            shift = (2 * w).bit_length() - 1
            att = att + jnp.where((ri >> shift) == (ci >> shift), sc, 0.0)

        st = st_ref[h]
        khat = None
        r_prev = jnp.zeros((1, HG_DIM), F32)
        qhat_blocks = []
        cross = []
        for i in range(nsub):
            r0 = i * HG_SUB
            bb = b[r0:r0 + HG_SUB]
            r_next = b[r0 + HG_SUB - 1:r0 + HG_SUB]
            qt = q[r0:r0 + HG_SUB] * jnp.exp(bb - r_prev)
            if khat is None:
                cross.append(jnp.zeros((HG_SUB, c), F32))
            else:
                kfull = jnp.concatenate([khat, jnp.zeros((c - r0, HG_DIM), F32)], axis=0)
                cross.append(lax.dot_general(qt.astype(BF16), kfull.astype(BF16), nt_dims,
                                             preferred_element_type=F32))
            qhat_blocks.append(qt * jnp.exp(r_prev))
            kt = kk[r0:r0 + HG_SUB] * jnp.exp(r_next - bb)
            if khat is None:
                khat = kt
            else:
                khat = jnp.concatenate([khat * jnp.exp(r_next - r_prev), kt], axis=0)
            r_prev = r_next
        att = att + jnp.concatenate(cross, axis=0)

        qhat = jnp.concatenate(qhat_blocks, axis=0).astype(BF16)
        o = jnp.dot(att.astype(BF16), v.astype(BF16), preferred_element_type=F32)
        o = o + lax.dot_general(qhat, st.astype(BF16), nt_dims, preferred_element_type=F32)
        st_new = st * jnp.exp(r_prev) + jnp.dot(v.T.astype(BF16), khat.astype(BF16),
                                                preferred_element_type=F32)
        st_ref[h] = st_new

        o = o[:rows_in]
        o = o * lax.rsqrt(jnp.mean(o * o, axis=-1, keepdims=True) + EPS)
        o = o * gain_ref[:, cols] * _silu(hg)
        out_ref[0, :, cols] = o.astype(out_ref.dtype)

    @pl.when(t == nt - 1)
    def _():
        for h in range(HG_HEADS):
            sfin_ref[0, h] = st_ref[h].T


def hgrn(hin, s0, lbp, gain, layer):
    b, l, w = hin.shape
    c = min(HG_CHUNK, l)
    assert l % c == 0 and c % HG_SUB == 0
    nt = l // c
    sums = _hgrn_sum_matrix()
    return pl.pallas_call(
        functools.partial(_hgrn_kernel, layer=layer),
        out_shape=[jax.ShapeDtypeStruct((b, l, HG_WIDTH), BF16),
                   jax.ShapeDtypeStruct((b, HG_HEADS, HG_DIM, HG_DIM), F32)],
        grid=(b, nt),
        in_specs=[pl.BlockSpec((1, c, w), lambda i, j: (i, j, 0)),
                  pl.BlockSpec((1, HG_HEADS, HG_DIM, HG_DIM), lambda i, j: (i, 0, 0, 0)),
                  pl.BlockSpec(lbp.shape, lambda i, j: (0, 0)),
                  pl.BlockSpec((1, HG_WIDTH), lambda i, j: (0, 0)),
                  pl.BlockSpec(sums.shape, lambda i, j: (0, 0))],
        out_specs=[pl.BlockSpec((1, c, HG_WIDTH), lambda i, j: (i, j, 0)),
                   pl.BlockSpec((1, HG_HEADS, HG_DIM, HG_DIM), lambda i, j: (i, 0, 0, 0))],
        scratch_shapes=[pltpu.VMEM((HG_HEADS, HG_DIM, HG_DIM), F32)],
        compiler_params=_params(("parallel", "arbitrary")),
        name="hgrn",
    )(hin, s0, lbp, gain, sums)


AUG_ONE = FOX_DIM
AUG_NEG = FOX_DIM + 3
ONE_LANE = FOX_HEADS


def _placement():
    pk = np.zeros((3 * LANES, FOX_HEADS * LANES), np.float32)
    pq = np.zeros((3 * LANES, FOX_HEADS * LANES), np.float32)
    for h in range(FOX_HEADS):
        for part in range(3):
            pk[part * LANES + h, h * LANES + AUG_NEG + part] = -1.0
            pq[part * LANES + h, h * LANES + AUG_ONE + part] = 1.0
            pk[ONE_LANE, h * LANES + AUG_ONE + part] = 1.0
            pq[ONE_LANE, h * LANES + AUG_NEG + part] = 1.0
    return jnp.asarray(pk, BF16), jnp.asarray(pq, BF16)


def _foxpack_kernel(q_ref, k_ref, v_ref, ff_ref, bias_ref, pk_ref, pq_ref,
                    qp_ref, kp_ref, vt_ref, lf_ref, carry_ref):
    @pl.when(pl.program_id(1) == 0)
    def _():
        carry_ref[...] = jnp.zeros_like(carry_ref)

    tm = q_ref.shape[1]
    lane = lax.broadcasted_iota(jnp.int32, (tm, LANES), 1)
    lf = jnp.where(lane < FOX_HEADS, _log_sigmoid(ff_ref[0] + bias_ref[...]), 0.0)
    lf_ref[0] = lf
    ri = lax.broadcasted_iota(jnp.int32, (tm, tm), 0)
    ci = lax.broadcasted_iota(jnp.int32, (tm, tm), 1)
    tril = jnp.where(ci <= ri, 1.0, 0.0).astype(BF16)
    hi, mid, lo = _split3(lf)
    c = (jnp.dot(tril, hi, preferred_element_type=F32)
         + jnp.dot(tril, mid, preferred_element_type=F32)
         + jnp.dot(tril, lo, preferred_element_type=F32)) + carry_ref[0:1, :]
    carry_ref[...] = jnp.broadcast_to(c[tm - 1:tm, :], carry_ref.shape)

    chi, cmid, clo = _split3(c * LOG2E)
    chi = jnp.where(lane == ONE_LANE, 1.0, chi.astype(F32)).astype(BF16)
    cterms = jnp.concatenate([chi, cmid, clo], axis=1)
    augk = jnp.dot(cterms, pk_ref[...], preferred_element_type=F32)
    augq = jnp.dot(cterms, pq_ref[...], preferred_element_type=F32)
    for h in range(FOX_HEADS):
        blk = slice((h // 2) * LANES, (h // 2 + 1) * LANES)
        hcols = slice(h * LANES, (h + 1) * LANES)
        kh = k_ref[0, 0, :, blk]
        qh = q_ref[0, :, blk].astype(F32)
        if h % 2:
            kh = pltpu.roll(kh, FOX_DIM, axis=1)
            qh = pltpu.roll(qh, FOX_DIM, axis=1)
        kp_ref[0, h] = jnp.where(lane < FOX_DIM, kh, augk[:, hcols]).astype(BF16)
        qp_ref[0, h] = jnp.where(lane < FOX_DIM, qh, augq[:, hcols]).astype(BF16)
    vt_ref[0] = v_ref[0, 0].T.astype(BF16)


def foxpack(fq, kbuf, vbuf, layer, ff, bias_pad, tm):
    b, l, w = fq.shape
    pk, pq = _placement()
    row = lambda i, j: (i, j, 0)
    lrow = lambda i, j: (layer, i, j, 0)
    const = lambda i, j: (0, 0)
    head_blk = pl.BlockSpec((1, FOX_HEADS, tm, LANES), lambda i, j: (i, 0, j, 0))
    return pl.pallas_call(
        _foxpack_kernel,
        out_shape=[jax.ShapeDtypeStruct((b, FOX_HEADS, l, LANES), BF16),
                   jax.ShapeDtypeStruct((b, FOX_HEADS, l, LANES), BF16),
                   jax.ShapeDtypeStruct((b, w, l), BF16),
                   jax.ShapeDtypeStruct((b, l, LANES), F32)],
        grid=(b, l // tm),
        in_specs=[pl.BlockSpec((1, tm, w), row),
                  pl.BlockSpec((1, 1, tm, w), lrow),
                  pl.BlockSpec((1, 1, tm, w), lrow),
                  pl.BlockSpec((1, tm, LANES), row),
                  pl.BlockSpec((1, LANES), const),
                  pl.BlockSpec(pk.shape, const),
                  pl.BlockSpec(pq.shape, const)],
        out_specs=[head_blk, head_blk,
                   pl.BlockSpec((1, w, tm), lambda i, j: (i, 0, j)),
                   pl.BlockSpec((1, tm, LANES), row)],
        scratch_shapes=[pltpu.VMEM((8, LANES), F32)],
        compiler_params=_params(("parallel", "arbitrary")),
        name="foxpack",
    )(fq, kbuf, vbuf, ff, bias_pad, pk, pq)


def _fox_prompt_kernel(qi_ref, kj_ref, q_ref, k_ref, vt_ref, o_ref, m_ref, l_ref, acc_ref):
    t = pl.program_id(1)
    i = qi_ref[t]
    j = kj_ref[t]
    tq = q_ref.shape[2]
    tk = k_ref.shape[2]
    nt = (((1,), (1,)), ((), ()))

    @pl.when(j == 0)
    def _():
        m_ref[...] = jnp.full_like(m_ref, NEG)
        l_ref[...] = jnp.zeros_like(l_ref)
        acc_ref[...] = jnp.zeros_like(acc_ref)

    def step(diagonal):
        if diagonal:
            keep = (lax.broadcasted_iota(jnp.int32, (tk, tq), 0)
                    <= lax.broadcasted_iota(jnp.int32, (tk, tq), 1))
        for h in range(FOX_HEADS):
            rows = slice(h * FOX_DIM, (h + 1) * FOX_DIM)
            st = lax.dot_general(k_ref[0, h], q_ref[0, h], nt, preferred_element_type=F32)
            if diagonal:
                st = jnp.where(keep, st, NEG)
            m_old = m_ref[h:h + 1, :]
            m_new = jnp.maximum(m_old, jnp.max(st, axis=0, keepdims=True))
            alpha = jnp.exp2(m_old - m_new)
            p = jnp.exp2(st - m_new)
            l_ref[h:h + 1, :] = alpha * l_ref[h:h + 1, :] + jnp.sum(p, axis=0, keepdims=True)
            acc_ref[rows, :] = alpha * acc_ref[rows, :] + jnp.dot(
                vt_ref[0, rows, :], p.astype(BF16), preferred_element_type=F32)
            m_ref[h:h + 1, :] = m_new

    @pl.when(j < i)
    def _():
        step(False)

    @pl.when(j == i)
    def _():
        step(True)
        outs = [acc_ref[h * FOX_DIM:(h + 1) * FOX_DIM, :] / l_ref[h:h + 1, :] for h in range(FOX_HEADS)]
        o_ref[0] = jnp.concatenate(outs, axis=0).T.astype(o_ref.dtype)


def fox_prompt(qp, kp, vt, tq):
    b, nh, l, _ = qp.shape
    w = vt.shape[1]
    nq = l // tq
    qi = np.asarray([i for i in range(nq) for _ in range(i + 1)], np.int32)
    kj = np.asarray([j for i in range(nq) for j in range(i + 1)], np.int32)
    grid_spec = pltpu.PrefetchScalarGridSpec(
        num_scalar_prefetch=2,
        grid=(b, len(qi)),
        in_specs=[pl.BlockSpec((1, nh, tq, LANES), lambda bi, t, qi, kj: (bi, 0, qi[t], 0)),
                  pl.BlockSpec((1, nh, tq, LANES), lambda bi, t, qi, kj: (bi, 0, kj[t], 0)),
                  pl.BlockSpec((1, w, tq), lambda bi, t, qi, kj: (bi, 0, kj[t]))],
        out_specs=pl.BlockSpec((1, tq, w), lambda bi, t, qi, kj: (bi, qi[t], 0)),
        scratch_shapes=[pltpu.VMEM((FOX_HEADS, tq), F32),
                        pltpu.VMEM((FOX_HEADS, tq), F32),
                        pltpu.VMEM((w, tq), F32)])
    return pl.pallas_call(
        _fox_prompt_kernel,
        out_shape=jax.ShapeDtypeStruct((b, l, w), BF16),
        grid_spec=grid_spec,
        compiler_params=_params(("parallel", "arbitrary")),
        name="fox_prompt",
    )(jnp.asarray(qi), jnp.asarray(kj), qp, kp, vt)


def _fox_sample_kernel(q_ref, kp_ref, vp_ref, kn_ref, vn_ref, cq_ref, ck_ref, o_ref):
    lq = q_ref.shape[1]
    past = kp_ref.shape[2]
    ri = lax.broadcasted_iota(jnp.int32, (lq, lq), 0)
    ci = lax.broadcasted_iota(jnp.int32, (lq, lq), 1)
    nt = (((1,), (1,)), ((), ()))
    for h in range(FOX_HEADS):
        cols = slice(h * FOX_DIM, (h + 1) * FOX_DIM)
        q = q_ref[0, :, cols]
        cq = cq_ref[0, :, h:h + 1]
        sp = lax.dot_general(q, kp_ref[0, 0, :, cols].astype(BF16), nt, preferred_element_type=F32)
        sp = sp + cq - ck_ref[0, h:h + 1, 0:past]
        sn = lax.dot_general(q, kn_ref[0, :, cols], nt, preferred_element_type=F32)
        sn = sn + cq - ck_ref[0, h:h + 1, past:past + lq]
        sn = jnp.where(ci <= ri, sn, NEG)
        m = jnp.maximum(jnp.max(sp, axis=-1, keepdims=True), jnp.max(sn, axis=-1, keepdims=True))
        pp = jnp.exp2(sp - m)
        pn = jnp.exp2(sn - m)
        den = jnp.sum(pp, axis=-1, keepdims=True) + jnp.sum(pn, axis=-1, keepdims=True)
        o = (jnp.dot(pp.astype(BF16), vp_ref[0, 0, :, cols].astype(BF16), preferred_element_type=F32)
             + jnp.dot(pn.astype(BF16), vn_ref[0, :, cols], preferred_element_type=F32))
        o_ref[0, :, cols] = (o / den).astype(o_ref.dtype)


def fox_sample(q, kcache, vcache, layer, kn, vn, cq, ckt):
    b, lq, w = q.shape
    past = kcache.shape[2]
    lc = ckt.shape[2]
    new = lambda i: (i, 0, 0)
    old = lambda i: (layer, i, 0, 0)
    return pl.pallas_call(
        _fox_sample_kernel,
        out_shape=jax.ShapeDtypeStruct((b, lq, w), BF16),
        grid=(b,),
        in_specs=[pl.BlockSpec((1, lq, w), new),
                  pl.BlockSpec((1, 1, past, w), old),
                  pl.BlockSpec((1, 1, past, w), old),
                  pl.BlockSpec((1, lq, w), new),
                  pl.BlockSpec((1, lq, w), new),
                  pl.BlockSpec((1, lq, FOX_HEADS), new),
                  pl.BlockSpec((1, FOX_HEADS, lc), new)],
        out_specs=pl.BlockSpec((1, lq, w), new),
        compiler_params=_params(("parallel",)),
        name="fox_sample",
    )(q, kcache, vcache, kn, vn, cq, ckt)


def _outproj_kernel(x_ref, hg_ref, fo_ref, mod_ref, fgain_ref, w_ref, gain2_ref,
                    wrh_ref, wrl_ref, br_ref, xo_ref, hs_ref, gs_ref, dest_ref, cnt_ref):
    s, ls, d = x_ref.shape
    tm = s * ls
    fo = fo_ref[...].astype(F32)
    fn = fo * lax.rsqrt(jnp.mean(fo * fo, axis=-1, keepdims=True) + EPS) * fgain_ref[...]
    mixed = (jnp.dot(hg_ref[...].reshape(tm, HG_WIDTH), w_ref[0:HG_WIDTH, :],
                     preferred_element_type=F32)
             + jnp.dot(fn.reshape(tm, FOX_WIDTH).astype(BF16), w_ref[HG_WIDTH:, :],
                       preferred_element_type=F32))
    x = x_ref[...] + mod_ref[:, 2:3, :] * mixed.reshape(s, ls, d)
    xo_ref[...] = x
    y = x * lax.rsqrt(jnp.mean(x * x, axis=-1, keepdims=True) + EPS) * gain2_ref[...]
    h2 = (y * (1.0 + mod_ref[:, 4:5, :]) + mod_ref[:, 3:4, :]).reshape(tm, d)
    h2b = h2.astype(BF16)

    nt = (((1,), (1,)), ((), ()))
    logits = lax.dot_general(wrh_ref[...], h2b, nt, preferred_element_type=F32) \
        + lax.dot_general(wrl_ref[...], h2b, nt, preferred_element_type=F32)
    z = jnp.exp(logits - jnp.max(logits, axis=0, keepdims=True))
    probs = z / jnp.sum(z, axis=0, keepdims=True)
    sel = probs + br_ref[...]
    rows = [sel[e:e + 1, :] for e in range(N_EXPERTS)]
    prow = [probs[e:e + 1, :] for e in range(N_EXPERTS)]

    def beats(a, ia, b_, ib):
        return jnp.where(a >= b_, 1.0, 0.0) if ia < ib else jnp.where(a > b_, 1.0, 0.0)

    top = []
    gscore = []
    for g in range(N_GROUPS):
        ids = range(g * GROUP_SIZE, (g + 1) * GROUP_SIZE)
        sc = jnp.zeros_like(rows[0])
        for e in ids:
            cnt = jnp.zeros_like(rows[0])
            for o in ids:
                if o != e:
                    cnt = cnt + beats(rows[o], o, rows[e], e)
            flag = jnp.where(cnt < TOP_K, 1.0, 0.0)
            top.append(flag)
            sc = sc + flag * rows[e]
        gscore.append(sc)
    chosen = []
    for g in range(N_GROUPS):
        cnt = jnp.zeros_like(rows[0])
        for o in range(N_GROUPS):
            if o != g:
                cnt = cnt + beats(gscore[o], o, gscore[g], g)
        chosen.append(jnp.where(cnt < 1.0, 1.0, 0.0))
    wts = [prow[e] * top[e] * chosen[e // GROUP_SIZE] for e in range(N_EXPERTS)]
    den = wts[0]
    for e in range(1, N_EXPERTS):
        den = den + wts[e]
    g4 = []
    for k in range(GROUP_SIZE):
        gk = wts[k]
        for g in range(1, N_GROUPS):
            gk = gk + wts[g * GROUP_SIZE + k]
        g4.append(gk / den)

    rr = hs_ref.shape[1]
    ki = lax.broadcasted_iota(jnp.int32, (tm, tm), 0)
    ji = lax.broadcasted_iota(jnp.int32, (tm, tm), 1)
    before = jnp.where(ki < ji, 1.0, 0.0).astype(BF16)
    chosen4 = jnp.concatenate(chosen, axis=0)
    rank = jnp.dot(chosen4.astype(BF16), before, preferred_element_type=F32)
    lane = lax.broadcasted_iota(jnp.int32, (1, LANES), 1)
    start = jnp.zeros((1, 1), F32)
    dest = jnp.zeros((1, tm), F32)
    cnt_row = jnp.zeros((1, LANES), F32)
    for g in range(N_GROUPS):
        n_g = jnp.sum(chosen[g], axis=1, keepdims=True)
        tiles_g = jnp.floor((n_g + (MOE_TILE - 1)) * (1.0 / MOE_TILE))
        dest = dest + chosen[g] * (start + rank[g:g + 1, :])
        cnt_row = cnt_row + jnp.where(lane == g, tiles_g, 0.0)
        start = start + tiles_g * MOE_TILE
    dest_i = dest.astype(jnp.int32)
    dest_ref[0] = dest_i
    cnt_ref[0] = cnt_row.astype(jnp.int32)
    perm = jnp.where(lax.broadcasted_iota(jnp.int32, (rr, tm), 0) == dest_i, 1.0, 0.0).astype(BF16)
    hs_ref[0] = jnp.dot(perm, h2b, preferred_element_type=F32).astype(BF16)
    g128 = jnp.concatenate(g4 + [jnp.zeros((LANES - GROUP_SIZE, tm), F32)], axis=0)
    ghi = g128.astype(BF16)
    glo = (g128 - ghi.astype(F32)).astype(BF16)
    gs_ref[0] = (lax.dot_general(perm, ghi, nt, preferred_element_type=F32)
                 + lax.dot_general(perm, glo, nt, preferred_element_type=F32))


def outproj(x, hg_out, fox_o, mod, fox_gain, w_out, gain2, wr_hi, wr_lo, b_router, s, ls):
    b, l, d = x.shape
    nblk = (b // s) * (l // ls)
    nl = l // ls
    tm = s * ls
    row = lambda i, j: (i, j, 0)
    const = lambda i, j: (0, 0)
    blk = lambda i, j: (i * nl + j, 0, 0)
    return pl.pallas_call(
        _outproj_kernel,
        out_shape=[jax.ShapeDtypeStruct((b, l, d), F32),
                   jax.ShapeDtypeStruct((nblk, MOE_ROWS, d), BF16),
                   jax.ShapeDtypeStruct((nblk, MOE_ROWS, LANES), F32),
                   jax.ShapeDtypeStruct((nblk, 1, tm), jnp.int32),
                   jax.ShapeDtypeStruct((nblk, 1, LANES), jnp.int32)],
        grid=(b // s, nl),
        in_specs=[pl.BlockSpec((s, ls, d), row),
                  pl.BlockSpec((s, ls, HG_WIDTH), row),
                  pl.BlockSpec((s, ls, FOX_WIDTH), row),
                  pl.BlockSpec((s, N_MOD, d), lambda i, j: (i, 0, 0)),
                  pl.BlockSpec((1, FOX_WIDTH), const),
                  pl.BlockSpec(w_out.shape, const, pipeline_mode=pl.Buffered(1)),
                  pl.BlockSpec((1, d), const),
                  pl.BlockSpec((N_EXPERTS, d), const),
                  pl.BlockSpec((N_EXPERTS, d), const),
                  pl.BlockSpec((N_EXPERTS, 1), const)],
        out_specs=[pl.BlockSpec((s, ls, d), row),
                   pl.BlockSpec((1, MOE_ROWS, d), blk),
                   pl.BlockSpec((1, MOE_ROWS, LANES), blk),
                   pl.BlockSpec((1, 1, tm), blk),
                   pl.BlockSpec((1, 1, LANES), blk)],
        compiler_params=_params(("parallel", "parallel")),
        name="outproj",
    )(x, hg_out, fox_o, mod, fox_gain, w_out, gain2, wr_hi, wr_lo, b_router)


def _moe_tables(cnt, rows_per_step):
    nblk = cnt.shape[0]
    tiles_per_step = rows_per_step // MOE_TILE
    max_tiles = nblk * (BLOCK_TOKENS // MOE_TILE + N_GROUPS)
    n_steps = -(-max_tiles // tiles_per_step) + N_GROUPS
    npair = N_GROUPS * nblk
    start_gm = (jnp.cumsum(cnt, axis=1) - cnt).T.reshape(-1)
    cnt_gm = cnt.T.reshape(-1)
    csum = jnp.cumsum(cnt_gm)
    tau = jnp.arange(max_tiles, dtype=jnp.int32)
    pair = jnp.minimum(jnp.sum((csum[None, :] <= tau[:, None]).astype(jnp.int32), axis=1), npair - 1)
    hot = (pair[:, None] == jnp.arange(npair, dtype=jnp.int32)[None, :]).astype(jnp.int32)
    k = tau - jnp.sum(hot * (csum - cnt_gm)[None, :], axis=1)
    valid = tau < csum[-1]
    tile_blk = jnp.where(valid, pair % nblk, 0).astype(jnp.int32)
    tile_row = jnp.where(valid, (jnp.sum(hot * start_gm[None, :], axis=1) + k) * MOE_TILE, 0).astype(jnp.int32)
    tot = jnp.sum(cnt, axis=0)
    gstart = jnp.cumsum(tot) - tot
    nsteps = (tot + tiles_per_step - 1) // tiles_per_step
    send = jnp.cumsum(nsteps)
    s = jnp.arange(n_steps, dtype=jnp.int32)
    sg = jnp.minimum(jnp.sum((send[None, :] <= s[:, None]).astype(jnp.int32), axis=1), N_GROUPS - 1)
    ghot = (sg[:, None] == jnp.arange(N_GROUPS, dtype=jnp.int32)[None, :]).astype(jnp.int32)
    pick = lambda v: jnp.sum(ghot * v[None, :], axis=1)
    first = pick(gstart) + (s - pick(send - nsteps)) * tiles_per_step
    num = jnp.clip(pick(gstart + tot) - first, 0, tiles_per_step)
    first = jnp.where(num > 0, first, 0)
    return sg.astype(jnp.int32), first.astype(jnp.int32), num.astype(jnp.int32), tile_blk, tile_row


def _moe_kernel(sg_ref, first_ref, num_ref, tblk_ref, trow_ref,
                hs_hbm, gs_hbm, wg_ref, wu_ref, wd_ref, yinit_hbm, ys_hbm,
                hbuf, gbuf, ybuf, sem_in, sem_out):
    del sg_ref, yinit_hbm
    s = pl.program_id(0)
    first = first_ref[s]
    num = num_ref[s]

    @pl.when(s == 0)
    def _():
        hbuf[...] = jnp.zeros_like(hbuf)
        gbuf[...] = jnp.zeros_like(gbuf)

    def h_copy(k):
        t = first + k
        row = pl.multiple_of(trow_ref[t], MOE_TILE)
        return pltpu.make_async_copy(hs_hbm.at[tblk_ref[t], pl.ds(row, MOE_TILE), :],
                                     hbuf.at[pl.ds(pl.multiple_of(k * MOE_TILE, MOE_TILE), MOE_TILE), :],
                                     sem_in.at[0])

    def g_copy(k):
        t = first + k
        row = pl.multiple_of(trow_ref[t], MOE_TILE)
        return pltpu.make_async_copy(gs_hbm.at[tblk_ref[t], pl.ds(row, MOE_TILE), :],
                                     gbuf.at[pl.ds(pl.multiple_of(k * MOE_TILE, MOE_TILE), MOE_TILE), :],
                                     sem_in.at[1])

    def y_copy(k):
        t = first + k
        row = pl.multiple_of(trow_ref[t], MOE_TILE)
        return pltpu.make_async_copy(ybuf.at[pl.ds(pl.multiple_of(k * MOE_TILE, MOE_TILE), MOE_TILE), :],
                                     ys_hbm.at[tblk_ref[t], pl.ds(row, MOE_TILE), :],
                                     sem_out.at[0])

    def start_in(k, c):
        h_copy(k).start()
        g_copy(k).start()
        return c

    def wait_in(k, c):
        h_copy(k).wait()
        g_copy(k).wait()
        return c

    lax.fori_loop(0, num, start_in, 0)
    lax.fori_loop(0, num, wait_in, 0)

    @pl.when(num > 0)
    def _():
        h = hbuf[...]
        acc = jnp.zeros((hbuf.shape[0], wd_ref.shape[2]), F32)
        for k in range(GROUP_SIZE):
            a = jnp.dot(h, wg_ref[k].astype(BF16), preferred_element_type=F32)
            u = jnp.dot(h, wu_ref[k].astype(BF16), preferred_element_type=F32)
            he = (_silu(a) * u * gbuf[:, k:k + 1]).astype(BF16)
            acc = acc + jnp.dot(he, wd_ref[k].astype(BF16), preferred_element_type=F32)
        ybuf[...] = acc.astype(ybuf.dtype)

    def start_out(k, c):
        y_copy(k).start()
        return c

    def wait_out(k, c):
        y_copy(k).wait()
        return c

    lax.fori_loop(0, num, start_out, 0)
    lax.fori_loop(0, num, wait_out, 0)


def moe_sparse(hs, gs, cnt, wg, wu, wd, layer):
    nblk, rr, d = hs.shape
    de = wg.shape[2]
    rows = MOE_STEP_ROWS
    sg, first, num, tile_blk, tile_row = _moe_tables(cnt, rows)
    n_steps = sg.shape[0]
    wmap = lambda s, sg, *_: (layer * N_GROUPS + sg[s], 0, 0)
    grid_spec = pltpu.PrefetchScalarGridSpec(
        num_scalar_prefetch=5,
        grid=(n_steps,),
        in_specs=[pl.BlockSpec(memory_space=pl.ANY),
                  pl.BlockSpec(memory_space=pl.ANY),
                  pl.BlockSpec((GROUP_SIZE, d, de), wmap, pipeline_mode=pl.Buffered(1)),
                  pl.BlockSpec((GROUP_SIZE, d, de), wmap, pipeline_mode=pl.Buffered(1)),
                  pl.BlockSpec((GROUP_SIZE, de, d), wmap, pipeline_mode=pl.Buffered(1)),
                  pl.BlockSpec(memory_space=pl.ANY)],
        out_specs=pl.BlockSpec(memory_space=pl.ANY),
        scratch_shapes=[pltpu.VMEM((rows, d), BF16),
                        pltpu.VMEM((rows, LANES), F32),
                        pltpu.VMEM((rows, d), BF16),
                        pltpu.SemaphoreType.DMA((2,)),
                        pltpu.SemaphoreType.DMA((1,))])
    return pl.pallas_call(
        _moe_kernel,
        out_shape=jax.ShapeDtypeStruct((nblk, rr, d), BF16),
        grid_spec=grid_spec,
        input_output_aliases={10: 0},
        compiler_params=_params(("arbitrary",)),
        name="moe",
    )(sg, first, num, tile_blk, tile_row, hs, gs, wg, wu, wd, jnp.zeros((nblk, rr, d), BF16))


def _combine_kernel(x_ref, ys_ref, dest_ref, mod_ref, fgain_ref, o_ref, *, final):
    s, ls, d = x_ref.shape
    tm = s * ls
    rr = ys_ref.shape[1]
    dcol = jnp.broadcast_to(dest_ref[0].astype(F32), (LANES, tm)).T
    lane = lax.broadcasted_iota(jnp.int32, (tm, LANES), 1).astype(F32)
    unperm = jnp.concatenate(
        [jnp.where(dcol == lane + float(c * LANES), 1.0, 0.0).astype(BF16) for c in range(rr // LANES)],
        axis=1)
    y = jnp.dot(unperm, ys_ref[0], preferred_element_type=F32)
    x = x_ref[...] + mod_ref[:, 5:6, :] * y.reshape(s, ls, d)
    if final:
        x = x * lax.rsqrt(jnp.mean(x * x, axis=-1, keepdims=True) + EPS) * fgain_ref[...]
    o_ref[...] = x


def combine(x, ys, dest, mod, final_gain, s, ls, final):
    b, l, d = x.shape
    nl = l // ls
    tm = s * ls
    row = lambda i, j: (i, j, 0)
    blk = lambda i, j: (i * nl + j, 0, 0)
    return pl.pallas_call(
        functools.partial(_combine_kernel, final=final),
        out_shape=jax.ShapeDtypeStruct((b, l, d), F32),
        grid=(b // s, nl),
        in_specs=[pl.BlockSpec((s, ls, d), row),
                  pl.BlockSpec((1, ys.shape[1], d), blk),
                  pl.BlockSpec((1, 1, tm), blk),
                  pl.BlockSpec((s, N_MOD, d), lambda i, j: (i, 0, 0)),
                  pl.BlockSpec((1, d), lambda i, j: (0, 0))],
        out_specs=pl.BlockSpec((s, ls, d), row),
        compiler_params=_params(("parallel", "parallel")),
        name="combine",
    )(x, ys, dest, mod, final_gain)


def _tile(b, l):
    ls = min(l, BLOCK_TOKENS)
    s = BLOCK_TOKENS // ls
    assert s * ls == BLOCK_TOKENS and b % s == 0 and l % ls == 0
    assert BLOCK_TOKENS + N_GROUPS * MOE_TILE <= MOE_ROWS
    return s, ls


def _layer(x, mod, l, p, hg_state0, fox_past, kv_prev, final):
    depth = p['norm_mix_gain'].shape[0]
    b, seq, d = x.shape
    s, ls = _tile(b, seq)
    sample = fox_past is not None
    hg_in, fq, kbuf, vbuf, *k16v16, ff = inproj(x, mod, p['norm_mix_gain'][l], p['w_in'][l], s, ls,
                                                l, depth, kv_prev, sample)

    hg_out, hg_state = hgrn(hg_in, hg_state0, p['hg_lower_bounds'], p['hg_norm_gain'][l], l)

    if not sample:
        tq = min(seq, BLOCK_TOKENS)
        qp, kp, vt, lf_pad = foxpack(fq, kbuf, vbuf, l, ff, p['fox_f_bias_pad'][l], tq)
        fox_o = fox_prompt(qp, kp, vt, tq)
        logf = lf_pad[:, :, :FOX_HEADS]
    else:
        kcache, vcache, plogf = fox_past
        past = kcache.shape[2]
        fft = jnp.swapaxes(ff[:, :, :FOX_HEADS], 1, 2)
        logft, _ = forget_cumsum(fft, p['fox_f_bias'][l], True, seq)
        tot = past + seq
        pad = (-tot) % LANES
        allt = jnp.concatenate([jnp.swapaxes(plogf[l], 1, 2), logft,
                                jnp.zeros((b, FOX_HEADS, pad), F32)], axis=2)
        _, ct = forget_cumsum(allt, p['fox_f_bias'][l], False, LANES)
        ct = ct * LOG2E
        cq = jnp.swapaxes(ct[:, :, past:tot], 1, 2)
        fox_o = fox_sample(fq, kcache, vcache, l, k16v16[0], k16v16[1], cq, ct)
        logf = jnp.swapaxes(logft, 1, 2)

    x_mid, hs, gs, dest, cnt = outproj(x, hg_out, fox_o, mod, p['fox_out_gain'][l], p['w_out'][l],
                                       p['norm_ffn_gain'][l], p['wr_hi'], p['wr_lo'], p['b_router'], s, ls)
    ys = moe_sparse(hs, gs, cnt[:, 0, :N_GROUPS], p['w_exp_gate'], p['w_exp_up'], p['w_exp_down'], l)
    x_new = combine(x_mid, ys, dest, mod, p['final_norm_gain'], s, ls, final)
    return x_new, (kbuf, vbuf), hg_state, logf


def kernel(x_prompt, x_sample, cache_fox_k, cache_fox_v, cache_fox_logf, state_hgrn, c_prompt, c_sample,
           norm_mix_gain, norm_ffn_gain, w_ada, b_ada, w_in, hg_lower_bounds, hg_norm_gain,
           fox_f_bias, fox_out_gain, w_out, w_router, b_router, w_exp_gate, w_exp_up, w_exp_down,
           final_norm_gain):
    depth, d = norm_mix_gain.shape
    bp = x_prompt.shape[0]
    n_in = w_in.shape[2]
    n_pad = 4 * HG_WIDTH + 3 * FOX_WIDTH + LANES - n_in
    wr_t = w_router.T
    wr_hi = wr_t.astype(BF16)
    p = {
        'norm_mix_gain': norm_mix_gain.reshape(depth, 1, d),
        'norm_ffn_gain': norm_ffn_gain.reshape(depth, 1, d),
        'w_in': jnp.pad(w_in, ((0, 0), (0, 0), (0, n_pad))).astype(BF16),
        'hg_lower_bounds': hg_lower_bounds,
        'hg_norm_gain': hg_norm_gain.reshape(depth, 1, HG_WIDTH),
        'fox_f_bias': fox_f_bias.reshape(depth, FOX_HEADS, 1),
        'fox_f_bias_pad': jnp.pad(fox_f_bias, ((0, 0), (0, LANES - FOX_HEADS))).reshape(depth, 1, LANES),
        'fox_out_gain': fox_out_gain.reshape(depth, 1, FOX_WIDTH),
        'w_out': w_out.astype(BF16),
        'wr_hi': wr_hi,
        'wr_lo': (wr_t - wr_hi.astype(F32)).astype(BF16),
        'b_router': b_router.reshape(N_EXPERTS, 1),
        'w_exp_gate': w_exp_gate.reshape((depth * N_EXPERTS,) + w_exp_gate.shape[2:]),
        'w_exp_up': w_exp_up.reshape((depth * N_EXPERTS,) + w_exp_up.shape[2:]),
        'w_exp_down': w_exp_down.reshape((depth * N_EXPERTS,) + w_exp_down.shape[2:]),
        'final_norm_gain': final_norm_gain.reshape(1, d),
    }
    mods = ada_mod(jnp.concatenate([c_prompt, c_sample], axis=0), w_ada, b_ada)
    mods = mods.reshape(depth, -1, N_MOD, d)
    bs, lsq = x_sample.shape[:2]
    lp = x_prompt.shape[1]
    past = cache_fox_k.shape[2]
    fox_past = (cache_fox_k.reshape(depth, bs, past, FOX_WIDTH),
                cache_fox_v.reshape(depth, bs, past, FOX_WIDTH), cache_fox_logf)

    xp, xs = x_prompt, x_sample
    kv_p = kv_s = None
    st_p, st_s, lf_p, lf_s = [], [], [], []
    zero_state = jnp.zeros((bp, HG_HEADS, HG_DIM, HG_DIM), F32)
    for l in range(depth):
        final = l == depth - 1
        xp, kv_p, st, lf = _layer(xp, mods[l, :bp], l, p, zero_state, None, kv_p, final)
        st_p.append(st)
        lf_p.append(lf)
        xs, kv_s, st, lf = _layer(xs, mods[l, bp:], l, p, state_hgrn[l], fox_past, kv_s, final)
        st_s.append(st)
        lf_s.append(lf)
    heads = lambda a, b, l: a.reshape(depth, b, l, FOX_HEADS, FOX_DIM)
    return (xp, xs,
            jnp.stack(st_p), heads(kv_p[0], bp, lp), heads(kv_p[1], bp, lp), jnp.stack(lf_p),
            jnp.stack(st_s), heads(kv_s[0], bs, lsq), heads(kv_s[1], bs, lsq), jnp.stack(lf_s))
```

```python
import functools
import math

import numpy as np
import jax
import jax.numpy as jnp
from jax import lax
from jax.experimental import pallas as pl
from jax.experimental.pallas import tpu as pltpu

HG_HEADS = 4
HG_DIM = 128
HG_WIDTH = HG_HEADS * HG_DIM
FOX_HEADS = 8
FOX_DIM = 64
FOX_WIDTH = FOX_HEADS * FOX_DIM
N_EXPERTS = 16
N_GROUPS = 4
GROUP_SIZE = N_EXPERTS // N_GROUPS
TOP_K = 2
N_MOD = 6
EPS = 1e-6

LANES = 128
HG_CHUNK = 128
HG_SUB = 16
HG_STREAMS = 2
MOE_TILE = 16
BLOCK_TOKENS = 512
MOE_ROWS = 640
MOE_STEP_ROWS = 1024
VMEM_LIMIT = 56 * 1024 * 1024

F32 = jnp.float32
BF16 = jnp.bfloat16
NEG = -1e30
LOG2E = math.log2(math.e)


def _params(sem, vmem=VMEM_LIMIT):
    return pltpu.CompilerParams(dimension_semantics=sem, vmem_limit_bytes=vmem)


def _split3(x):
    hi = x.astype(BF16)
    r1 = x - hi.astype(F32)
    mid = r1.astype(BF16)
    lo = (r1 - mid.astype(F32)).astype(BF16)
    return hi, mid, lo


def _log_sigmoid(x):
    return jnp.minimum(x, 0.0) - jnp.log1p(jnp.exp(-jnp.abs(x)))


def _silu(x):
    return x * jax.nn.sigmoid(x)


def _ada_kernel(c_ref, w_ref, b_ref, o_ref):
    s = _silu(c_ref[...]).astype(BF16)
    o_ref[0] = jnp.dot(s, w_ref[0].astype(BF16), preferred_element_type=F32) + b_ref[0]


def ada_mod(c, w_ada, b_ada):
    depth, d, n = w_ada.shape
    nb = c.shape[0]
    tn = 1536
    return pl.pallas_call(
        _ada_kernel,
        out_shape=jax.ShapeDtypeStruct((depth, nb, n), F32),
        grid=(depth, n // tn),
        in_specs=[pl.BlockSpec((nb, d), lambda l, j: (0, 0)),
                  pl.BlockSpec((1, d, tn), lambda l, j: (l, 0, j)),
                  pl.BlockSpec((1, 1, tn), lambda l, j: (l, 0, j))],
        out_specs=pl.BlockSpec((1, nb, tn), lambda l, j: (l, 0, j)),
        compiler_params=_params(("arbitrary", "arbitrary")),
        name="ada_mod",
    )(c, w_ada, b_ada.reshape(depth, 1, n))


def _inproj_kernel(x_ref, mod_ref, gain_ref, w_ref, *refs, has_prev, emit16):
    outs = refs[2:] if has_prev else refs
    hg_ref, fq_ref, fk32_ref, fv32_ref = outs[:4]
    ff_ref = outs[-1]
    s, ls, d = x_ref.shape
    x = x_ref[...]
    y = x * lax.rsqrt(jnp.mean(x * x, axis=-1, keepdims=True) + EPS) * gain_ref[...]
    h = y * (1.0 + mod_ref[:, 1:2, :]) + mod_ref[:, 0:1, :]
    hb = h.reshape(s * ls, d).astype(BF16)

    def proj(lo, hi):
        return jnp.dot(hb, w_ref[:, lo:hi], preferred_element_type=F32)

    c0 = 4 * HG_WIDTH
    hg_ref[...] = proj(0, c0).reshape(s, ls, c0)
    fq = proj(c0, c0 + FOX_WIDTH) * (FOX_DIM ** -0.5 * LOG2E)
    fq_ref[...] = fq.reshape(s, ls, FOX_WIDTH).astype(BF16)
    fk = proj(c0 + FOX_WIDTH, c0 + 2 * FOX_WIDTH).reshape(s, ls, FOX_WIDTH)
    fv = proj(c0 + 2 * FOX_WIDTH, c0 + 3 * FOX_WIDTH).reshape(s, ls, FOX_WIDTH)
    for slot in range(fk32_ref.shape[0]):
        fk32_ref[slot] = fk
        fv32_ref[slot] = fv
    if emit16:
        outs[4][...] = fk.astype(BF16)
        outs[5][...] = fv.astype(BF16)
    ff_ref[...] = proj(c0 + 3 * FOX_WIDTH, c0 + 3 * FOX_WIDTH + LANES).reshape(s, ls, LANES)


def inproj(x, mod, gain, w_pad, s, ls, layer, depth, kv_prev, emit16):
    b, l, d = x.shape
    n = w_pad.shape[1]
    row = lambda i, j: (i, j, 0)
    lrow = lambda i, j: (layer, i, j, 0)
    shapes = [jax.ShapeDtypeStruct((b, l, 4 * HG_WIDTH), F32),
              jax.ShapeDtypeStruct((b, l, FOX_WIDTH), BF16),
              jax.ShapeDtypeStruct((depth, b, l, FOX_WIDTH), F32),
              jax.ShapeDtypeStruct((depth, b, l, FOX_WIDTH), F32)]
    slots = depth if kv_prev is None else 1
    if kv_prev is None:
        assert layer == 0
    specs = [pl.BlockSpec((s, ls, 4 * HG_WIDTH), row),
             pl.BlockSpec((s, ls, FOX_WIDTH), row),
             pl.BlockSpec((slots, s, ls, FOX_WIDTH), lrow),
             pl.BlockSpec((slots, s, ls, FOX_WIDTH), lrow)]
    if emit16:
        shapes += [jax.ShapeDtypeStruct((b, l, FOX_WIDTH), BF16)] * 2
        specs += [pl.BlockSpec((s, ls, FOX_WIDTH), row)] * 2
    shapes.append(jax.ShapeDtypeStruct((b, l, LANES), F32))
    specs.append(pl.BlockSpec((s, ls, LANES), row))
    in_specs = [pl.BlockSpec((s, ls, d), row),
                pl.BlockSpec((s, N_MOD, d), lambda i, j: (i, 0, 0)),
                pl.BlockSpec((1, d), lambda i, j: (0, 0)),
                pl.BlockSpec((d, n), lambda i, j: (0, 0), pipeline_mode=pl.Buffered(1))]
    args = [x, mod, gain, w_pad]
    aliases = {}
    if kv_prev is not None:
        in_specs += [pl.BlockSpec(memory_space=pl.ANY)] * 2
        args += list(kv_prev)
        aliases = {4: 2, 5: 3}
    return pl.pallas_call(
        functools.partial(_inproj_kernel, has_prev=kv_prev is not None, emit16=emit16),
        out_shape=shapes,
        grid=(b // s, l // ls),
        in_specs=in_specs,
        out_specs=specs,
        input_output_aliases=aliases,
        compiler_params=_params(("parallel", "parallel")),
        name="inproj",
    )(*args)


def _cumsum_kernel(x_ref, bias_ref, lf_ref, c_ref, carry_ref, *, apply_ls):
    @pl.when(pl.program_id(1) == 0)
    def _():
        carry_ref[...] = jnp.zeros_like(carry_ref)

    bb, nh, tc = x_ref.shape
    x = x_ref[...].reshape(bb * nh, tc)
    lf = _log_sigmoid(x + bias_ref[...]) if apply_ls else x
    lf_ref[...] = lf.reshape(bb, nh, tc)
    r = lax.broadcasted_iota(jnp.int32, (tc, tc), 0)
    c = lax.broadcasted_iota(jnp.int32, (tc, tc), 1)
    tri = jnp.where(r <= c, 1.0, 0.0).astype(BF16)
    hi, mid, lo = _split3(lf)
    tot = (jnp.dot(hi, tri, preferred_element_type=F32)
           + jnp.dot(mid, tri, preferred_element_type=F32)
           + jnp.dot(lo, tri, preferred_element_type=F32)) + carry_ref[:, 0:1]
    c_ref[...] = tot.reshape(bb, nh, tc)
    carry_ref[...] = jnp.broadcast_to(tot[:, tc - 1:tc], carry_ref.shape)


def forget_cumsum(xt, bias, apply_ls, tc):
    b, h, l = xt.shape
    blk = pl.BlockSpec((b, h, tc), lambda i, j: (0, 0, j))
    return pl.pallas_call(
        functools.partial(_cumsum_kernel, apply_ls=apply_ls),
        out_shape=[jax.ShapeDtypeStruct((b, h, l), F32)] * 2,
        grid=(1, l // tc),
        in_specs=[blk, pl.BlockSpec((b * h, 1), lambda i, j: (0, 0))],
        out_specs=[blk, blk],
        scratch_shapes=[pltpu.VMEM((b * h, LANES), F32)],
        compiler_params=_params(("arbitrary", "arbitrary")),
        name="forget_cumsum",
    )(xt, jnp.tile(bias, (b, 1)))


HG_HALVES = (8, 4, 2, 1)


def _hgrn_sum_matrix():
    c = HG_CHUNK
    t = np.arange(c)[:, None]
    u = np.arange(c)[None, :]
    mats = [(u <= t)]
    for w in HG_HALVES:
        pos = t % (2 * w)
        mid = t - pos + w - 1
        upper = pos >= w
        mats.append(np.where(upper, (u > mid) & (u <= t), (u > t) & (u <= mid)))
    return jnp.asarray(np.concatenate(mats, axis=0).astype(np.float32), BF16)


def _hgrn_kernel(hin_ref, s0_ref, lbp_ref, gain_ref, sums_ref, out_ref, sfin_ref, st_ref, *, layer):
    t = pl.program_id(1)
    nt = pl.num_programs(1)
    c = HG_CHUNK
    nsub = c // HG_SUB
    nb = hin_ref.shape[0]
    rows_in = hin_ref.shape[1]

    def pad_rows(a):
        if rows_in == c:
            return a
        return jnp.concatenate([a, jnp.zeros((c - rows_in, a.shape[1]), a.dtype)], axis=0)

    @pl.when(t == 0)
    def _():
        for bi in range(nb):
            for h in range(HG_HEADS):
                st_ref[bi, h] = s0_ref[bi, h].T

    lbp = lbp_ref[...]
    e = jnp.exp(lbp - jnp.max(lbp, axis=0, keepdims=True))
    p = e / jnp.sum(e, axis=0, keepdims=True)
    acc = p[0:1]
    first = acc
    for i in range(1, layer + 1):
        acc = acc + p[i:i + 1]
    lb_all = acc - first

    ri = lax.broadcasted_iota(jnp.int32, (c, c), 0)
    ci = lax.broadcasted_iota(jnp.int32, (c, c), 1)
    ones = jnp.ones((HG_DIM, HG_DIM), BF16)
    nt_dims = (((1,), (1,)), ((), ()))

    streams = [(bi, h) for bi in range(nb) for h in range(HG_HEADS)]
    q, kk, v, b, sums, att = {}, {}, {}, {}, {}, {}
    for bi, h in streams:
        lb = lb_all[:, h * HG_DIM:(h + 1) * HG_DIM]
        hq = hin_ref[bi, :, h * HG_DIM:(h + 1) * HG_DIM]
        hf = hin_ref[bi, :, HG_WIDTH + h * HG_DIM:HG_WIDTH + (h + 1) * HG_DIM]
        hi = hin_ref[bi, :, 2 * HG_WIDTH + h * HG_DIM:2 * HG_WIDTH + (h + 1) * HG_DIM]
        q[bi, h] = pad_rows(_silu(hq) * (HG_DIM ** -0.5))
        a1 = jnp.log(lb)
        a2 = jnp.log1p(-lb) + _log_sigmoid(hf)
        mx = jnp.maximum(a1, a2)
        lf = pad_rows(mx + jnp.log(jnp.exp(a1 - mx) + jnp.exp(a2 - mx)))
        kk[bi, h] = pad_rows((1.0 - lb) * jax.nn.sigmoid(-hf))
        v[bi, h] = pad_rows(hi)
        lhi, lmid, llo = _split3(lf)
        sums[bi, h] = (jnp.dot(sums_ref[...], lhi, preferred_element_type=F32)
                       + jnp.dot(sums_ref[...], lmid, preferred_element_type=F32)
                       + jnp.dot(sums_ref[...], llo, preferred_element_type=F32))
        b[bi, h] = sums[bi, h][0:c]

    for key in streams:
        att[key] = jnp.where(ri == ci, jnp.dot((q[key] * kk[key]).astype(BF16), ones,
                                               preferred_element_type=F32), 0.0)
    for lvl, w in enumerate(HG_HALVES):
        upper = (ri & (2 * w - 1)) >= w
        shift = (2 * w).bit_length() - 1
        same = (ri >> shift) == (ci >> shift)
        for key in streams:
            dec = jnp.exp(sums[key][(lvl + 1) * c:(lvl + 2) * c])
            qs = jnp.where(upper, q[key] * dec, 0.0).astype(BF16)
            ks = jnp.where(upper, 0.0, kk[key] * dec).astype(BF16)
            sc = lax.dot_general(qs, ks, nt_dims, preferred_element_type=F32)
            att[key] = att[key] + jnp.where(same, sc, 0.0)

    khat = {key: None for key in streams}
    r_prev = {key: jnp.zeros((1, HG_DIM), F32) for key in streams}
    qhat_blocks = {key: [] for key in streams}
    cross = {key: [] for key in streams}
    for i in range(nsub):
        r0 = i * HG_SUB
        for key in streams:
            bb = b[key][r0:r0 + HG_SUB]
            r_next = b[key][r0 + HG_SUB - 1:r0 + HG_SUB]
            qt = q[key][r0:r0 + HG_SUB] * jnp.exp(bb - r_prev[key])
            if khat[key] is None:
                cross[key].append(jnp.zeros((HG_SUB, c), F32))
            else:
                kfull = jnp.concatenate([khat[key], jnp.zeros((c - r0, HG_DIM), F32)], axis=0)
                cross[key].append(lax.dot_general(qt.astype(BF16), kfull.astype(BF16), nt_dims,
                                                  preferred_element_type=F32))
            qhat_blocks[key].append(qt * jnp.exp(r_prev[key]))
            kt = kk[key][r0:r0 + HG_SUB] * jnp.exp(r_next - bb)
            if khat[key] is None:
                khat[key] = kt
            else:
                khat[key] = jnp.concatenate([khat[key] * jnp.exp(r_next - r_prev[key]), kt], axis=0)
            r_prev[key] = r_next

    for bi, h in streams:
        key = (bi, h)
        cols = slice(h * HG_DIM, (h + 1) * HG_DIM)
        st = st_ref[bi, h]
        a = att[key] + jnp.concatenate(cross[key], axis=0)
        qhat = jnp.concatenate(qhat_blocks[key], axis=0).astype(BF16)
        o = jnp.dot(a.astype(BF16), v[key].astype(BF16), preferred_element_type=F32)
        o = o + lax.dot_general(qhat, st.astype(BF16), nt_dims, preferred_element_type=F32)
        st_ref[bi, h] = st * jnp.exp(r_prev[key]) + jnp.dot(
            v[key].T.astype(BF16), khat[key].astype(BF16), preferred_element_type=F32)

        hg = hin_ref[bi, :, 3 * HG_WIDTH + h * HG_DIM:3 * HG_WIDTH + (h + 1) * HG_DIM]
        o = o[:rows_in]
        o = o * lax.rsqrt(jnp.mean(o * o, axis=-1, keepdims=True) + EPS)
        o = o * gain_ref[:, cols] * _silu(hg)
        out_ref[bi, :, cols] = o.astype(out_ref.dtype)

    @pl.when(t == nt - 1)
    def _():
        for bi, h in streams:
            sfin_ref[bi, h] = st_ref[bi, h].T


def hgrn(hin, s0, lbp, gain, layer):
    b, l, w = hin.shape
    c = min(HG_CHUNK, l)
    assert l % c == 0 and c % HG_SUB == 0
    nt = l // c
    nb = HG_STREAMS
    assert b % nb == 0
    sums = _hgrn_sum_matrix()
    return pl.pallas_call(
        functools.partial(_hgrn_kernel, layer=layer),
        out_shape=[jax.ShapeDtypeStruct((b, l, HG_WIDTH), BF16),
                   jax.ShapeDtypeStruct((b, HG_HEADS, HG_DIM, HG_DIM), F32)],
        grid=(b // nb, nt),
        in_specs=[pl.BlockSpec((nb, c, w), lambda i, j: (i, j, 0)),
                  pl.BlockSpec((nb, HG_HEADS, HG_DIM, HG_DIM), lambda i, j: (i, 0, 0, 0)),
                  pl.BlockSpec(lbp.shape, lambda i, j: (0, 0)),
                  pl.BlockSpec((1, HG_WIDTH), lambda i, j: (0, 0)),
                  pl.BlockSpec(sums.shape, lambda i, j: (0, 0))],
        out_specs=[pl.BlockSpec((nb, c, HG_WIDTH), lambda i, j: (i, j, 0)),
                   pl.BlockSpec((nb, HG_HEADS, HG_DIM, HG_DIM), lambda i, j: (i, 0, 0, 0))],
        scratch_shapes=[pltpu.VMEM((nb, HG_HEADS, HG_DIM, HG_DIM), F32)],
        compiler_params=_params(("parallel", "arbitrary")),
        name="hgrn",
    )(hin, s0, lbp, gain, sums)


AUG_ONE = FOX_DIM
AUG_NEG = FOX_DIM + 3
ONE_LANE = FOX_HEADS
FOX_GROUP = 4
FOX_VROWS = FOX_DIM + 16


def _placement():
    pk = np.zeros((3 * LANES, FOX_HEADS * LANES), np.float32)
    pq = np.zeros((3 * LANES, FOX_HEADS * LANES), np.float32)
    for h in range(FOX_HEADS):
        for part in range(3):
            pk[part * LANES + h, h * LANES + AUG_NEG + part] = -1.0
            pq[part * LANES + h, h * LANES + AUG_ONE + part] = 1.0
            pk[ONE_LANE, h * LANES + AUG_ONE + part] = 1.0
            pq[ONE_LANE, h * LANES + AUG_NEG + part] = 1.0
    return jnp.asarray(pk, BF16), jnp.asarray(pq, BF16)


def _foxpack_kernel(q_ref, k_ref, v_ref, ff_ref, bias_ref, pk_ref, pq_ref,
                    qp_ref, kp_ref, vt_ref, lf_ref, carry_ref):
    @pl.when(pl.program_id(1) == 0)
    def _():
        carry_ref[...] = jnp.zeros_like(carry_ref)

    tm = q_ref.shape[1]
    lane = lax.broadcasted_iota(jnp.int32, (tm, LANES), 1)
    lf = jnp.where(lane < FOX_HEADS, _log_sigmoid(ff_ref[0] + bias_ref[...]), 0.0)
    lf_ref[0] = lf
    ri = lax.broadcasted_iota(jnp.int32, (tm, tm), 0)
    ci = lax.broadcasted_iota(jnp.int32, (tm, tm), 1)
    tril = jnp.where(ci <= ri, 1.0, 0.0).astype(BF16)
    hi, mid, lo = _split3(lf)
    c = (jnp.dot(tril, hi, preferred_element_type=F32)
         + jnp.dot(tril, mid, preferred_element_type=F32)
         + jnp.dot(tril, lo, preferred_element_type=F32)) + carry_ref[0:1, :]
    carry_ref[...] = jnp.broadcast_to(c[tm - 1:tm, :], carry_ref.shape)

    chi, cmid, clo = _split3(c * LOG2E)
    chi = jnp.where(lane == ONE_LANE, 1.0, chi.astype(F32)).astype(BF16)
    cterms = jnp.concatenate([chi, cmid, clo], axis=1)
    augk = jnp.dot(cterms, pk_ref[...], preferred_element_type=F32)
    augq = jnp.dot(cterms, pq_ref[...], preferred_element_type=F32)
    for h in range(FOX_HEADS):
        blk = slice((h // 2) * LANES, (h // 2 + 1) * LANES)
        hcols = slice(h * LANES, (h + 1) * LANES)
        kh = k_ref[0, 0, :, blk]
        qh = q_ref[0, :, blk].astype(F32)
        if h % 2:
            kh = pltpu.roll(kh, FOX_DIM, axis=1)
            qh = pltpu.roll(qh, FOX_DIM, axis=1)
        kp_ref[0, h] = jnp.where(lane < FOX_DIM, kh, augk[:, hcols]).astype(BF16)
        qp_ref[0, h] = jnp.where(lane < FOX_DIM, qh, augq[:, hcols]).astype(BF16)
    vt = v_ref[0, 0].T
    tail = jnp.where(lax.broadcasted_iota(jnp.int32, (FOX_VROWS - FOX_DIM, tm), 0) == 0, 1.0, 0.0)
    for h in range(FOX_HEADS):
        vt_ref[0, h] = jnp.concatenate([vt[h * FOX_DIM:(h + 1) * FOX_DIM], tail], axis=0).astype(BF16)


def foxpack(fq, kbuf, vbuf, layer, ff, bias_pad, tm):
    b, l, w = fq.shape
    pk, pq = _placement()
    row = lambda i, j: (i, j, 0)
    lrow = lambda i, j: (layer, i, j, 0)
    const = lambda i, j: (0, 0)
    head_blk = pl.BlockSpec((1, FOX_HEADS, tm, LANES), lambda i, j: (i, 0, j, 0))
    return pl.pallas_call(
        _foxpack_kernel,
        out_shape=[jax.ShapeDtypeStruct((b, FOX_HEADS, l, LANES), BF16),
                   jax.ShapeDtypeStruct((b, FOX_HEADS, l, LANES), BF16),
                   jax.ShapeDtypeStruct((b, FOX_HEADS, FOX_VROWS, l), BF16),
                   jax.ShapeDtypeStruct((b, l, LANES), F32)],
        grid=(b, l // tm),
        in_specs=[pl.BlockSpec((1, tm, w), row),
                  pl.BlockSpec((1, 1, tm, w), lrow),
                  pl.BlockSpec((1, 1, tm, w), lrow),
                  pl.BlockSpec((1, tm, LANES), row),
                  pl.BlockSpec((1, LANES), const),
                  pl.BlockSpec(pk.shape, const),
                  pl.BlockSpec(pq.shape, const)],
        out_specs=[head_blk, head_blk,
                   pl.BlockSpec((1, FOX_HEADS, FOX_VROWS, tm), lambda i, j: (i, 0, 0, j)),
                   pl.BlockSpec((1, tm, LANES), row)],
        scratch_shapes=[pltpu.VMEM((8, LANES), F32)],
        compiler_params=_params(("parallel", "arbitrary")),
        name="foxpack",
    )(fq, kbuf, vbuf, ff, bias_pad, pk, pq)


def _fox_prompt_kernel(qi_ref, kj_ref, q_ref, k_ref, vt_ref, o_ref, m_ref, acc_ref):
    t = pl.program_id(1)
    i = qi_ref[t]
    j = kj_ref[t]
    tq = q_ref.shape[2]
    tk = k_ref.shape[2]
    nt = (((1,), (1,)), ((), ()))

    @pl.when(j == 0)
    def _():
        m_ref[...] = jnp.full_like(m_ref, NEG)
        acc_ref[...] = jnp.zeros_like(acc_ref)

    def step(diagonal):
        if diagonal:
            keep = (lax.broadcasted_iota(jnp.int32, (tk, tq), 0)
                    <= lax.broadcasted_iota(jnp.int32, (tk, tq), 1))
        for h0 in range(0, FOX_HEADS, FOX_GROUP):
            heads = range(h0, h0 + FOX_GROUP)
            st = {h: lax.dot_general(k_ref[0, h], q_ref[0, h], nt, preferred_element_type=F32)
                  for h in heads}
            if diagonal:
                st = {h: jnp.where(keep, s, NEG) for h, s in st.items()}
            m_old = {h: m_ref[h:h + 1, :] for h in heads}
            m_new = {h: jnp.maximum(m_old[h], jnp.max(st[h], axis=0, keepdims=True)) for h in heads}
            p = {h: jnp.exp2((st[h] - m_new[h]).astype(BF16)) for h in heads}
            for h in heads:
                alpha = jnp.exp2(m_old[h] - m_new[h])
                acc_ref[h] = alpha * acc_ref[h] + jnp.dot(vt_ref[0, h], p[h], preferred_element_type=F32)
                m_ref[h:h + 1, :] = m_new[h]

    @pl.when(j < i)
    def _():
        step(False)

    @pl.when(j == i)
    def _():
        step(True)
        outs = [acc_ref[h, 0:FOX_DIM, :] / acc_ref[h, FOX_DIM:FOX_DIM + 1, :] for h in range(FOX_HEADS)]
        o_ref[0] = jnp.concatenate(outs, axis=0).T.astype(o_ref.dtype)


def fox_prompt(qp, kp, vt, tq):
    b, nh, l, _ = qp.shape
    w = nh * FOX_DIM
    nq = l // tq
    qi = np.asarray([i for i in range(nq) for _ in range(i + 1)], np.int32)
    kj = np.asarray([j for i in range(nq) for j in range(i + 1)], np.int32)
    grid_spec = pltpu.PrefetchScalarGridSpec(
        num_scalar_prefetch=2,
        grid=(b, len(qi)),
        in_specs=[pl.BlockSpec((1, nh, tq, LANES), lambda bi, t, qi, kj: (bi, 0, qi[t], 0)),
                  pl.BlockSpec((1, nh, tq, LANES), lambda bi, t, qi, kj: (bi, 0, kj[t], 0)),
                  pl.BlockSpec((1, nh, FOX_VROWS, tq), lambda bi, t, qi, kj: (bi, 0, 0, kj[t]))],
        out_specs=pl.BlockSpec((1, tq, w), lambda bi, t, qi, kj: (bi, qi[t], 0)),
        scratch_shapes=[pltpu.VMEM((FOX_HEADS, tq), F32),
                        pltpu.VMEM((FOX_HEADS, FOX_VROWS, tq), F32)])
    return pl.pallas_call(
        _fox_prompt_kernel,
        out_shape=jax.ShapeDtypeStruct((b, l, w), BF16),
        grid_spec=grid_spec,
        compiler_params=_params(("parallel", "arbitrary")),
        name="fox_prompt",
    )(jnp.asarray(qi), jnp.asarray(kj), qp, kp, vt)


def _fox_sample_kernel(q_ref, kp_ref, vp_ref, kn_ref, vn_ref, cq_ref, ck_ref, o_ref):
    lq = q_ref.shape[1]
    past = kp_ref.shape[4]
    ri = lax.broadcasted_iota(jnp.int32, (lq, lq), 0)
    ci = lax.broadcasted_iota(jnp.int32, (lq, lq), 1)
    nt = (((1,), (1,)), ((), ()))
    for h in range(FOX_HEADS):
        cols = slice(h * FOX_DIM, (h + 1) * FOX_DIM)
        q = q_ref[0, :, cols]
        cq = cq_ref[0, :, h:h + 1]
        sp = jnp.dot(q, kp_ref[0, 0, h].astype(BF16), preferred_element_type=F32)
        sp = sp + cq - ck_ref[0, h:h + 1, 0:past]
        sn = lax.dot_general(q, kn_ref[0, :, cols], nt, preferred_element_type=F32)
        sn = sn + cq - ck_ref[0, h:h + 1, past:past + lq]
        sn = jnp.where(ci <= ri, sn, NEG)
        m = jnp.maximum(jnp.max(sp, axis=-1, keepdims=True), jnp.max(sn, axis=-1, keepdims=True))
        pp = jnp.exp2(sp - m)
        pn = jnp.exp2(sn - m)
        den = jnp.sum(pp, axis=-1, keepdims=True) + jnp.sum(pn, axis=-1, keepdims=True)
        o = (lax.dot_general(pp.astype(BF16), vp_ref[0, 0, h].astype(BF16), nt, preferred_element_type=F32)
             + jnp.dot(pn.astype(BF16), vn_ref[0, :, cols], preferred_element_type=F32))
        o_ref[0, :, cols] = (o / den).astype(o_ref.dtype)


def fox_sample(q, kcache_t, vcache_t, layer, kn, vn, cq, ckt):
    b, lq, w = q.shape
    past = kcache_t.shape[4]
    lc = ckt.shape[2]
    new = lambda i: (i, 0, 0)
    old = lambda i: (layer, i, 0, 0, 0)
    cache_blk = pl.BlockSpec((1, 1, FOX_HEADS, FOX_DIM, past), old)
    return pl.pallas_call(
        _fox_sample_kernel,
        out_shape=jax.ShapeDtypeStruct((b, lq, w), BF16),
        grid=(b,),
        in_specs=[pl.BlockSpec((1, lq, w), new),
                  cache_blk,
                  cache_blk,
                  pl.BlockSpec((1, lq, w), new),
                  pl.BlockSpec((1, lq, w), new),
                  pl.BlockSpec((1, lq, FOX_HEADS), new),
                  pl.BlockSpec((1, FOX_HEADS, lc), new)],
        out_specs=pl.BlockSpec((1, lq, w), new),
        compiler_params=_params(("parallel",)),
        name="fox_sample",
    )(q, kcache_t, vcache_t, kn, vn, cq, ckt)


def _outproj_kernel(x_ref, hg_ref, fo_ref, mod_ref, fgain_ref, w_ref, gain2_ref,
                    wrh_ref, wrl_ref, br_ref, xo_ref, hs_ref, gs_ref, dest_ref, cnt_ref):
    s, ls, d = x_ref.shape
    tm = s * ls
    fo = fo_ref[...].astype(F32)
    fn = fo * lax.rsqrt(jnp.mean(fo * fo, axis=-1, keepdims=True) + EPS) * fgain_ref[...]
    mixed = (jnp.dot(hg_ref[...].reshape(tm, HG_WIDTH), w_ref[0:HG_WIDTH, :],
                     preferred_element_type=F32)
             + jnp.dot(fn.reshape(tm, FOX_WIDTH).astype(BF16), w_ref[HG_WIDTH:, :],
                       preferred_element_type=F32))
    x = x_ref[...] + mod_ref[:, 2:3, :] * mixed.reshape(s, ls, d)
    xo_ref[...] = x
    y = x * lax.rsqrt(jnp.mean(x * x, axis=-1, keepdims=True) + EPS) * gain2_ref[...]
    h2 = (y * (1.0 + mod_ref[:, 4:5, :]) + mod_ref[:, 3:4, :]).reshape(tm, d)
    h2b = h2.astype(BF16)

    nt = (((1,), (1,)), ((), ()))
    logits = lax.dot_general(wrh_ref[...], h2b, nt, preferred_element_type=F32) \
        + lax.dot_general(wrl_ref[...], h2b, nt, preferred_element_type=F32)
    z = jnp.exp(logits - jnp.max(logits, axis=0, keepdims=True))
    probs = z / jnp.sum(z, axis=0, keepdims=True)
    sel = probs + br_ref[...]
    rows = [sel[e:e + 1, :] for e in range(N_EXPERTS)]
    prow = [probs[e:e + 1, :] for e in range(N_EXPERTS)]

    def beats(a, ia, b_, ib):
        return jnp.where(a >= b_, 1.0, 0.0) if ia < ib else jnp.where(a > b_, 1.0, 0.0)

    top = []
    gscore = []
    for g in range(N_GROUPS):
        ids = range(g * GROUP_SIZE, (g + 1) * GROUP_SIZE)
        sc = jnp.zeros_like(rows[0])
        for e in ids:
            cnt = jnp.zeros_like(rows[0])
            for o in ids:
                if o != e:
                    cnt = cnt + beats(rows[o], o, rows[e], e)
            flag = jnp.where(cnt < TOP_K, 1.0, 0.0)
            top.append(flag)
            sc = sc + flag * rows[e]
        gscore.append(sc)
    chosen = []
    for g in range(N_GROUPS):
        cnt = jnp.zeros_like(rows[0])
        for o in range(N_GROUPS):
            if o != g:
                cnt = cnt + beats(gscore[o], o, gscore[g], g)
        chosen.append(jnp.where(cnt < 1.0, 1.0, 0.0))
    wts = [prow[e] * top[e] * chosen[e // GROUP_SIZE] for e in range(N_EXPERTS)]
    den = wts[0]
    for e in range(1, N_EXPERTS):
        den = den + wts[e]
    g4 = []
    for k in range(GROUP_SIZE):
        gk = wts[k]
        for g in range(1, N_GROUPS):
            gk = gk + wts[g * GROUP_SIZE + k]
        g4.append(gk / den)

    rr = hs_ref.shape[1]
    ki = lax.broadcasted_iota(jnp.int32, (tm, tm), 0)
    ji = lax.broadcasted_iota(jnp.int32, (tm, tm), 1)
    before = jnp.where(ki < ji, 1.0, 0.0).astype(BF16)
    chosen4 = jnp.concatenate(chosen, axis=0)
    rank = jnp.dot(chosen4.astype(BF16), before, preferred_element_type=F32)
    lane = lax.broadcasted_iota(jnp.int32, (1, LANES), 1)
    start = jnp.zeros((1, 1), F32)
    dest = jnp.zeros((1, tm), F32)
    cnt_row = jnp.zeros((1, LANES), F32)
    for g in range(N_GROUPS):
        n_g = jnp.sum(chosen[g], axis=1, keepdims=True)
        tiles_g = jnp.floor((n_g + (MOE_TILE - 1)) * (1.0 / MOE_TILE))
        dest = dest + chosen[g] * (start + rank[g:g + 1, :])
        cnt_row = cnt_row + jnp.where(lane == g, tiles_g, 0.0)
        start = start + tiles_g * MOE_TILE
    dest_i = dest.astype(jnp.int32)
    dest_ref[0] = dest_i
    cnt_ref[0] = cnt_row.astype(jnp.int32)
    perm = jnp.where(lax.broadcasted_iota(jnp.int32, (rr, tm), 0) == dest_i, 1.0, 0.0).astype(BF16)
    hs_ref[0] = jnp.dot(perm, h2b, preferred_element_type=F32).astype(BF16)
    g128 = jnp.concatenate(g4 + [jnp.zeros((LANES - GROUP_SIZE, tm), F32)], axis=0)
    ghi = g128.astype(BF16)
    glo = (g128 - ghi.astype(F32)).astype(BF16)
    gs_ref[0] = (lax.dot_general(perm, ghi, nt, preferred_element_type=F32)
                 + lax.dot_general(perm, glo, nt, preferred_element_type=F32))


def outproj(x, hg_out, fox_o, mod, fox_gain, w_out, gain2, wr_hi, wr_lo, b_router, s, ls):
    b, l, d = x.shape
    nblk = (b // s) * (l // ls)
    nl = l // ls
    tm = s * ls
    row = lambda i, j: (i, j, 0)
    const = lambda i, j: (0, 0)
    blk = lambda i, j: (i * nl + j, 0, 0)
    return pl.pallas_call(
        _outproj_kernel,
        out_shape=[jax.ShapeDtypeStruct((b, l, d), F32),
                   jax.ShapeDtypeStruct((nblk, MOE_ROWS, d), BF16),
                   jax.ShapeDtypeStruct((nblk, MOE_ROWS, LANES), F32),
                   jax.ShapeDtypeStruct((nblk, 1, tm), jnp.int32),
                   jax.ShapeDtypeStruct((nblk, 1, LANES), jnp.int32)],
        grid=(b // s, nl),
        in_specs=[pl.BlockSpec((s, ls, d), row),
                  pl.BlockSpec((s, ls, HG_WIDTH), row),
                  pl.BlockSpec((s, ls, FOX_WIDTH), row),
                  pl.BlockSpec((s, N_MOD, d), lambda i, j: (i, 0, 0)),
                  pl.BlockSpec((1, FOX_WIDTH), const),
                  pl.BlockSpec(w_out.shape, const, pipeline_mode=pl.Buffered(1)),
                  pl.BlockSpec((1, d), const),
                  pl.BlockSpec((N_EXPERTS, d), const),
                  pl.BlockSpec((N_EXPERTS, d), const),
                  pl.BlockSpec((N_EXPERTS, 1), const)],
        out_specs=[pl.BlockSpec((s, ls, d), row),
                   pl.BlockSpec((1, MOE_ROWS, d), blk),
                   pl.BlockSpec((1, MOE_ROWS, LANES), blk),
                   pl.BlockSpec((1, 1, tm), blk),
                   pl.BlockSpec((1, 1, LANES), blk)],
        compiler_params=_params(("parallel", "parallel")),
        name="outproj",
    )(x, hg_out, fox_o, mod, fox_gain, w_out, gain2, wr_hi, wr_lo, b_router)


def _moe_tables(cnt, rows_per_step):
    nblk = cnt.shape[0]
    tiles_per_step = rows_per_step // MOE_TILE
    max_tiles = nblk * (BLOCK_TOKENS // MOE_TILE + N_GROUPS)
    n_steps = -(-max_tiles // tiles_per_step) + N_GROUPS
    npair = N_GROUPS * nblk
    start_gm = (jnp.cumsum(cnt, axis=1) - cnt).T.reshape(-1)
    cnt_gm = cnt.T.reshape(-1)
    csum = jnp.cumsum(cnt_gm)
    tau = jnp.arange(max_tiles, dtype=jnp.int32)
    pair = jnp.minimum(jnp.sum((csum[None, :] <= tau[:, None]).astype(jnp.int32), axis=1), npair - 1)
    hot = (pair[:, None] == jnp.arange(npair, dtype=jnp.int32)[None, :]).astype(jnp.int32)
    k = tau - jnp.sum(hot * (csum - cnt_gm)[None, :], axis=1)
    valid = tau < csum[-1]
    tile_blk = jnp.where(valid, pair % nblk, 0).astype(jnp.int32)
    tile_row = jnp.where(valid, (jnp.sum(hot * start_gm[None, :], axis=1) + k) * MOE_TILE, 0).astype(jnp.int32)
    tot = jnp.sum(cnt, axis=0)
    gstart = jnp.cumsum(tot) - tot
    nsteps = (tot + tiles_per_step - 1) // tiles_per_step
    send = jnp.cumsum(nsteps)
    s = jnp.arange(n_steps, dtype=jnp.int32)
    sg = jnp.minimum(jnp.sum((send[None, :] <= s[:, None]).astype(jnp.int32), axis=1), N_GROUPS - 1)
    ghot = (sg[:, None] == jnp.arange(N_GROUPS, dtype=jnp.int32)[None, :]).astype(jnp.int32)
    pick = lambda v: jnp.sum(ghot * v[None, :], axis=1)
    first = pick(gstart) + (s - pick(send - nsteps)) * tiles_per_step
    num = jnp.clip(pick(gstart + tot) - first, 0, tiles_per_step)
    first = jnp.where(num > 0, first, 0)
    return sg.astype(jnp.int32), first.astype(jnp.int32), num.astype(jnp.int32), tile_blk, tile_row


def _moe_kernel(sg_ref, first_ref, num_ref, tblk_ref, trow_ref,
                hs_hbm, gs_hbm, wg_ref, wu_ref, wd_ref, yinit_hbm, ys_hbm,
                hbuf, gbuf, ybuf, sem_in, sem_out):
    del sg_ref, yinit_hbm
    s = pl.program_id(0)
    ns = pl.num_programs(0)
    slot = lax.rem(s, 2)

    def tile(k):
        return pl.ds(pl.multiple_of(k * MOE_TILE, MOE_TILE), MOE_TILE)

    def src(step, k):
        t = first_ref[step] + k
        return tblk_ref[t], pl.ds(pl.multiple_of(trow_ref[t], MOE_TILE), MOE_TILE)

    def h_copy(step, sl, k):
        blk, rows = src(step, k)
        return pltpu.make_async_copy(hs_hbm.at[blk, rows, :], hbuf.at[sl, tile(k), :], sem_in.at[sl, 0])

    def g_copy(step, sl, k):
        blk, rows = src(step, k)
        return pltpu.make_async_copy(gs_hbm.at[blk, rows, :], gbuf.at[sl, tile(k), :], sem_in.at[sl, 1])

    def y_copy(step, sl, k):
        blk, rows = src(step, k)
        return pltpu.make_async_copy(ybuf.at[sl, tile(k), :], ys_hbm.at[blk, rows, :], sem_out.at[sl])

    def each_tile(step, fn):
        def body(k, c):
            fn(k)
            return c
        lax.fori_loop(0, num_ref[step], body, 0)

    def start_gather(step, sl):
        def fn(k):
            h_copy(step, sl, k).start()
            g_copy(step, sl, k).start()
        each_tile(step, fn)

    def wait_gather(step, sl):
        def fn(k):
            h_copy(step, sl, k).wait()
            g_copy(step, sl, k).wait()
        each_tile(step, fn)

    @pl.when(s == 0)
    def _():
        hbuf[...] = jnp.zeros_like(hbuf)
        gbuf[...] = jnp.zeros_like(gbuf)
        start_gather(0, 0)

    wait_gather(s, slot)

    @pl.when(s + 1 < ns)
    def _():
        start_gather(s + 1, 1 - slot)

    @pl.when(s >= 2)
    def _():
        each_tile(s - 2, lambda k: y_copy(s - 2, slot, k).wait())

    @pl.when(num_ref[s] > 0)
    def _():
        h = hbuf[slot]
        g = gbuf[slot]
        acc = jnp.zeros((h.shape[0], wd_ref.shape[2]), F32)
        for k in range(GROUP_SIZE):
            a = jnp.dot(h, wg_ref[k].astype(BF16), preferred_element_type=F32)
            u = jnp.dot(h, wu_ref[k].astype(BF16), preferred_element_type=F32)
            he = (_silu(a) * u * g[:, k:k + 1]).astype(BF16)
            acc = acc + jnp.dot(he, wd_ref[k].astype(BF16), preferred_element_type=F32)
        ybuf[slot] = acc.astype(ybuf.dtype)

    each_tile(s, lambda k: y_copy(s, slot, k).start())

    @pl.when(s == ns - 1)
    def _():
        each_tile(s, lambda k: y_copy(s, slot, k).wait())

        @pl.when(s >= 1)
        def _():
            each_tile(s - 1, lambda k: y_copy(s - 1, 1 - slot, k).wait())


def moe_sparse(hs, gs, cnt, wg, wu, wd, layer):
    nblk, rr, d = hs.shape
    de = wg.shape[2]
    rows = MOE_STEP_ROWS
    sg, first, num, tile_blk, tile_row = _moe_tables(cnt, rows)
    n_steps = sg.shape[0]
    wmap = lambda s, sg, *_: (layer * N_GROUPS + sg[s], 0, 0)
    grid_spec = pltpu.PrefetchScalarGridSpec(
        num_scalar_prefetch=5,
        grid=(n_steps,),
        in_specs=[pl.BlockSpec(memory_space=pl.ANY),
                  pl.BlockSpec(memory_space=pl.ANY),
                  pl.BlockSpec((GROUP_SIZE, d, de), wmap, pipeline_mode=pl.Buffered(1)),
                  pl.BlockSpec((GROUP_SIZE, d, de), wmap, pipeline_mode=pl.Buffered(1)),
                  pl.BlockSpec((GROUP_SIZE, de, d), wmap, pipeline_mode=pl.Buffered(1)),
                  pl.BlockSpec(memory_space=pl.ANY)],
        out_specs=pl.BlockSpec(memory_space=pl.ANY),
        scratch_shapes=[pltpu.VMEM((2, rows, d), BF16),
                        pltpu.VMEM((2, rows, LANES), F32),
                        pltpu.VMEM((2, rows, d), BF16),
                        pltpu.SemaphoreType.DMA((2, 2)),
                        pltpu.SemaphoreType.DMA((2,))])
    return pl.pallas_call(
        _moe_kernel,
        out_shape=jax.ShapeDtypeStruct((nblk, rr, d), BF16),
        grid_spec=grid_spec,
        input_output_aliases={10: 0},
        compiler_params=_params(("arbitrary",)),
        name="moe",
    )(sg, first, num, tile_blk, tile_row, hs, gs, wg, wu, wd, jnp.zeros((nblk, rr, d), BF16))


def _combine_kernel(x_ref, ys_ref, dest_ref, mod_ref, fgain_ref, o_ref, *, final):
    s, ls, d = x_ref.shape
    tm = s * ls
    rr = ys_ref.shape[1]
    dcol = jnp.broadcast_to(dest_ref[0].astype(F32), (LANES, tm)).T
    lane = lax.broadcasted_iota(jnp.int32, (tm, LANES), 1).astype(F32)
    unperm = jnp.concatenate(
        [jnp.where(dcol == lane + float(c * LANES), 1.0, 0.0).astype(BF16) for c in range(rr // LANES)],
        axis=1)
    y = jnp.dot(unperm, ys_ref[0], preferred_element_type=F32)
    x = x_ref[...] + mod_ref[:, 5:6, :] * y.reshape(s, ls, d)
    if final:
        x = x * lax.rsqrt(jnp.mean(x * x, axis=-1, keepdims=True) + EPS) * fgain_ref[...]
    o_ref[...] = x


def combine(x, ys, dest, mod, final_gain, s, ls, final):
    b, l, d = x.shape
    nl = l // ls
    tm = s * ls
    row = lambda i, j: (i, j, 0)
    blk = lambda i, j: (i * nl + j, 0, 0)
    return pl.pallas_call(
        functools.partial(_combine_kernel, final=final),
        out_shape=jax.ShapeDtypeStruct((b, l, d), F32),
        grid=(b // s, nl),
        in_specs=[pl.BlockSpec((s, ls, d), row),
                  pl.BlockSpec((1, ys.shape[1], d), blk),
                  pl.BlockSpec((1, 1, tm), blk),
                  pl.BlockSpec((s, N_MOD, d), lambda i, j: (i, 0, 0)),
                  pl.BlockSpec((1, d), lambda i, j: (0, 0))],
        out_specs=pl.BlockSpec((s, ls, d), row),
        compiler_params=_params(("parallel", "parallel")),
        name="combine",
    )(x, ys, dest, mod, final_gain)


def _tile(b, l):
    ls = min(l, BLOCK_TOKENS)
    s = BLOCK_TOKENS // ls
    assert s * ls == BLOCK_TOKENS and b % s == 0 and l % ls == 0
    assert BLOCK_TOKENS + N_GROUPS * MOE_TILE <= MOE_ROWS
    return s, ls


def _layer(x, mod, l, p, hg_state0, fox_past, kv_prev, final):
    depth = p['norm_mix_gain'].shape[0]
    b, seq, d = x.shape
    s, ls = _tile(b, seq)
    sample = fox_past is not None
    hg_in, fq, kbuf, vbuf, *k16v16, ff = inproj(x, mod, p['norm_mix_gain'][l], p['w_in'][l], s, ls,
                                                l, depth, kv_prev, sample)

    hg_out, hg_state = hgrn(hg_in, hg_state0, p['hg_lower_bounds'], p['hg_norm_gain'][l], l)

    if not sample:
        tq = min(seq, BLOCK_TOKENS)
        qp, kp, vt, lf_pad = foxpack(fq, kbuf, vbuf, l, ff, p['fox_f_bias_pad'][l], tq)
        fox_o = fox_prompt(qp, kp, vt, tq)
        logf = lf_pad[:, :, :FOX_HEADS]
    else:
        kcache, vcache, plogf = fox_past
        past = kcache.shape[4]
        fft = jnp.swapaxes(ff[:, :, :FOX_HEADS], 1, 2)
        logft, _ = forget_cumsum(fft, p['fox_f_bias'][l], True, seq)
        tot = past + seq
        pad = (-tot) % LANES
        allt = jnp.concatenate([jnp.swapaxes(plogf[l], 1, 2), logft,
                                jnp.zeros((b, FOX_HEADS, pad), F32)], axis=2)
        _, ct = forget_cumsum(allt, p['fox_f_bias'][l], False, LANES)
        ct = ct * LOG2E
        cq = jnp.swapaxes(ct[:, :, past:tot], 1, 2)
        fox_o = fox_sample(fq, kcache, vcache, l, k16v16[0], k16v16[1], cq, ct)
        logf = jnp.swapaxes(logft, 1, 2)

    x_mid, hs, gs, dest, cnt = outproj(x, hg_out, fox_o, mod, p['fox_out_gain'][l], p['w_out'][l],
                                       p['norm_ffn_gain'][l], p['wr_hi'], p['wr_lo'], p['b_router'], s, ls)
    ys = moe_sparse(hs, gs, cnt[:, 0, :N_GROUPS], p['w_exp_gate'], p['w_exp_up'], p['w_exp_down'], l)
    x_new = combine(x_mid, ys, dest, mod, p['final_norm_gain'], s, ls, final)
    return x_new, (kbuf, vbuf), hg_state, logf


def kernel(x_prompt, x_sample, cache_fox_k, cache_fox_v, cache_fox_logf, state_hgrn, c_prompt, c_sample,
           norm_mix_gain, norm_ffn_gain, w_ada, b_ada, w_in, hg_lower_bounds, hg_norm_gain,
           fox_f_bias, fox_out_gain, w_out, w_router, b_router, w_exp_gate, w_exp_up, w_exp_down,
           final_norm_gain):
    depth, d = norm_mix_gain.shape
    bp = x_prompt.shape[0]
    n_in = w_in.shape[2]
    n_pad = 4 * HG_WIDTH + 3 * FOX_WIDTH + LANES - n_in
    wr_t = w_router.T
    wr_hi = wr_t.astype(BF16)
    p = {
        'norm_mix_gain': norm_mix_gain.reshape(depth, 1, d),
        'norm_ffn_gain': norm_ffn_gain.reshape(depth, 1, d),
        'w_in': jnp.pad(w_in, ((0, 0), (0, 0), (0, n_pad))).astype(BF16),
        'hg_lower_bounds': hg_lower_bounds,
        'hg_norm_gain': hg_norm_gain.reshape(depth, 1, HG_WIDTH),
        'fox_f_bias': fox_f_bias.reshape(depth, FOX_HEADS, 1),
        'fox_f_bias_pad': jnp.pad(fox_f_bias, ((0, 0), (0, LANES - FOX_HEADS))).reshape(depth, 1, LANES),
        'fox_out_gain': fox_out_gain.reshape(depth, 1, FOX_WIDTH),
        'w_out': w_out.astype(BF16),
        'wr_hi': wr_hi,
        'wr_lo': (wr_t - wr_hi.astype(F32)).astype(BF16),
        'b_router': b_router.reshape(N_EXPERTS, 1),
        'w_exp_gate': w_exp_gate.reshape((depth * N_EXPERTS,) + w_exp_gate.shape[2:]),
        'w_exp_up': w_exp_up.reshape((depth * N_EXPERTS,) + w_exp_up.shape[2:]),
        'w_exp_down': w_exp_down.reshape((depth * N_EXPERTS,) + w_exp_down.shape[2:]),
        'final_norm_gain': final_norm_gain.reshape(1, d),
    }
    mods = ada_mod(jnp.concatenate([c_prompt, c_sample], axis=0), w_ada, b_ada)
    mods = mods.reshape(depth, -1, N_MOD, d)
    bs, lsq = x_sample.shape[:2]
    lp = x_prompt.shape[1]
    past = cache_fox_k.shape[2]
    fox_past = (jnp.transpose(cache_fox_k, (0, 1, 3, 4, 2)), jnp.transpose(cache_fox_v, (0, 1, 3, 4, 2)),
                cache_fox_logf)

    xp, xs = x_prompt, x_sample
    kv_p = kv_s = None
    st_p, st_s, lf_p, lf_s = [], [], [], []
    zero_state = jnp.zeros((bp, HG_HEADS, HG_DIM, HG_DIM), F32)
    for l in range(depth):
        final = l == depth - 1
        xp, kv_p, st, lf = _layer(xp, mods[l, :bp], l, p, zero_state, None, kv_p, final)
        st_p.append(st)
        lf_p.append(lf)
        xs, kv_s, st, lf = _layer(xs, mods[l, bp:], l, p, state_hgrn[l], fox_past, kv_s, final)
        st_s.append(st)
        lf_s.append(lf)
    heads = lambda a, b, l: a.reshape(depth, b, l, FOX_HEADS, FOX_DIM)
    return (xp, xs,
            jnp.stack(st_p), heads(kv_p[0], bp, lp), heads(kv_p[1], bp, lp), jnp.stack(lf_p),
            jnp.stack(st_s), heads(kv_s[0], bs, lsq), heads(kv_s[1], bs, lsq), jnp.stack(lf_s))
```

```python
import functools
import math

import numpy as np
import jax
import jax.numpy as jnp
from jax import lax
from jax.experimental import pallas as pl
from jax.experimental.pallas import tpu as pltpu

HG_HEADS = 4
HG_DIM = 128
HG_WIDTH = HG_HEADS * HG_DIM
FOX_HEADS = 8
FOX_DIM = 64
FOX_WIDTH = FOX_HEADS * FOX_DIM
N_EXPERTS = 16
N_GROUPS = 4
GROUP_SIZE = N_EXPERTS // N_GROUPS
TOP_K = 2
N_MOD = 6
EPS = 1e-6

LANES = 128
HG_CHUNK = 128
HG_SUB = 16
HG_STREAMS = 2
MOE_TILE = 16
BLOCK_TOKENS = 512
MOE_ROWS = 640
MOE_STEP_ROWS = 1024
VMEM_LIMIT = 56 * 1024 * 1024

F32 = jnp.float32
BF16 = jnp.bfloat16
NEG = -1e30
LOG2E = math.log2(math.e)


def _params(sem, vmem=VMEM_LIMIT):
    return pltpu.CompilerParams(dimension_semantics=sem, vmem_limit_bytes=vmem)


def _split3(x):
    hi = x.astype(BF16)
    r1 = x - hi.astype(F32)
    mid = r1.astype(BF16)
    lo = (r1 - mid.astype(F32)).astype(BF16)
    return hi, mid, lo


def _sigmoid(x):
    return 1.0 / (1.0 + jnp.exp(-x))


def _log_sigmoid(x):
    return jnp.minimum(x, 0.0) - jnp.log(1.0 + jnp.exp(-jnp.abs(x)))


def _silu(x):
    return x * _sigmoid(x)


def _ada_kernel(c_ref, w_ref, b_ref, o_ref):
    s = _silu(c_ref[...]).astype(BF16)
    o_ref[0] = jnp.dot(s, w_ref[0].astype(BF16), preferred_element_type=F32) + b_ref[0]


def ada_mod(c, w_ada, b_ada):
    depth, d, n = w_ada.shape
    nb = c.shape[0]
    tn = 1536
    return pl.pallas_call(
        _ada_kernel,
        out_shape=jax.ShapeDtypeStruct((depth, nb, n), F32),
        grid=(depth, n // tn),
        in_specs=[pl.BlockSpec((nb, d), lambda l, j: (0, 0)),
                  pl.BlockSpec((1, d, tn), lambda l, j: (l, 0, j)),
                  pl.BlockSpec((1, 1, tn), lambda l, j: (l, 0, j))],
        out_specs=pl.BlockSpec((1, nb, tn), lambda l, j: (l, 0, j)),
        compiler_params=_params(("arbitrary", "arbitrary")),
        name="ada_mod",
    )(c, w_ada, b_ada.reshape(depth, 1, n))


def _inproj_kernel(x_ref, mod_ref, gain_ref, w_ref, *refs, has_prev, emit16):
    outs = refs[2:] if has_prev else refs
    hg_ref, fq_ref, fk32_ref, fv32_ref = outs[:4]
    ff_ref = outs[-1]
    s, ls, d = x_ref.shape
    x = x_ref[...]
    y = x * lax.rsqrt(jnp.mean(x * x, axis=-1, keepdims=True) + EPS) * gain_ref[...]
    h = y * (1.0 + mod_ref[:, 1:2, :]) + mod_ref[:, 0:1, :]
    hb = h.reshape(s * ls, d).astype(BF16)

    def proj(lo, hi):
        return jnp.dot(hb, w_ref[:, lo:hi], preferred_element_type=F32)

    c0 = 4 * HG_WIDTH
    hg_ref[...] = proj(0, c0).reshape(s, ls, c0)
    fq = proj(c0, c0 + FOX_WIDTH) * (FOX_DIM ** -0.5 * LOG2E)
    fq_ref[...] = fq.reshape(s, ls, FOX_WIDTH).astype(BF16)
    fk = proj(c0 + FOX_WIDTH, c0 + 2 * FOX_WIDTH).reshape(s, ls, FOX_WIDTH)
    fv = proj(c0 + 2 * FOX_WIDTH, c0 + 3 * FOX_WIDTH).reshape(s, ls, FOX_WIDTH)
    for slot in range(fk32_ref.shape[0]):
        fk32_ref[slot] = fk
        fv32_ref[slot] = fv
    if emit16:
        outs[4][...] = fk.astype(BF16)
        outs[5][...] = fv.astype(BF16)
    ff_ref[...] = proj(c0 + 3 * FOX_WIDTH, c0 + 3 * FOX_WIDTH + LANES).reshape(s, ls, LANES)


def inproj(x, mod, gain, w_pad, s, ls, layer, depth, kv_prev, emit16):
    b, l, d = x.shape
    n = w_pad.shape[1]
    row = lambda i, j: (i, j, 0)
    lrow = lambda i, j: (layer, i, j, 0)
    shapes = [jax.ShapeDtypeStruct((b, l, 4 * HG_WIDTH), F32),
              jax.ShapeDtypeStruct((b, l, FOX_WIDTH), BF16),
              jax.ShapeDtypeStruct((depth, b, l, FOX_WIDTH), F32),
              jax.ShapeDtypeStruct((depth, b, l, FOX_WIDTH), F32)]
    slots = depth if kv_prev is None else 1
    if kv_prev is None:
        assert layer == 0
    specs = [pl.BlockSpec((s, ls, 4 * HG_WIDTH), row),
             pl.BlockSpec((s, ls, FOX_WIDTH), row),
             pl.BlockSpec((slots, s, ls, FOX_WIDTH), lrow),
             pl.BlockSpec((slots, s, ls, FOX_WIDTH), lrow)]
    if emit16:
        shapes += [jax.ShapeDtypeStruct((b, l, FOX_WIDTH), BF16)] * 2
        specs += [pl.BlockSpec((s, ls, FOX_WIDTH), row)] * 2
    shapes.append(jax.ShapeDtypeStruct((b, l, LANES), F32))
    specs.append(pl.BlockSpec((s, ls, LANES), row))
    in_specs = [pl.BlockSpec((s, ls, d), row),
                pl.BlockSpec((s, N_MOD, d), lambda i, j: (i, 0, 0)),
                pl.BlockSpec((1, d), lambda i, j: (0, 0)),
                pl.BlockSpec((d, n), lambda i, j: (0, 0), pipeline_mode=pl.Buffered(1))]
    args = [x, mod, gain, w_pad]
    aliases = {}
    if kv_prev is not None:
        in_specs += [pl.BlockSpec(memory_space=pl.ANY)] * 2
        args += list(kv_prev)
        aliases = {4: 2, 5: 3}
    return pl.pallas_call(
        functools.partial(_inproj_kernel, has_prev=kv_prev is not None, emit16=emit16),
        out_shape=shapes,
        grid=(b // s, l // ls),
        in_specs=in_specs,
        out_specs=specs,
        input_output_aliases=aliases,
        compiler_params=_params(("parallel", "parallel")),
        name="inproj",
    )(*args)


def _cumsum_kernel(x_ref, bias_ref, lf_ref, c_ref, carry_ref, *, apply_ls):
    @pl.when(pl.program_id(1) == 0)
    def _():
        carry_ref[...] = jnp.zeros_like(carry_ref)

    bb, nh, tc = x_ref.shape
    x = x_ref[...].reshape(bb * nh, tc)
    lf = _log_sigmoid(x + bias_ref[...]) if apply_ls else x
    lf_ref[...] = lf.reshape(bb, nh, tc)
    r = lax.broadcasted_iota(jnp.int32, (tc, tc), 0)
    c = lax.broadcasted_iota(jnp.int32, (tc, tc), 1)
    tri = jnp.where(r <= c, 1.0, 0.0).astype(BF16)
    hi, mid, lo = _split3(lf)
    tot = (jnp.dot(hi, tri, preferred_element_type=F32)
           + jnp.dot(mid, tri, preferred_element_type=F32)
           + jnp.dot(lo, tri, preferred_element_type=F32)) + carry_ref[:, 0:1]
    c_ref[...] = tot.reshape(bb, nh, tc)
    carry_ref[...] = jnp.broadcast_to(tot[:, tc - 1:tc], carry_ref.shape)


def forget_cumsum(xt, bias, apply_ls, tc):
    b, h, l = xt.shape
    blk = pl.BlockSpec((b, h, tc), lambda i, j: (0, 0, j))
    return pl.pallas_call(
        functools.partial(_cumsum_kernel, apply_ls=apply_ls),
        out_shape=[jax.ShapeDtypeStruct((b, h, l), F32)] * 2,
        grid=(1, l // tc),
        in_specs=[blk, pl.BlockSpec((b * h, 1), lambda i, j: (0, 0))],
        out_specs=[blk, blk],
        scratch_shapes=[pltpu.VMEM((b * h, LANES), F32)],
        compiler_params=_params(("arbitrary", "arbitrary")),
        name="forget_cumsum",
    )(xt, jnp.tile(bias, (b, 1)))


HG_HALVES = (8, 4, 2, 1)


def _hgrn_sum_matrix():
    c = HG_CHUNK
    t = np.arange(c)[:, None]
    u = np.arange(c)[None, :]
    mats = [(u <= t)]
    for w in HG_HALVES:
        pos = t % (2 * w)
        mid = t - pos + w - 1
        upper = pos >= w
        mats.append(np.where(upper, (u > mid) & (u <= t), (u > t) & (u <= mid)))
    return jnp.asarray(np.concatenate(mats, axis=0).astype(np.float32), BF16)


def _hgrn_kernel(hin_ref, s0_ref, lbp_ref, gain_ref, sums_ref, out_ref, sfin_ref, st_ref, *, layer):
    t = pl.program_id(1)
    nt = pl.num_programs(1)
    c = HG_CHUNK
    nsub = c // HG_SUB
    nb = hin_ref.shape[0]
    rows_in = hin_ref.shape[1]

    def pad_rows(a):
        if rows_in == c:
            return a
        return jnp.concatenate([a, jnp.zeros((c - rows_in, a.shape[1]), a.dtype)], axis=0)

    @pl.when(t == 0)
    def _():
        for bi in range(nb):
            for h in range(HG_HEADS):
                st_ref[bi, h] = s0_ref[bi, h].T

    lbp = lbp_ref[...]
    e = jnp.exp(lbp - jnp.max(lbp, axis=0, keepdims=True))
    p = e / jnp.sum(e, axis=0, keepdims=True)
    acc = p[0:1]
    first = acc
    for i in range(1, layer + 1):
        acc = acc + p[i:i + 1]
    lb_all = acc - first

    ri = lax.broadcasted_iota(jnp.int32, (c, c), 0)
    ci = lax.broadcasted_iota(jnp.int32, (c, c), 1)
    ones = jnp.ones((HG_DIM, HG_DIM), BF16)
    nt_dims = (((1,), (1,)), ((), ()))

    streams = [(bi, h) for bi in range(nb) for h in range(HG_HEADS)]
    q, kk, v, b, sums, att = {}, {}, {}, {}, {}, {}
    for bi, h in streams:
        lb = lb_all[:, h * HG_DIM:(h + 1) * HG_DIM]
        hq = hin_ref[bi, :, h * HG_DIM:(h + 1) * HG_DIM]
        hf = hin_ref[bi, :, HG_WIDTH + h * HG_DIM:HG_WIDTH + (h + 1) * HG_DIM]
        hi = hin_ref[bi, :, 2 * HG_WIDTH + h * HG_DIM:2 * HG_WIDTH + (h + 1) * HG_DIM]
        q[bi, h] = pad_rows(_silu(hq) * (HG_DIM ** -0.5))
        a1 = jnp.log(lb)
        e = jnp.exp(-jnp.abs(hf))
        den = 1.0 + e
        a2 = jnp.log1p(-lb) + (jnp.minimum(hf, 0.0) - jnp.log(den))
        mx = jnp.maximum(a1, a2)
        lf = pad_rows(mx + jnp.log(jnp.exp(a1 - mx) + jnp.exp(a2 - mx)))
        kk[bi, h] = pad_rows((1.0 - lb) * (jnp.where(hf > 0.0, e, 1.0) / den))
        v[bi, h] = pad_rows(hi)
        lhi, lmid, llo = _split3(lf)
        sums[bi, h] = (jnp.dot(sums_ref[...], lhi, preferred_element_type=F32)
                       + jnp.dot(sums_ref[...], lmid, preferred_element_type=F32)
                       + jnp.dot(sums_ref[...], llo, preferred_element_type=F32))
        b[bi, h] = sums[bi, h][0:c]

    for key in streams:
        att[key] = jnp.where(ri == ci, jnp.dot((q[key] * kk[key]).astype(BF16), ones,
                                               preferred_element_type=F32), 0.0)
    for lvl, w in enumerate(HG_HALVES):
        shift = (2 * w).bit_length() - 1
        pair = ((ri >> shift) == (ci >> shift)) & ((ri & (2 * w - 1)) >= w) & ((ci & (2 * w - 1)) < w)
        for key in streams:
            dec = jnp.exp(sums[key][(lvl + 1) * c:(lvl + 2) * c])
            sc = lax.dot_general((q[key] * dec).astype(BF16), (kk[key] * dec).astype(BF16), nt_dims,
                                 preferred_element_type=F32)
            att[key] = att[key] + jnp.where(pair, sc, 0.0)

    khat = {key: None for key in streams}
    r_prev = {key: jnp.zeros((1, HG_DIM), F32) for key in streams}
    qhat_blocks = {key: [] for key in streams}
    cross = {key: [] for key in streams}
    for i in range(nsub):
        r0 = i * HG_SUB
        for key in streams:
            bb = b[key][r0:r0 + HG_SUB]
            r_next = b[key][r0 + HG_SUB - 1:r0 + HG_SUB]
            qt = q[key][r0:r0 + HG_SUB] * jnp.exp(bb - r_prev[key])
            if khat[key] is None:
                cross[key].append(jnp.zeros((HG_SUB, c), F32))
            else:
                kfull = jnp.concatenate([khat[key], jnp.zeros((c - r0, HG_DIM), F32)], axis=0)
                cross[key].append(lax.dot_general(qt.astype(BF16), kfull.astype(BF16), nt_dims,
                                                  preferred_element_type=F32))
            qhat_blocks[key].append(qt * jnp.exp(r_prev[key]))
            kt = kk[key][r0:r0 + HG_SUB] * jnp.exp(r_next - bb)
            if khat[key] is None:
                khat[key] = kt
            else:
                khat[key] = jnp.concatenate([khat[key] * jnp.exp(r_next - r_prev[key]), kt], axis=0)
            r_prev[key] = r_next

    for bi, h in streams:
        key = (bi, h)
        cols = slice(h * HG_DIM, (h + 1) * HG_DIM)
        st = st_ref[bi, h]
        a = att[key] + jnp.concatenate(cross[key], axis=0)
        qhat = jnp.concatenate(qhat_blocks[key], axis=0).astype(BF16)
        o = jnp.dot(a.astype(BF16), v[key].astype(BF16), preferred_element_type=F32)
        o = o + lax.dot_general(qhat, st.astype(BF16), nt_dims, preferred_element_type=F32)
        st_ref[bi, h] = st * jnp.exp(r_prev[key]) + jnp.dot(
            v[key].T.astype(BF16), khat[key].astype(BF16), preferred_element_type=F32)

        hg = hin_ref[bi, :, 3 * HG_WIDTH + h * HG_DIM:3 * HG_WIDTH + (h + 1) * HG_DIM]
        o = o[:rows_in]
        o = o * lax.rsqrt(jnp.mean(o * o, axis=-1, keepdims=True) + EPS)
        o = o * gain_ref[:, cols] * _silu(hg)
        out_ref[bi, :, cols] = o.astype(out_ref.dtype)

    @pl.when(t == nt - 1)
    def _():
        for bi, h in streams:
            sfin_ref[bi, h] = st_ref[bi, h].T


def hgrn(hin, s0, lbp, gain, layer):
    b, l, w = hin.shape
    c = min(HG_CHUNK, l)
    assert l % c == 0 and c % HG_SUB == 0
    nt = l // c
    nb = HG_STREAMS
    assert b % nb == 0
    sums = _hgrn_sum_matrix()
    return pl.pallas_call(
        functools.partial(_hgrn_kernel, layer=layer),
        out_shape=[jax.ShapeDtypeStruct((b, l, HG_WIDTH), BF16),
                   jax.ShapeDtypeStruct((b, HG_HEADS, HG_DIM, HG_DIM), F32)],
        grid=(b // nb, nt),
        in_specs=[pl.BlockSpec((nb, c, w), lambda i, j: (i, j, 0)),
                  pl.BlockSpec((nb, HG_HEADS, HG_DIM, HG_DIM), lambda i, j: (i, 0, 0, 0)),
                  pl.BlockSpec(lbp.shape, lambda i, j: (0, 0)),
                  pl.BlockSpec((1, HG_WIDTH), lambda i, j: (0, 0)),
                  pl.BlockSpec(sums.shape, lambda i, j: (0, 0))],
        out_specs=[pl.BlockSpec((nb, c, HG_WIDTH), lambda i, j: (i, j, 0)),
                   pl.BlockSpec((nb, HG_HEADS, HG_DIM, HG_DIM), lambda i, j: (i, 0, 0, 0))],
        scratch_shapes=[pltpu.VMEM((nb, HG_HEADS, HG_DIM, HG_DIM), F32)],
        compiler_params=_params(("parallel", "arbitrary")),
        name="hgrn",
    )(hin, s0, lbp, gain, sums)


AUG_ONE = FOX_DIM
AUG_NEG = FOX_DIM + 3
ONE_LANE = FOX_HEADS
FOX_GROUP = 4
FOX_Q_TILE = 512
FOX_VROWS = FOX_DIM + 16


def _placement():
    pk = np.zeros((3 * LANES, FOX_HEADS * LANES), np.float32)
    pq = np.zeros((3 * LANES, FOX_HEADS * LANES), np.float32)
    for h in range(FOX_HEADS):
        for part in range(3):
            pk[part * LANES + h, h * LANES + AUG_NEG + part] = -1.0
            pq[part * LANES + h, h * LANES + AUG_ONE + part] = 1.0
            pk[ONE_LANE, h * LANES + AUG_ONE + part] = 1.0
            pq[ONE_LANE, h * LANES + AUG_NEG + part] = 1.0
    return jnp.asarray(pk, BF16), jnp.asarray(pq, BF16)


def _foxpack_kernel(q_ref, k_ref, v_ref, ff_ref, bias_ref, pk_ref, pq_ref,
                    qp_ref, kp_ref, vt_ref, lf_ref, carry_ref):
    @pl.when(pl.program_id(1) == 0)
    def _():
        carry_ref[...] = jnp.zeros_like(carry_ref)

    tm = q_ref.shape[1]
    lane = lax.broadcasted_iota(jnp.int32, (tm, LANES), 1)
    lf = jnp.where(lane < FOX_HEADS, _log_sigmoid(ff_ref[0] + bias_ref[...]), 0.0)
    lf_ref[0] = lf
    ri = lax.broadcasted_iota(jnp.int32, (tm, tm), 0)
    ci = lax.broadcasted_iota(jnp.int32, (tm, tm), 1)
    tril = jnp.where(ci <= ri, 1.0, 0.0).astype(BF16)
    hi, mid, lo = _split3(lf)
    c = (jnp.dot(tril, hi, preferred_element_type=F32)
         + jnp.dot(tril, mid, preferred_element_type=F32)
         + jnp.dot(tril, lo, preferred_element_type=F32)) + carry_ref[0:1, :]
    carry_ref[...] = jnp.broadcast_to(c[tm - 1:tm, :], carry_ref.shape)

    chi, cmid, clo = _split3(c * LOG2E)
    chi = jnp.where(lane == ONE_LANE, 1.0, chi.astype(F32)).astype(BF16)
    cterms = jnp.concatenate([chi, cmid, clo], axis=1)
    augk = jnp.dot(cterms, pk_ref[...], preferred_element_type=F32)
    augq = jnp.dot(cterms, pq_ref[...], preferred_element_type=F32)
    for h in range(FOX_HEADS):
        blk = slice((h // 2) * LANES, (h // 2 + 1) * LANES)
        hcols = slice(h * LANES, (h + 1) * LANES)
        kh = k_ref[0, 0, :, blk]
        qh = q_ref[0, :, blk].astype(F32)
        if h % 2:
            kh = pltpu.roll(kh, FOX_DIM, axis=1)
            qh = pltpu.roll(qh, FOX_DIM, axis=1)
        kp_ref[0, h] = jnp.where(lane < FOX_DIM, kh, augk[:, hcols]).astype(BF16)
        qp_ref[0, h] = jnp.where(lane < FOX_DIM, qh, augq[:, hcols]).astype(BF16)
    vt = v_ref[0, 0].T
    tail = jnp.where(lax.broadcasted_iota(jnp.int32, (FOX_VROWS - FOX_DIM, tm), 0) == 0, 1.0, 0.0)
    for h in range(FOX_HEADS):
        vt_ref[0, h] = jnp.concatenate([vt[h * FOX_DIM:(h + 1) * FOX_DIM], tail], axis=0).astype(BF16)


def foxpack(fq, kbuf, vbuf, layer, ff, bias_pad, tm):
    b, l, w = fq.shape
    pk, pq = _placement()
    row = lambda i, j: (i, j, 0)
    lrow = lambda i, j: (layer, i, j, 0)
    const = lambda i, j: (0, 0)
    head_blk = pl.BlockSpec((1, FOX_HEADS, tm, LANES), lambda i, j: (i, 0, j, 0))
    return pl.pallas_call(
        _foxpack_kernel,
        out_shape=[jax.ShapeDtypeStruct((b, FOX_HEADS, l, LANES), BF16),
                   jax.ShapeDtypeStruct((b, FOX_HEADS, l, LANES), BF16),
                   jax.ShapeDtypeStruct((b, FOX_HEADS, FOX_VROWS, l), BF16),
                   jax.ShapeDtypeStruct((b, l, LANES), F32)],
        grid=(b, l // tm),
        in_specs=[pl.BlockSpec((1, tm, w), row),
                  pl.BlockSpec((1, 1, tm, w), lrow),
                  pl.BlockSpec((1, 1, tm, w), lrow),
                  pl.BlockSpec((1, tm, LANES), row),
                  pl.BlockSpec((1, LANES), const),
                  pl.BlockSpec(pk.shape, const),
                  pl.BlockSpec(pq.shape, const)],
        out_specs=[head_blk, head_blk,
                   pl.BlockSpec((1, FOX_HEADS, FOX_VROWS, tm), lambda i, j: (i, 0, 0, j)),
                   pl.BlockSpec((1, tm, LANES), row)],
        scratch_shapes=[pltpu.VMEM((8, LANES), F32)],
        compiler_params=_params(("parallel", "arbitrary")),
        name="foxpack",
    )(fq, kbuf, vbuf, ff, bias_pad, pk, pq)


def _fox_prompt_kernel(qi_ref, kj_ref, q_ref, k_ref, vt_ref, o_ref, m_ref, acc_ref):
    t = pl.program_id(1)
    i = qi_ref[t]
    j = kj_ref[t]
    tq = q_ref.shape[2]
    tk = k_ref.shape[2]
    ratio = tq // tk
    nt = (((1,), (1,)), ((), ()))

    @pl.when(j == 0)
    def _():
        m_ref[...] = jnp.full_like(m_ref, NEG)
        acc_ref[...] = jnp.zeros_like(acc_ref)

    def step(diagonal):
        if diagonal:
            keep = (lax.broadcasted_iota(jnp.int32, (tk, tq), 0) + (j - ratio * i) * tk
                    <= lax.broadcasted_iota(jnp.int32, (tk, tq), 1))
        for h0 in range(0, FOX_HEADS, FOX_GROUP):
            heads = range(h0, h0 + FOX_GROUP)
            st = {h: lax.dot_general(k_ref[0, h], q_ref[0, h], nt, preferred_element_type=F32)
                  for h in heads}
            if diagonal:
                st = {h: jnp.where(keep, s, NEG) for h, s in st.items()}
            m_old = {h: m_ref[h:h + 1, :] for h in heads}
            m_new = {h: jnp.maximum(m_old[h], jnp.max(st[h], axis=0, keepdims=True)) for h in heads}
            p = {h: jnp.exp2((st[h] - m_new[h]).astype(BF16)) for h in heads}
            for h in heads:
                alpha = jnp.exp2(m_old[h] - m_new[h])
                acc_ref[h] = alpha * acc_ref[h] + jnp.dot(vt_ref[0, h], p[h], preferred_element_type=F32)
                m_ref[h:h + 1, :] = m_new[h]

    @pl.when(j < ratio * i)
    def _():
        step(False)

    @pl.when(j >= ratio * i)
    def _():
        step(True)

    @pl.when(j == ratio * (i + 1) - 1)
    def _():
        outs = [acc_ref[h, 0:FOX_DIM, :] / acc_ref[h, FOX_DIM:FOX_DIM + 1, :] for h in range(FOX_HEADS)]
        o_ref[0] = jnp.concatenate(outs, axis=0).T.astype(o_ref.dtype)


def fox_prompt(qp, kp, vt, tq, tk):
    b, nh, l, _ = qp.shape
    w = nh * FOX_DIM
    nq = l // tq
    ratio = tq // tk
    assert ratio * tk == tq and nq * tq == l
    qi = np.asarray([i for i in range(nq) for _ in range(ratio * (i + 1))], np.int32)
    kj = np.asarray([j for i in range(nq) for j in range(ratio * (i + 1))], np.int32)
    grid_spec = pltpu.PrefetchScalarGridSpec(
        num_scalar_prefetch=2,
        grid=(b, len(qi)),
        in_specs=[pl.BlockSpec((1, nh, tq, LANES), lambda bi, t, qi, kj: (bi, 0, qi[t], 0)),
                  pl.BlockSpec((1, nh, tk, LANES), lambda bi, t, qi, kj: (bi, 0, kj[t], 0)),
                  pl.BlockSpec((1, nh, FOX_VROWS, tk), lambda bi, t, qi, kj: (bi, 0, 0, kj[t]))],
        out_specs=pl.BlockSpec((1, tq, w), lambda bi, t, qi, kj: (bi, qi[t], 0)),
        scratch_shapes=[pltpu.VMEM((FOX_HEADS, tq), F32),
                        pltpu.VMEM((FOX_HEADS, FOX_VROWS, tq), F32)])
    return pl.pallas_call(
        _fox_prompt_kernel,
        out_shape=jax.ShapeDtypeStruct((b, l, w), BF16),
        grid_spec=grid_spec,
        compiler_params=_params(("parallel", "arbitrary")),
        name="fox_prompt",
    )(jnp.asarray(qi), jnp.asarray(kj), qp, kp, vt)


def _fox_sample_kernel(q_ref, kp_ref, vp_ref, kn_ref, vn_ref, cq_ref, ck_ref, o_ref):
    lq = q_ref.shape[1]
    past = kp_ref.shape[4]
    ri = lax.broadcasted_iota(jnp.int32, (lq, lq), 0)
    ci = lax.broadcasted_iota(jnp.int32, (lq, lq), 1)
    nt = (((1,), (1,)), ((), ()))
    for h in range(FOX_HEADS):
        cols = slice(h * FOX_DIM, (h + 1) * FOX_DIM)
        q = q_ref[0, :, cols]
        cq = cq_ref[0, :, h:h + 1]
        sp = jnp.dot(q, kp_ref[0, 0, h].astype(BF16), preferred_element_type=F32)
        sp = sp + cq - ck_ref[0, h:h + 1, 0:past]
        sn = lax.dot_general(q, kn_ref[0, :, cols], nt, preferred_element_type=F32)
        sn = sn + cq - ck_ref[0, h:h + 1, past:past + lq]
        sn = jnp.where(ci <= ri, sn, NEG)
        m = jnp.maximum(jnp.max(sp, axis=-1, keepdims=True), jnp.max(sn, axis=-1, keepdims=True))
        pp = jnp.exp2(sp - m)
        pn = jnp.exp2(sn - m)
        den = jnp.sum(pp, axis=-1, keepdims=True) + jnp.sum(pn, axis=-1, keepdims=True)
        o = (lax.dot_general(pp.astype(BF16), vp_ref[0, 0, h].astype(BF16), nt, preferred_element_type=F32)
             + jnp.dot(pn.astype(BF16), vn_ref[0, :, cols], preferred_element_type=F32))
        o_ref[0, :, cols] = (o / den).astype(o_ref.dtype)


def fox_sample(q, kcache_t, vcache_t, layer, kn, vn, cq, ckt):
    b, lq, w = q.shape
    past = kcache_t.shape[4]
    lc = ckt.shape[2]
    new = lambda i: (i, 0, 0)
    old = lambda i: (layer, i, 0, 0, 0)
    cache_blk = pl.BlockSpec((1, 1, FOX_HEADS, FOX_DIM, past), old)
    return pl.pallas_call(
        _fox_sample_kernel,
        out_shape=jax.ShapeDtypeStruct((b, lq, w), BF16),
        grid=(b,),
        in_specs=[pl.BlockSpec((1, lq, w), new),
                  cache_blk,
                  cache_blk,
                  pl.BlockSpec((1, lq, w), new),
                  pl.BlockSpec((1, lq, w), new),
                  pl.BlockSpec((1, lq, FOX_HEADS), new),
                  pl.BlockSpec((1, FOX_HEADS, lc), new)],
        out_specs=pl.BlockSpec((1, lq, w), new),
        compiler_params=_params(("parallel",)),
        name="fox_sample",
    )(q, kcache_t, vcache_t, kn, vn, cq, ckt)


def _outproj_kernel(x_ref, hg_ref, fo_ref, mod_ref, fgain_ref, w_ref, gain2_ref,
                    wrh_ref, wrl_ref, br_ref, xo_ref, hs_ref, gs_ref, dest_ref, cnt_ref):
    s, ls, d = x_ref.shape
    tm = s * ls
    fo = fo_ref[...].astype(F32)
    fn = fo * lax.rsqrt(jnp.mean(fo * fo, axis=-1, keepdims=True) + EPS) * fgain_ref[...]
    mixed = (jnp.dot(hg_ref[...].reshape(tm, HG_WIDTH), w_ref[0:HG_WIDTH, :],
                     preferred_element_type=F32)
             + jnp.dot(fn.reshape(tm, FOX_WIDTH).astype(BF16), w_ref[HG_WIDTH:, :],
                       preferred_element_type=F32))
    x = x_ref[...] + mod_ref[:, 2:3, :] * mixed.reshape(s, ls, d)
    xo_ref[...] = x
    y = x * lax.rsqrt(jnp.mean(x * x, axis=-1, keepdims=True) + EPS) * gain2_ref[...]
    h2 = (y * (1.0 + mod_ref[:, 4:5, :]) + mod_ref[:, 3:4, :]).reshape(tm, d)
    h2b = h2.astype(BF16)

    nt = (((1,), (1,)), ((), ()))
    logits = lax.dot_general(wrh_ref[...], h2b, nt, preferred_element_type=F32) \
        + lax.dot_general(wrl_ref[...], h2b, nt, preferred_element_type=F32)
    z = jnp.exp(logits - jnp.max(logits, axis=0, keepdims=True))
    probs = z / jnp.sum(z, axis=0, keepdims=True)
    sel = probs + br_ref[...]
    rows = [sel[e:e + 1, :] for e in range(N_EXPERTS)]
    prow = [probs[e:e + 1, :] for e in range(N_EXPERTS)]

    def beats(a, ia, b_, ib):
        return jnp.where(a >= b_, 1.0, 0.0) if ia < ib else jnp.where(a > b_, 1.0, 0.0)

    top = []
    gscore = []
    for g in range(N_GROUPS):
        ids = range(g * GROUP_SIZE, (g + 1) * GROUP_SIZE)
        sc = jnp.zeros_like(rows[0])
        for e in ids:
            cnt = jnp.zeros_like(rows[0])
            for o in ids:
                if o != e:
                    cnt = cnt + beats(rows[o], o, rows[e], e)
            flag = jnp.where(cnt < TOP_K, 1.0, 0.0)
            top.append(flag)
            sc = sc + flag * rows[e]
        gscore.append(sc)
    chosen = []
    for g in range(N_GROUPS):
        cnt = jnp.zeros_like(rows[0])
        for o in range(N_GROUPS):
            if o != g:
                cnt = cnt + beats(gscore[o], o, gscore[g], g)
        chosen.append(jnp.where(cnt < 1.0, 1.0, 0.0))
    wts = [prow[e] * top[e] * chosen[e // GROUP_SIZE] for e in range(N_EXPERTS)]
    den = wts[0]
    for e in range(1, N_EXPERTS):
        den = den + wts[e]
    g4 = []
    for k in range(GROUP_SIZE):
        gk = wts[k]
        for g in range(1, N_GROUPS):
            gk = gk + wts[g * GROUP_SIZE + k]
        g4.append(gk / den)

    rr = hs_ref.shape[1]
    ki = lax.broadcasted_iota(jnp.int32, (tm, tm), 0)
    ji = lax.broadcasted_iota(jnp.int32, (tm, tm), 1)
    before = jnp.where(ki < ji, 1.0, 0.0).astype(BF16)
    chosen4 = jnp.concatenate(chosen, axis=0)
    rank = jnp.dot(chosen4.astype(BF16), before, preferred_element_type=F32)
    lane = lax.broadcasted_iota(jnp.int32, (1, LANES), 1)
    start = jnp.zeros((1, 1), F32)
    dest = jnp.zeros((1, tm), F32)
    cnt_row = jnp.zeros((1, LANES), F32)
    for g in range(N_GROUPS):
        n_g = jnp.sum(chosen[g], axis=1, keepdims=True)
        tiles_g = jnp.floor((n_g + (MOE_TILE - 1)) * (1.0 / MOE_TILE))
        dest = dest + chosen[g] * (start + rank[g:g + 1, :])
        cnt_row = cnt_row + jnp.where(lane == g, tiles_g, 0.0)
        start = start + tiles_g * MOE_TILE
    dest_i = dest.astype(jnp.int32)
    dest_ref[0] = dest_i
    cnt_ref[0] = cnt_row.astype(jnp.int32)
    perm = jnp.where(lax.broadcasted_iota(jnp.int32, (rr, tm), 0) == dest_i, 1.0, 0.0).astype(BF16)
    hs_ref[0] = jnp.dot(perm, h2b, preferred_element_type=F32).astype(BF16)
    g128 = jnp.concatenate(g4 + [jnp.zeros((LANES - GROUP_SIZE, tm), F32)], axis=0)
    ghi = g128.astype(BF16)
    glo = (g128 - ghi.astype(F32)).astype(BF16)
    gs_ref[0] = (lax.dot_general(perm, ghi, nt, preferred_element_type=F32)
                 + lax.dot_general(perm, glo, nt, preferred_element_type=F32))


def outproj(x, hg_out, fox_o, mod, fox_gain, w_out, gain2, wr_hi, wr_lo, b_router, s, ls):
    b, l, d = x.shape
    nblk = (b // s) * (l // ls)
    nl = l // ls
    tm = s * ls
    row = lambda i, j: (i, j, 0)
    const = lambda i, j: (0, 0)
    blk = lambda i, j: (i * nl + j, 0, 0)
    return pl.pallas_call(
        _outproj_kernel,
        out_shape=[jax.ShapeDtypeStruct((b, l, d), F32),
                   jax.ShapeDtypeStruct((nblk, MOE_ROWS, d), BF16),
                   jax.ShapeDtypeStruct((nblk, MOE_ROWS, LANES), F32),
                   jax.ShapeDtypeStruct((nblk, 1, tm), jnp.int32),
                   jax.ShapeDtypeStruct((nblk, 1, LANES), jnp.int32)],
        grid=(b // s, nl),
        in_specs=[pl.BlockSpec((s, ls, d), row),
                  pl.BlockSpec((s, ls, HG_WIDTH), row),
                  pl.BlockSpec((s, ls, FOX_WIDTH), row),
                  pl.BlockSpec((s, N_MOD, d), lambda i, j: (i, 0, 0)),
                  pl.BlockSpec((1, FOX_WIDTH), const),
                  pl.BlockSpec(w_out.shape, const, pipeline_mode=pl.Buffered(1)),
                  pl.BlockSpec((1, d), const),
                  pl.BlockSpec((N_EXPERTS, d), const),
                  pl.BlockSpec((N_EXPERTS, d), const),
                  pl.BlockSpec((N_EXPERTS, 1), const)],
        out_specs=[pl.BlockSpec((s, ls, d), row),
                   pl.BlockSpec((1, MOE_ROWS, d), blk),
                   pl.BlockSpec((1, MOE_ROWS, LANES), blk),
                   pl.BlockSpec((1, 1, tm), blk),
                   pl.BlockSpec((1, 1, LANES), blk)],
        compiler_params=_params(("parallel", "parallel")),
        name="outproj",
    )(x, hg_out, fox_o, mod, fox_gain, w_out, gain2, wr_hi, wr_lo, b_router)


def _moe_tables(cnt, rows_per_step):
    nblk = cnt.shape[0]
    tiles_per_step = rows_per_step // MOE_TILE
    max_tiles = nblk * (BLOCK_TOKENS // MOE_TILE + N_GROUPS)
    n_steps = -(-max_tiles // tiles_per_step) + N_GROUPS
    npair = N_GROUPS * nblk
    start_gm = (jnp.cumsum(cnt, axis=1) - cnt).T.reshape(-1)
    cnt_gm = cnt.T.reshape(-1)
    csum = jnp.cumsum(cnt_gm)
    tau = jnp.arange(max_tiles, dtype=jnp.int32)
    pair = jnp.minimum(jnp.sum((csum[None, :] <= tau[:, None]).astype(jnp.int32), axis=1), npair - 1)
    hot = (pair[:, None] == jnp.arange(npair, dtype=jnp.int32)[None, :]).astype(jnp.int32)
    k = tau - jnp.sum(hot * (csum - cnt_gm)[None, :], axis=1)
    valid = tau < csum[-1]
    tile_blk = jnp.where(valid, pair % nblk, 0).astype(jnp.int32)
    tile_row = jnp.where(valid, (jnp.sum(hot * start_gm[None, :], axis=1) + k) * MOE_TILE, 0).astype(jnp.int32)
    tot = jnp.sum(cnt, axis=0)
    gstart = jnp.cumsum(tot) - tot
    nsteps = (tot + tiles_per_step - 1) // tiles_per_step
    send = jnp.cumsum(nsteps)
    s = jnp.arange(n_steps, dtype=jnp.int32)
    sg = jnp.minimum(jnp.sum((send[None, :] <= s[:, None]).astype(jnp.int32), axis=1), N_GROUPS - 1)
    ghot = (sg[:, None] == jnp.arange(N_GROUPS, dtype=jnp.int32)[None, :]).astype(jnp.int32)
    pick = lambda v: jnp.sum(ghot * v[None, :], axis=1)
    first = pick(gstart) + (s - pick(send - nsteps)) * tiles_per_step
    num = jnp.clip(pick(gstart + tot) - first, 0, tiles_per_step)
    first = jnp.where(num > 0, first, 0)
    return sg.astype(jnp.int32), first.astype(jnp.int32), num.astype(jnp.int32), tile_blk, tile_row


def _moe_kernel(sg_ref, first_ref, num_ref, tblk_ref, trow_ref,
                hs_a, gs_a, hs_b, gs_b, wg_ref, wu_ref, wd_ref, yinit_a, yinit_b, ys_a, ys_b,
                hbuf, gbuf, ybuf, sem_in, sem_out):
    del sg_ref, yinit_a, yinit_b
    s = pl.program_id(0)
    ns = pl.num_programs(0)
    slot = lax.rem(s, 2)
    na = hs_a.shape[0]

    def tile(k):
        return pl.ds(pl.multiple_of(k * MOE_TILE, MOE_TILE), MOE_TILE)

    def src(step, k):
        t = first_ref[step] + k
        return tblk_ref[t], pl.ds(pl.multiple_of(trow_ref[t], MOE_TILE), MOE_TILE)

    class _Either:
        def __init__(self, blk, make):
            self.blk, self.make = blk, make

        def start(self):
            @pl.when(self.blk < na)
            def _():
                self.make(0, self.blk).start()

            @pl.when(self.blk >= na)
            def _():
                self.make(1, self.blk - na).start()

        def wait(self):
            self.make(0, 0).wait()

    def h_copy(step, sl, k):
        blk, rows = src(step, k)
        return _Either(blk, lambda g, b: pltpu.make_async_copy(
            (hs_a, hs_b)[g].at[b, rows, :], hbuf.at[sl, tile(k), :], sem_in.at[sl, 0]))

    def g_copy(step, sl, k):
        blk, rows = src(step, k)
        return _Either(blk, lambda g, b: pltpu.make_async_copy(
            (gs_a, gs_b)[g].at[b, rows, :], gbuf.at[sl, tile(k), :], sem_in.at[sl, 1]))

    def y_copy(step, sl, k):
        blk, rows = src(step, k)
        return _Either(blk, lambda g, b: pltpu.make_async_copy(
            ybuf.at[sl, tile(k), :], (ys_a, ys_b)[g].at[b, rows, :], sem_out.at[sl]))

    def each_tile(step, fn):
        def body(k, c):
            fn(k)
            return c
        lax.fori_loop(0, num_ref[step], body, 0)

    def start_gather(step, sl):
        def fn(k):
            h_copy(step, sl, k).start()
            g_copy(step, sl, k).start()
        each_tile(step, fn)

    def wait_gather(step, sl):
        def fn(k):
            h_copy(step, sl, k).wait()
            g_copy(step, sl, k).wait()
        each_tile(step, fn)

    @pl.when(s == 0)
    def _():
        hbuf[...] = jnp.zeros_like(hbuf)
        gbuf[...] = jnp.zeros_like(gbuf)
        start_gather(0, 0)

    wait_gather(s, slot)

    @pl.when(s + 1 < ns)
    def _():
        start_gather(s + 1, 1 - slot)

    @pl.when(s >= 2)
    def _():
        each_tile(s - 2, lambda k: y_copy(s - 2, slot, k).wait())

    @pl.when(num_ref[s] > 0)
    def _():
        h = hbuf[slot]
        g = gbuf[slot]
        acc = jnp.zeros((h.shape[0], wd_ref.shape[2]), F32)
        for k in range(GROUP_SIZE):
            a = jnp.dot(h, wg_ref[k].astype(BF16), preferred_element_type=F32)
            u = jnp.dot(h, wu_ref[k].astype(BF16), preferred_element_type=F32)
            he = (_silu(a) * u * g[:, k:k + 1]).astype(BF16)
            acc = acc + jnp.dot(he, wd_ref[k].astype(BF16), preferred_element_type=F32)
        ybuf[slot] = acc.astype(ybuf.dtype)

    each_tile(s, lambda k: y_copy(s, slot, k).start())

    @pl.when(s == ns - 1)
    def _():
        each_tile(s, lambda k: y_copy(s, slot, k).wait())

        @pl.when(s >= 1)
        def _():
            each_tile(s - 1, lambda k: y_copy(s - 1, 1 - slot, k).wait())


def moe_sparse(hs_a, gs_a, hs_b, gs_b, cnt, wg, wu, wd, layer):
    _, rr, d = hs_a.shape
    de = wg.shape[2]
    rows = MOE_STEP_ROWS
    sg, first, num, tile_blk, tile_row = _moe_tables(cnt, rows)
    n_steps = sg.shape[0]
    wmap = lambda s, sg, *_: (layer * N_GROUPS + sg[s], 0, 0)
    grid_spec = pltpu.PrefetchScalarGridSpec(
        num_scalar_prefetch=5,
        grid=(n_steps,),
        in_specs=[pl.BlockSpec(memory_space=pl.ANY)] * 4
        + [pl.BlockSpec((GROUP_SIZE, d, de), wmap, pipeline_mode=pl.Buffered(1)),
           pl.BlockSpec((GROUP_SIZE, d, de), wmap, pipeline_mode=pl.Buffered(1)),
           pl.BlockSpec((GROUP_SIZE, de, d), wmap, pipeline_mode=pl.Buffered(1))]
        + [pl.BlockSpec(memory_space=pl.ANY)] * 2,
        out_specs=[pl.BlockSpec(memory_space=pl.ANY)] * 2,
        scratch_shapes=[pltpu.VMEM((2, rows, d), BF16),
                        pltpu.VMEM((2, rows, LANES), F32),
                        pltpu.VMEM((2, rows, d), BF16),
                        pltpu.SemaphoreType.DMA((2, 2)),
                        pltpu.SemaphoreType.DMA((2,))])
    return pl.pallas_call(
        _moe_kernel,
        out_shape=[jax.ShapeDtypeStruct(hs_a.shape, BF16), jax.ShapeDtypeStruct(hs_b.shape, BF16)],
        grid_spec=grid_spec,
        input_output_aliases={12: 0, 13: 1},
        compiler_params=_params(("arbitrary",)),
        name="moe",
    )(sg, first, num, tile_blk, tile_row, hs_a, gs_a, hs_b, gs_b, wg, wu, wd,
      jnp.zeros(hs_a.shape, BF16), jnp.zeros(hs_b.shape, BF16))


def _combine_kernel(x_ref, ys_ref, dest_ref, mod_ref, fgain_ref, o_ref, *, final):
    s, ls, d = x_ref.shape
    tm = s * ls
    rr = ys_ref.shape[1]
    dcol = jnp.broadcast_to(dest_ref[0].astype(F32), (LANES, tm)).T
    lane = lax.broadcasted_iota(jnp.int32, (tm, LANES), 1).astype(F32)
    unperm = jnp.concatenate(
        [jnp.where(dcol == lane + float(c * LANES), 1.0, 0.0).astype(BF16) for c in range(rr // LANES)],
        axis=1)
    y = jnp.dot(unperm, ys_ref[0], preferred_element_type=F32)
    x = x_ref[...] + mod_ref[:, 5:6, :] * y.reshape(s, ls, d)
    if final:
        x = x * lax.rsqrt(jnp.mean(x * x, axis=-1, keepdims=True) + EPS) * fgain_ref[...]
    o_ref[...] = x


def combine(x, ys, dest, mod, final_gain, s, ls, final):
    b, l, d = x.shape
    nl = l // ls
    tm = s * ls
    row = lambda i, j: (i, j, 0)
    blk = lambda i, j: (i * nl + j, 0, 0)
    return pl.pallas_call(
        functools.partial(_combine_kernel, final=final),
        out_shape=jax.ShapeDtypeStruct((b, l, d), F32),
        grid=(b // s, nl),
        in_specs=[pl.BlockSpec((s, ls, d), row),
                  pl.BlockSpec((1, ys.shape[1], d), blk),
                  pl.BlockSpec((1, 1, tm), blk),
                  pl.BlockSpec((s, N_MOD, d), lambda i, j: (i, 0, 0)),
                  pl.BlockSpec((1, d), lambda i, j: (0, 0))],
        out_specs=pl.BlockSpec((s, ls, d), row),
        compiler_params=_params(("parallel", "parallel")),
        name="combine",
    )(x, ys, dest, mod, final_gain)


def _tile(b, l):
    ls = min(l, BLOCK_TOKENS)
    s = BLOCK_TOKENS // ls
    assert s * ls == BLOCK_TOKENS and b % s == 0 and l % ls == 0
    assert BLOCK_TOKENS + N_GROUPS * MOE_TILE <= MOE_ROWS
    return s, ls


def _mixer(x, mod, l, p, hg_state0, fox_past, kv_prev):
    depth = p['norm_mix_gain'].shape[0]
    b, seq, d = x.shape
    s, ls = _tile(b, seq)
    sample = fox_past is not None
    hg_in, fq, kbuf, vbuf, *k16v16, ff = inproj(x, mod, p['norm_mix_gain'][l], p['w_in'][l], s, ls,
                                                l, depth, kv_prev, sample)

    hg_out, hg_state = hgrn(hg_in, hg_state0, p['hg_lower_bounds'], p['hg_norm_gain'][l], l)

    if not sample:
        tk = min(seq, BLOCK_TOKENS)
        tq = min(seq, FOX_Q_TILE)
        qp, kp, vt, lf_pad = foxpack(fq, kbuf, vbuf, l, ff, p['fox_f_bias_pad'][l], tk)
        fox_o = fox_prompt(qp, kp, vt, tq, tk)
        logf = lf_pad[:, :, :FOX_HEADS]
    else:
        kcache, vcache, plogf = fox_past
        past = kcache.shape[4]
        fft = jnp.swapaxes(ff[:, :, :FOX_HEADS], 1, 2)
        logft, _ = forget_cumsum(fft, p['fox_f_bias'][l], True, seq)
        tot = past + seq
        pad = (-tot) % LANES
        allt = jnp.concatenate([jnp.swapaxes(plogf[l], 1, 2), logft,
                                jnp.zeros((b, FOX_HEADS, pad), F32)], axis=2)
        _, ct = forget_cumsum(allt, p['fox_f_bias'][l], False, LANES)
        ct = ct * LOG2E
        cq = jnp.swapaxes(ct[:, :, past:tot], 1, 2)
        fox_o = fox_sample(fq, kcache, vcache, l, k16v16[0], k16v16[1], cq, ct)
        logf = jnp.swapaxes(logft, 1, 2)

    routed = outproj(x, hg_out, fox_o, mod, p['fox_out_gain'][l], p['w_out'][l],
                     p['norm_ffn_gain'][l], p['wr_hi'], p['wr_lo'], p['b_router'], s, ls)
    return routed, (kbuf, vbuf), hg_state, logf


def _ffn(routed_p, routed_s, mod_p, mod_s, l, p, final):
    xm_p, hs_p, gs_p, dest_p, cnt_p = routed_p
    xm_s, hs_s, gs_s, dest_s, cnt_s = routed_s
    cnt = jnp.concatenate([cnt_p, cnt_s], axis=0)[:, 0, :N_GROUPS]
    ys_p, ys_s = moe_sparse(hs_p, gs_p, hs_s, gs_s, cnt, p['w_exp_gate'], p['w_exp_up'], p['w_exp_down'], l)
    outs = []
    for xm, ys, dest, mod in ((xm_p, ys_p, dest_p, mod_p), (xm_s, ys_s, dest_s, mod_s)):
        s, ls = _tile(xm.shape[0], xm.shape[1])
        outs.append(combine(xm, ys, dest, mod, p['final_norm_gain'], s, ls, final))
    return outs


def kernel(x_prompt, x_sample, cache_fox_k, cache_fox_v, cache_fox_logf, state_hgrn, c_prompt, c_sample,
           norm_mix_gain, norm_ffn_gain, w_ada, b_ada, w_in, hg_lower_bounds, hg_norm_gain,
           fox_f_bias, fox_out_gain, w_out, w_router, b_router, w_exp_gate, w_exp_up, w_exp_down,
           final_norm_gain):
    depth, d = norm_mix_gain.shape
    bp = x_prompt.shape[0]
    n_in = w_in.shape[2]
    n_pad = 4 * HG_WIDTH + 3 * FOX_WIDTH + LANES - n_in
    wr_t = w_router.T
    wr_hi = wr_t.astype(BF16)
    p = {
        'norm_mix_gain': norm_mix_gain.reshape(depth, 1, d),
        'norm_ffn_gain': norm_ffn_gain.reshape(depth, 1, d),
        'w_in': jnp.pad(w_in, ((0, 0), (0, 0), (0, n_pad))).astype(BF16),
        'hg_lower_bounds': hg_lower_bounds,
        'hg_norm_gain': hg_norm_gain.reshape(depth, 1, HG_WIDTH),
        'fox_f_bias': fox_f_bias.reshape(depth, FOX_HEADS, 1),
        'fox_f_bias_pad': jnp.pad(fox_f_bias, ((0, 0), (0, LANES - FOX_HEADS))).reshape(depth, 1, LANES),
        'fox_out_gain': fox_out_gain.reshape(depth, 1, FOX_WIDTH),
        'w_out': w_out.astype(BF16),
        'wr_hi': wr_hi,
        'wr_lo': (wr_t - wr_hi.astype(F32)).astype(BF16),
        'b_router': b_router.reshape(N_EXPERTS, 1),
        'w_exp_gate': w_exp_gate.reshape((depth * N_EXPERTS,) + w_exp_gate.shape[2:]),
        'w_exp_up': w_exp_up.reshape((depth * N_EXPERTS,) + w_exp_up.shape[2:]),
        'w_exp_down': w_exp_down.reshape((depth * N_EXPERTS,) + w_exp_down.shape[2:]),
        'final_norm_gain': final_norm_gain.reshape(1, d),
    }
    mods = ada_mod(jnp.concatenate([c_prompt, c_sample], axis=0), w_ada, b_ada)
    mods = mods.reshape(depth, -1, N_MOD, d)
    bs, lsq = x_sample.shape[:2]
    lp = x_prompt.shape[1]
    past = cache_fox_k.shape[2]
    fox_past = (jnp.transpose(cache_fox_k, (0, 1, 3, 4, 2)), jnp.transpose(cache_fox_v, (0, 1, 3, 4, 2)),
                cache_fox_logf)

    xp, xs = x_prompt, x_sample
    kv_p = kv_s = None
    st_p, st_s, lf_p, lf_s = [], [], [], []
    zero_state = jnp.zeros((bp, HG_HEADS, HG_DIM, HG_DIM), F32)
    for l in range(depth):
        routed_p, kv_p, st, lf = _mixer(xp, mods[l, :bp], l, p, zero_state, None, kv_p)
        st_p.append(st)
        lf_p.append(lf)
        routed_s, kv_s, st, lf = _mixer(xs, mods[l, bp:], l, p, state_hgrn[l], fox_past, kv_s)
        st_s.append(st)
        lf_s.append(lf)
        xp, xs = _ffn(routed_p, routed_s, mods[l, :bp], mods[l, bp:], l, p, l == depth - 1)
    heads = lambda a, b, l: a.reshape(depth, b, l, FOX_HEADS, FOX_DIM)
    return (xp, xs,
            jnp.stack(st_p), heads(kv_p[0], bp, lp), heads(kv_p[1], bp, lp), jnp.stack(lf_p),
            jnp.stack(st_s), heads(kv_s[0], bs, lsq), heads(kv_s[1], bs, lsq), jnp.stack(lf_s))
```

```python
import functools
import math

import numpy as np
import jax
import jax.numpy as jnp
from jax import lax
from jax.experimental import pallas as pl
from jax.experimental.pallas import tpu as pltpu

HG_HEADS = 4
HG_DIM = 128
HG_WIDTH = HG_HEADS * HG_DIM
FOX_HEADS = 8
FOX_DIM = 64
FOX_WIDTH = FOX_HEADS * FOX_DIM
N_EXPERTS = 16
N_GROUPS = 4
GROUP_SIZE = N_EXPERTS // N_GROUPS
TOP_K = 2
N_MOD = 6
EPS = 1e-6

LANES = 128
HG_CHUNK = 128
HG_SUB = 16
HG_STREAMS = 2
MOE_TILE = 16
BLOCK_TOKENS = 512
MOE_ROWS = 640
MOE_STEP_ROWS = 1024
VMEM_LIMIT = 56 * 1024 * 1024

F32 = jnp.float32
BF16 = jnp.bfloat16
NEG = -1e30
LOG2E = math.log2(math.e)


def _params(sem, vmem=VMEM_LIMIT):
    return pltpu.CompilerParams(dimension_semantics=sem, vmem_limit_bytes=vmem)


def _split3(x):
    hi = x.astype(BF16)
    r1 = x - hi.astype(F32)
    mid = r1.astype(BF16)
    lo = (r1 - mid.astype(F32)).astype(BF16)
    return hi, mid, lo


def _sigmoid(x):
    return 1.0 / (1.0 + jnp.exp(-x))


def _log_sigmoid(x):
    return jnp.minimum(x, 0.0) - jnp.log(1.0 + jnp.exp(-jnp.abs(x)))


def _silu(x):
    return x * _sigmoid(x)


def _ada_kernel(c_ref, w_ref, b_ref, o_ref):
    s = _silu(c_ref[...]).astype(BF16)
    o_ref[0] = jnp.dot(s, w_ref[0].astype(BF16), preferred_element_type=F32) + b_ref[0]


def ada_mod(c, w_ada, b_ada):
    depth, d, n = w_ada.shape
    nb = c.shape[0]
    tn = 1536
    return pl.pallas_call(
        _ada_kernel,
        out_shape=jax.ShapeDtypeStruct((depth, nb, n), F32),
        grid=(depth, n // tn),
        in_specs=[pl.BlockSpec((nb, d), lambda l, j: (0, 0)),
                  pl.BlockSpec((1, d, tn), lambda l, j: (l, 0, j)),
                  pl.BlockSpec((1, 1, tn), lambda l, j: (l, 0, j))],
        out_specs=pl.BlockSpec((1, nb, tn), lambda l, j: (l, 0, j)),
        compiler_params=_params(("arbitrary", "arbitrary")),
        name="ada_mod",
    )(c, w_ada, b_ada.reshape(depth, 1, n))


def _inproj_kernel(x_ref, mod_ref, gain_ref, w_ref, *refs, has_prev, emit16):
    outs = refs[2:] if has_prev else refs
    hg_ref, fq_ref, fk32_ref, fv32_ref = outs[:4]
    ff_ref = outs[-1]
    s, ls, d = x_ref.shape
    x = x_ref[...]
    y = x * lax.rsqrt(jnp.mean(x * x, axis=-1, keepdims=True) + EPS) * gain_ref[...]
    h = y * (1.0 + mod_ref[:, 1:2, :]) + mod_ref[:, 0:1, :]
    hb = h.reshape(s * ls, d).astype(BF16)

    def proj(lo, hi):
        return jnp.dot(hb, w_ref[:, lo:hi], preferred_element_type=F32)

    c0 = 4 * HG_WIDTH
    hg_ref[...] = proj(0, c0).reshape(s, ls, c0)
    fq = proj(c0, c0 + FOX_WIDTH) * (FOX_DIM ** -0.5 * LOG2E)
    fq_ref[...] = fq.reshape(s, ls, FOX_WIDTH).astype(BF16)
    fk = proj(c0 + FOX_WIDTH, c0 + 2 * FOX_WIDTH).reshape(s, ls, FOX_WIDTH)
    fv = proj(c0 + 2 * FOX_WIDTH, c0 + 3 * FOX_WIDTH).reshape(s, ls, FOX_WIDTH)
    for slot in range(fk32_ref.shape[0]):
        fk32_ref[slot] = fk
        fv32_ref[slot] = fv
    if emit16:
        outs[4][...] = fk.astype(BF16)
        outs[5][...] = fv.astype(BF16)
    ff_ref[...] = proj(c0 + 3 * FOX_WIDTH, c0 + 3 * FOX_WIDTH + LANES).reshape(s, ls, LANES)


def inproj(x, mod, gain, w_pad, s, ls, layer, depth, kv_prev, emit16):
    b, l, d = x.shape
    n = w_pad.shape[1]
    row = lambda i, j: (i, j, 0)
    lrow = lambda i, j: (layer, i, j, 0)
    shapes = [jax.ShapeDtypeStruct((b, l, 4 * HG_WIDTH), F32),
              jax.ShapeDtypeStruct((b, l, FOX_WIDTH), BF16),
              jax.ShapeDtypeStruct((depth, b, l, FOX_WIDTH), F32),
              jax.ShapeDtypeStruct((depth, b, l, FOX_WIDTH), F32)]
    slots = depth if kv_prev is None else 1
    if kv_prev is None:
        assert layer == 0
    specs = [pl.BlockSpec((s, ls, 4 * HG_WIDTH), row),
             pl.BlockSpec((s, ls, FOX_WIDTH), row),
             pl.BlockSpec((slots, s, ls, FOX_WIDTH), lrow),
             pl.BlockSpec((slots, s, ls, FOX_WIDTH), lrow)]
    if emit16:
        shapes += [jax.ShapeDtypeStruct((b, l, FOX_WIDTH), BF16)] * 2
        specs += [pl.BlockSpec((s, ls, FOX_WIDTH), row)] * 2
    shapes.append(jax.ShapeDtypeStruct((b, l, LANES), F32))
    specs.append(pl.BlockSpec((s, ls, LANES), row))
    in_specs = [pl.BlockSpec((s, ls, d), row),
                pl.BlockSpec((s, N_MOD, d), lambda i, j: (i, 0, 0)),
                pl.BlockSpec((1, d), lambda i, j: (0, 0)),
                pl.BlockSpec((d, n), lambda i, j: (0, 0), pipeline_mode=pl.Buffered(1))]
    args = [x, mod, gain, w_pad]
    aliases = {}
    if kv_prev is not None:
        in_specs += [pl.BlockSpec(memory_space=pl.ANY)] * 2
        args += list(kv_prev)
        aliases = {4: 2, 5: 3}
    return pl.pallas_call(
        functools.partial(_inproj_kernel, has_prev=kv_prev is not None, emit16=emit16),
        out_shape=shapes,
        grid=(b // s, l // ls),
        in_specs=in_specs,
        out_specs=specs,
        input_output_aliases=aliases,
        compiler_params=_params(("parallel", "parallel")),
        name="inproj",
    )(*args)


def _cumsum_kernel(x_ref, bias_ref, lf_ref, c_ref, carry_ref, *, apply_ls):
    @pl.when(pl.program_id(1) == 0)
    def _():
        carry_ref[...] = jnp.zeros_like(carry_ref)

    bb, nh, tc = x_ref.shape
    x = x_ref[...].reshape(bb * nh, tc)
    lf = _log_sigmoid(x + bias_ref[...]) if apply_ls else x
    lf_ref[...] = lf.reshape(bb, nh, tc)
    r = lax.broadcasted_iota(jnp.int32, (tc, tc), 0)
    c = lax.broadcasted_iota(jnp.int32, (tc, tc), 1)
    tri = jnp.where(r <= c, 1.0, 0.0).astype(BF16)
    hi, mid, lo = _split3(lf)
    tot = (jnp.dot(hi, tri, preferred_element_type=F32)
           + jnp.dot(mid, tri, preferred_element_type=F32)
           + jnp.dot(lo, tri, preferred_element_type=F32)) + carry_ref[:, 0:1]
    c_ref[...] = tot.reshape(bb, nh, tc)
    carry_ref[...] = jnp.broadcast_to(tot[:, tc - 1:tc], carry_ref.shape)


def forget_cumsum(xt, bias, apply_ls, tc):
    b, h, l = xt.shape
    blk = pl.BlockSpec((b, h, tc), lambda i, j: (0, 0, j))
    return pl.pallas_call(
        functools.partial(_cumsum_kernel, apply_ls=apply_ls),
        out_shape=[jax.ShapeDtypeStruct((b, h, l), F32)] * 2,
        grid=(1, l // tc),
        in_specs=[blk, pl.BlockSpec((b * h, 1), lambda i, j: (0, 0))],
        out_specs=[blk, blk],
        scratch_shapes=[pltpu.VMEM((b * h, LANES), F32)],
        compiler_params=_params(("arbitrary", "arbitrary")),
        name="forget_cumsum",
    )(xt, jnp.tile(bias, (b, 1)))


HG_HALVES = (8, 4, 2, 1)


def _hgrn_sum_matrix():
    c = HG_CHUNK
    t = np.arange(c)[:, None]
    u = np.arange(c)[None, :]
    mats = [(u <= t)]
    for w in HG_HALVES:
        pos = t % (2 * w)
        mid = t - pos + w - 1
        upper = pos >= w
        mats.append(np.where(upper, (u > mid) & (u <= t), (u > t) & (u <= mid)))
    return jnp.asarray(np.concatenate(mats, axis=0).astype(np.float32), BF16)


def _hgrn_kernel(hin_ref, s0_ref, lbp_ref, gain_ref, sums_ref, out_ref, sfin_ref, st_ref, *, layer):
    t = pl.program_id(1)
    nt = pl.num_programs(1)
    c = HG_CHUNK
    nsub = c // HG_SUB
    nb = hin_ref.shape[0]
    rows_in = hin_ref.shape[1]

    def pad_rows(a):
        if rows_in == c:
            return a
        return jnp.concatenate([a, jnp.zeros((c - rows_in, a.shape[1]), a.dtype)], axis=0)

    @pl.when(t == 0)
    def _():
        for bi in range(nb):
            for h in range(HG_HEADS):
                st_ref[bi, h] = s0_ref[bi, h].T

    lbp = lbp_ref[...]
    e = jnp.exp(lbp - jnp.max(lbp, axis=0, keepdims=True))
    p = e / jnp.sum(e, axis=0, keepdims=True)
    acc = p[0:1]
    first = acc
    for i in range(1, layer + 1):
        acc = acc + p[i:i + 1]
    lb_all = acc - first

    ri = lax.broadcasted_iota(jnp.int32, (c, c), 0)
    ci = lax.broadcasted_iota(jnp.int32, (c, c), 1)
    ones = jnp.ones((HG_DIM, HG_DIM), BF16)
    nt_dims = (((1,), (1,)), ((), ()))

    streams = [(bi, h) for bi in range(nb) for h in range(HG_HEADS)]
    q, kk, v, b, sums, att = {}, {}, {}, {}, {}, {}
    for bi, h in streams:
        lb = lb_all[:, h * HG_DIM:(h + 1) * HG_DIM]
        hq = hin_ref[bi, :, h * HG_DIM:(h + 1) * HG_DIM]
        hf = hin_ref[bi, :, HG_WIDTH + h * HG_DIM:HG_WIDTH + (h + 1) * HG_DIM]
        hi = hin_ref[bi, :, 2 * HG_WIDTH + h * HG_DIM:2 * HG_WIDTH + (h + 1) * HG_DIM]
        q[bi, h] = pad_rows(_silu(hq) * (HG_DIM ** -0.5))
        a1 = jnp.log(lb)
        e = jnp.exp(-jnp.abs(hf))
        den = 1.0 + e
        a2 = jnp.log1p(-lb) + (jnp.minimum(hf, 0.0) - jnp.log(den))
        mx = jnp.maximum(a1, a2)
        lf = pad_rows(mx + jnp.log(jnp.exp(a1 - mx) + jnp.exp(a2 - mx)))
        kk[bi, h] = pad_rows((1.0 - lb) * (jnp.where(hf > 0.0, e, 1.0) / den))
        v[bi, h] = pad_rows(hi)
        lhi, lmid, llo = _split3(lf)
        sums[bi, h] = (jnp.dot(sums_ref[...], lhi, preferred_element_type=F32)
                       + jnp.dot(sums_ref[...], lmid, preferred_element_type=F32)
                       + jnp.dot(sums_ref[...], llo, preferred_element_type=F32))
        b[bi, h] = sums[bi, h][0:c]

    for key in streams:
        att[key] = jnp.where(ri == ci, jnp.dot((q[key] * kk[key]).astype(BF16), ones,
                                               preferred_element_type=F32), 0.0)
    for lvl, w in enumerate(HG_HALVES):
        shift = (2 * w).bit_length() - 1
        pair = ((ri >> shift) == (ci >> shift)) & ((ri & (2 * w - 1)) >= w) & ((ci & (2 * w - 1)) < w)
        for key in streams:
            dec = jnp.exp(sums[key][(lvl + 1) * c:(lvl + 2) * c])
            sc = lax.dot_general((q[key] * dec).astype(BF16), (kk[key] * dec).astype(BF16), nt_dims,
                                 preferred_element_type=F32)
            att[key] = att[key] + jnp.where(pair, sc, 0.0)

    khat = {key: None for key in streams}
    r_prev = {key: jnp.zeros((1, HG_DIM), F32) for key in streams}
    qhat_blocks = {key: [] for key in streams}
    cross = {key: [] for key in streams}
    for i in range(nsub):
        r0 = i * HG_SUB
        for key in streams:
            bb = b[key][r0:r0 + HG_SUB]
            r_next = b[key][r0 + HG_SUB - 1:r0 + HG_SUB]
            qt = q[key][r0:r0 + HG_SUB] * jnp.exp(bb - r_prev[key])
            if khat[key] is None:
                cross[key].append(jnp.zeros((HG_SUB, c), F32))
            else:
                kfull = jnp.concatenate([khat[key], jnp.zeros((c - r0, HG_DIM), F32)], axis=0)
                cross[key].append(lax.dot_general(qt.astype(BF16), kfull.astype(BF16), nt_dims,
                                                  preferred_element_type=F32))
            qhat_blocks[key].append(qt * jnp.exp(r_prev[key]))
            kt = kk[key][r0:r0 + HG_SUB] * jnp.exp(r_next - bb)
            if khat[key] is None:
                khat[key] = kt
            else:
                khat[key] = jnp.concatenate([khat[key] * jnp.exp(r_next - r_prev[key]), kt], axis=0)
            r_prev[key] = r_next

    for bi, h in streams:
        key = (bi, h)
        cols = slice(h * HG_DIM, (h + 1) * HG_DIM)
        st = st_ref[bi, h]
        a = att[key] + jnp.concatenate(cross[key], axis=0)
        qhat = jnp.concatenate(qhat_blocks[key], axis=0).astype(BF16)
        o = jnp.dot(a.astype(BF16), v[key].astype(BF16), preferred_element_type=F32)
        o = o + lax.dot_general(qhat, st.astype(BF16), nt_dims, preferred_element_type=F32)
        st_ref[bi, h] = st * jnp.exp(r_prev[key]) + jnp.dot(
            v[key].T.astype(BF16), khat[key].astype(BF16), preferred_element_type=F32)

        hg = hin_ref[bi, :, 3 * HG_WIDTH + h * HG_DIM:3 * HG_WIDTH + (h + 1) * HG_DIM]
        o = o[:rows_in]
        o = o * lax.rsqrt(jnp.mean(o * o, axis=-1, keepdims=True) + EPS)
        o = o * gain_ref[:, cols] * _silu(hg)
        out_ref[bi, :, cols] = o.astype(out_ref.dtype)

    @pl.when(t == nt - 1)
    def _():
        for bi, h in streams:
            sfin_ref[bi, h] = st_ref[bi, h].T


def hgrn(hin, s0, lbp, gain, layer):
    b, l, w = hin.shape
    c = min(HG_CHUNK, l)
    assert l % c == 0 and c % HG_SUB == 0
    nt = l // c
    nb = HG_STREAMS
    assert b % nb == 0
    sums = _hgrn_sum_matrix()
    return pl.pallas_call(
        functools.partial(_hgrn_kernel, layer=layer),
        out_shape=[jax.ShapeDtypeStruct((b, l, HG_WIDTH), BF16),
                   jax.ShapeDtypeStruct((b, HG_HEADS, HG_DIM, HG_DIM), F32)],
        grid=(b // nb, nt),
        in_specs=[pl.BlockSpec((nb, c, w), lambda i, j: (i, j, 0)),
                  pl.BlockSpec((nb, HG_HEADS, HG_DIM, HG_DIM), lambda i, j: (i, 0, 0, 0)),
                  pl.BlockSpec(lbp.shape, lambda i, j: (0, 0)),
                  pl.BlockSpec((1, HG_WIDTH), lambda i, j: (0, 0)),
                  pl.BlockSpec(sums.shape, lambda i, j: (0, 0))],
        out_specs=[pl.BlockSpec((nb, c, HG_WIDTH), lambda i, j: (i, j, 0)),
                   pl.BlockSpec((nb, HG_HEADS, HG_DIM, HG_DIM), lambda i, j: (i, 0, 0, 0))],
        scratch_shapes=[pltpu.VMEM((nb, HG_HEADS, HG_DIM, HG_DIM), F32)],
        compiler_params=_params(("parallel", "arbitrary")),
        name="hgrn",
    )(hin, s0, lbp, gain, sums)


AUG_ONE = FOX_DIM
AUG_NEG = FOX_DIM + 3
ONE_LANE = FOX_HEADS
FOX_GROUP = 4
FOX_VROWS = FOX_DIM + 16


def _placement():
    pk = np.zeros((3 * LANES, FOX_HEADS * LANES), np.float32)
    pq = np.zeros((3 * LANES, FOX_HEADS * LANES), np.float32)
    for h in range(FOX_HEADS):
        for part in range(3):
            pk[part * LANES + h, h * LANES + AUG_NEG + part] = -1.0
            pq[part * LANES + h, h * LANES + AUG_ONE + part] = 1.0
            pk[ONE_LANE, h * LANES + AUG_ONE + part] = 1.0
            pq[ONE_LANE, h * LANES + AUG_NEG + part] = 1.0
    return jnp.asarray(pk, BF16), jnp.asarray(pq, BF16)


def _foxpack_kernel(q_ref, k_ref, v_ref, ff_ref, bias_ref, pk_ref, pq_ref,
                    qp_ref, kp_ref, vt_ref, lf_ref, carry_ref):
    @pl.when(pl.program_id(1) == 0)
    def _():
        carry_ref[...] = jnp.zeros_like(carry_ref)

    tm = q_ref.shape[1]
    lane = lax.broadcasted_iota(jnp.int32, (tm, LANES), 1)
    lf = jnp.where(lane < FOX_HEADS, _log_sigmoid(ff_ref[0] + bias_ref[...]), 0.0)
    lf_ref[0] = lf
    ri = lax.broadcasted_iota(jnp.int32, (tm, tm), 0)
    ci = lax.broadcasted_iota(jnp.int32, (tm, tm), 1)
    tril = jnp.where(ci <= ri, 1.0, 0.0).astype(BF16)
    hi, mid, lo = _split3(lf)
    c = (jnp.dot(tril, hi, preferred_element_type=F32)
         + jnp.dot(tril, mid, preferred_element_type=F32)
         + jnp.dot(tril, lo, preferred_element_type=F32)) + carry_ref[0:1, :]
    carry_ref[...] = jnp.broadcast_to(c[tm - 1:tm, :], carry_ref.shape)

    chi, cmid, clo = _split3(c * LOG2E)
    chi = jnp.where(lane == ONE_LANE, 1.0, chi.astype(F32)).astype(BF16)
    cterms = jnp.concatenate([chi, cmid, clo], axis=1)
    augk = jnp.dot(cterms, pk_ref[...], preferred_element_type=F32)
    augq = jnp.dot(cterms, pq_ref[...], preferred_element_type=F32)
    for h in range(FOX_HEADS):
        blk = slice((h // 2) * LANES, (h // 2 + 1) * LANES)
        hcols = slice(h * LANES, (h + 1) * LANES)
        kh = k_ref[0, 0, :, blk]
        qh = q_ref[0, :, blk].astype(F32)
        if h % 2:
            kh = pltpu.roll(kh, FOX_DIM, axis=1)
            qh = pltpu.roll(qh, FOX_DIM, axis=1)
        kp_ref[0, h] = jnp.where(lane < FOX_DIM, kh, augk[:, hcols]).astype(BF16)
        qp_ref[0, h] = jnp.where(lane < FOX_DIM, qh, augq[:, hcols]).astype(BF16)
    vt = v_ref[0, 0].T
    tail = jnp.where(lax.broadcasted_iota(jnp.int32, (FOX_VROWS - FOX_DIM, tm), 0) == 0, 1.0, 0.0)
    for h in range(FOX_HEADS):
        vt_ref[0, h] = jnp.concatenate([vt[h * FOX_DIM:(h + 1) * FOX_DIM], tail], axis=0).astype(BF16)


def foxpack(fq, kbuf, vbuf, layer, ff, bias_pad, tm):
    b, l, w = fq.shape
    pk, pq = _placement()
    row = lambda i, j: (i, j, 0)
    lrow = lambda i, j: (layer, i, j, 0)
    const = lambda i, j: (0, 0)
    head_blk = pl.BlockSpec((1, FOX_HEADS, tm, LANES), lambda i, j: (i, 0, j, 0))
    return pl.pallas_call(
        _foxpack_kernel,
        out_shape=[jax.ShapeDtypeStruct((b, FOX_HEADS, l, LANES), BF16),
                   jax.ShapeDtypeStruct((b, FOX_HEADS, l, LANES), BF16),
                   jax.ShapeDtypeStruct((b, FOX_HEADS, FOX_VROWS, l), BF16),
                   jax.ShapeDtypeStruct((b, l, LANES), F32)],
        grid=(b, l // tm),
        in_specs=[pl.BlockSpec((1, tm, w), row),
                  pl.BlockSpec((1, 1, tm, w), lrow),
                  pl.BlockSpec((1, 1, tm, w), lrow),
                  pl.BlockSpec((1, tm, LANES), row),
                  pl.BlockSpec((1, LANES), const),
                  pl.BlockSpec(pk.shape, const),
                  pl.BlockSpec(pq.shape, const)],
        out_specs=[head_blk, head_blk,
                   pl.BlockSpec((1, FOX_HEADS, FOX_VROWS, tm), lambda i, j: (i, 0, 0, j)),
                   pl.BlockSpec((1, tm, LANES), row)],
        scratch_shapes=[pltpu.VMEM((8, LANES), F32)],
        compiler_params=_params(("parallel", "arbitrary")),
        name="foxpack",
    )(fq, kbuf, vbuf, ff, bias_pad, pk, pq)


def _fox_prompt_kernel(qi_ref, kj_ref, q_ref, k_ref, vt_ref, o_ref, m_ref, acc_ref):
    t = pl.program_id(1)
    i = qi_ref[t]
    j = kj_ref[t]
    tq = q_ref.shape[2]
    tk = k_ref.shape[2]
    assert tq == tk
    nt = (((1,), (1,)), ((), ()))

    @pl.when(j == 0)
    def _():
        m_ref[...] = jnp.full_like(m_ref, NEG)
        acc_ref[...] = jnp.zeros_like(acc_ref)

    def step(diagonal):
        if diagonal:
            keep = (lax.broadcasted_iota(jnp.int32, (tk, tq), 0)
                    <= lax.broadcasted_iota(jnp.int32, (tk, tq), 1))
        for h0 in range(0, FOX_HEADS, FOX_GROUP):
            heads = range(h0, h0 + FOX_GROUP)
            st = {h: lax.dot_general(k_ref[0, h], q_ref[0, h], nt, preferred_element_type=F32)
                  for h in heads}
            if diagonal:
                st = {h: jnp.where(keep, s, NEG) for h, s in st.items()}
            m_old = {h: m_ref[h:h + 1, :] for h in heads}
            m_new = {h: jnp.maximum(m_old[h], jnp.max(st[h], axis=0, keepdims=True)) for h in heads}
            p = {h: jnp.exp2((st[h] - m_new[h]).astype(BF16)) for h in heads}
            for h in heads:
                alpha = jnp.exp2(m_old[h] - m_new[h])
                acc_ref[h] = alpha * acc_ref[h] + jnp.dot(vt_ref[0, h], p[h], preferred_element_type=F32)
                m_ref[h:h + 1, :] = m_new[h]

    @pl.when(j < i)
    def _():
        step(False)

    @pl.when(j == i)
    def _():
        step(True)
        outs = [acc_ref[h, 0:FOX_DIM, :] / acc_ref[h, FOX_DIM:FOX_DIM + 1, :] for h in range(FOX_HEADS)]
        o_ref[0] = jnp.concatenate(outs, axis=0).T.astype(o_ref.dtype)


def fox_prompt(qp, kp, vt, tq):
    b, nh, l, _ = qp.shape
    w = nh * FOX_DIM
    nq = l // tq
    tk = tq
    assert nq * tq == l
    qi = np.asarray([i for i in range(nq) for _ in range(i + 1)], np.int32)
    kj = np.asarray([j for i in range(nq) for j in range(i + 1)], np.int32)
    grid_spec = pltpu.PrefetchScalarGridSpec(
        num_scalar_prefetch=2,
        grid=(b, len(qi)),
        in_specs=[pl.BlockSpec((1, nh, tq, LANES), lambda bi, t, qi, kj: (bi, 0, qi[t], 0)),
                  pl.BlockSpec((1, nh, tk, LANES), lambda bi, t, qi, kj: (bi, 0, kj[t], 0)),
                  pl.BlockSpec((1, nh, FOX_VROWS, tk), lambda bi, t, qi, kj: (bi, 0, 0, kj[t]))],
        out_specs=pl.BlockSpec((1, tq, w), lambda bi, t, qi, kj: (bi, qi[t], 0)),
        scratch_shapes=[pltpu.VMEM((FOX_HEADS, tq), F32),
                        pltpu.VMEM((FOX_HEADS, FOX_VROWS, tq), F32)])
    return pl.pallas_call(
        _fox_prompt_kernel,
        out_shape=jax.ShapeDtypeStruct((b, l, w), BF16),
        grid_spec=grid_spec,
        compiler_params=_params(("parallel", "arbitrary")),
        name="fox_prompt",
    )(jnp.asarray(qi), jnp.asarray(kj), qp, kp, vt)


def _fox_sample_kernel(q_ref, kp_ref, vp_ref, kn_ref, vn_ref, cq_ref, ck_ref, o_ref):
    lq = q_ref.shape[1]
    past = kp_ref.shape[4]
    ri = lax.broadcasted_iota(jnp.int32, (lq, lq), 0)
    ci = lax.broadcasted_iota(jnp.int32, (lq, lq), 1)
    nt = (((1,), (1,)), ((), ()))
    for h in range(FOX_HEADS):
        cols = slice(h * FOX_DIM, (h + 1) * FOX_DIM)
        q = q_ref[0, :, cols]
        cq = cq_ref[0, :, h:h + 1]
        sp = jnp.dot(q, kp_ref[0, 0, h].astype(BF16), preferred_element_type=F32)
        sp = sp + cq - ck_ref[0, h:h + 1, 0:past]
        sn = lax.dot_general(q, kn_ref[0, :, cols], nt, preferred_element_type=F32)
        sn = sn + cq - ck_ref[0, h:h + 1, past:past + lq]
        sn = jnp.where(ci <= ri, sn, NEG)
        m = jnp.maximum(jnp.max(sp, axis=-1, keepdims=True), jnp.max(sn, axis=-1, keepdims=True))
        pp = jnp.exp2(sp - m)
        pn = jnp.exp2(sn - m)
        den = jnp.sum(pp, axis=-1, keepdims=True) + jnp.sum(pn, axis=-1, keepdims=True)
        o = (lax.dot_general(pp.astype(BF16), vp_ref[0, 0, h].astype(BF16), nt, preferred_element_type=F32)
             + jnp.dot(pn.astype(BF16), vn_ref[0, :, cols], preferred_element_type=F32))
        o_ref[0, :, cols] = (o / den).astype(o_ref.dtype)


def fox_sample(q, kcache_t, vcache_t, layer, kn, vn, cq, ckt):
    b, lq, w = q.shape
    past = kcache_t.shape[4]
    lc = ckt.shape[2]
    new = lambda i: (i, 0, 0)
    old = lambda i: (layer, i, 0, 0, 0)
    cache_blk = pl.BlockSpec((1, 1, FOX_HEADS, FOX_DIM, past), old)
    return pl.pallas_call(
        _fox_sample_kernel,
        out_shape=jax.ShapeDtypeStruct((b, lq, w), BF16),
        grid=(b,),
        in_specs=[pl.BlockSpec((1, lq, w), new),
                  cache_blk,
                  cache_blk,
                  pl.BlockSpec((1, lq, w), new),
                  pl.BlockSpec((1, lq, w), new),
                  pl.BlockSpec((1, lq, FOX_HEADS), new),
                  pl.BlockSpec((1, FOX_HEADS, lc), new)],
        out_specs=pl.BlockSpec((1, lq, w), new),
        compiler_params=_params(("parallel",)),
        name="fox_sample",
    )(q, kcache_t, vcache_t, kn, vn, cq, ckt)


def _outproj_kernel(x_ref, hg_ref, fo_ref, mod_ref, fgain_ref, w_ref, gain2_ref,
                    wrh_ref, wrl_ref, br_ref, xo_ref, hs_ref, gs_ref, dest_ref, cnt_ref):
    s, ls, d = x_ref.shape
    tm = s * ls
    fo = fo_ref[...].astype(F32)
    fn = fo * lax.rsqrt(jnp.mean(fo * fo, axis=-1, keepdims=True) + EPS) * fgain_ref[...]
    mixed = (jnp.dot(hg_ref[...].reshape(tm, HG_WIDTH), w_ref[0:HG_WIDTH, :],
                     preferred_element_type=F32)
             + jnp.dot(fn.reshape(tm, FOX_WIDTH).astype(BF16), w_ref[HG_WIDTH:, :],
                       preferred_element_type=F32))
    x = x_ref[...] + mod_ref[:, 2:3, :] * mixed.reshape(s, ls, d)
    xo_ref[...] = x
    y = x * lax.rsqrt(jnp.mean(x * x, axis=-1, keepdims=True) + EPS) * gain2_ref[...]
    h2 = (y * (1.0 + mod_ref[:, 4:5, :]) + mod_ref[:, 3:4, :]).reshape(tm, d)
    h2b = h2.astype(BF16)

    nt = (((1,), (1,)), ((), ()))
    logits = lax.dot_general(wrh_ref[...], h2b, nt, preferred_element_type=F32) \
        + lax.dot_general(wrl_ref[...], h2b, nt, preferred_element_type=F32)
    z = jnp.exp(logits - jnp.max(logits, axis=0, keepdims=True))
    probs = z / jnp.sum(z, axis=0, keepdims=True)
    sel = probs + br_ref[...]
    rows = [sel[e:e + 1, :] for e in range(N_EXPERTS)]
    prow = [probs[e:e + 1, :] for e in range(N_EXPERTS)]

    def beats(a, ia, b_, ib):
        return jnp.where(a >= b_, 1.0, 0.0) if ia < ib else jnp.where(a > b_, 1.0, 0.0)

    top = []
    gscore = []
    for g in range(N_GROUPS):
        ids = range(g * GROUP_SIZE, (g + 1) * GROUP_SIZE)
        sc = jnp.zeros_like(rows[0])
        for e in ids:
            cnt = jnp.zeros_like(rows[0])
            for o in ids:
                if o != e:
                    cnt = cnt + beats(rows[o], o, rows[e], e)
            flag = jnp.where(cnt < TOP_K, 1.0, 0.0)
            top.append(flag)
            sc = sc + flag * rows[e]
        gscore.append(sc)
    chosen = []
    for g in range(N_GROUPS):
        cnt = jnp.zeros_like(rows[0])
        for o in range(N_GROUPS):
            if o != g:
                cnt = cnt + beats(gscore[o], o, gscore[g], g)
        chosen.append(jnp.where(cnt < 1.0, 1.0, 0.0))
    wts = [prow[e] * top[e] * chosen[e // GROUP_SIZE] for e in range(N_EXPERTS)]
    den = wts[0]
    for e in range(1, N_EXPERTS):
        den = den + wts[e]
    g4 = []
    for k in range(GROUP_SIZE):
        gk = wts[k]
        for g in range(1, N_GROUPS):
            gk = gk + wts[g * GROUP_SIZE + k]
        g4.append(gk / den)

    rr = hs_ref.shape[1]
    ki = lax.broadcasted_iota(jnp.int32, (tm, tm), 0)
    ji = lax.broadcasted_iota(jnp.int32, (tm, tm), 1)
    before = jnp.where(ki < ji, 1.0, 0.0).astype(BF16)
    chosen4 = jnp.concatenate(chosen, axis=0)
    rank = jnp.dot(chosen4.astype(BF16), before, preferred_element_type=F32)
    lane = lax.broadcasted_iota(jnp.int32, (1, LANES), 1)
    start = jnp.zeros((1, 1), F32)
    dest = jnp.zeros((1, tm), F32)
    cnt_row = jnp.zeros((1, LANES), F32)
    for g in range(N_GROUPS):
        n_g = jnp.sum(chosen[g], axis=1, keepdims=True)
        tiles_g = jnp.floor((n_g + (MOE_TILE - 1)) * (1.0 / MOE_TILE))
        dest = dest + chosen[g] * (start + rank[g:g + 1, :])
        cnt_row = cnt_row + jnp.where(lane == g, tiles_g, 0.0)
        start = start + tiles_g * MOE_TILE
    dest_i = dest.astype(jnp.int32)
    dest_ref[0] = dest_i
    cnt_ref[0] = cnt_row.astype(jnp.int32)
    perm = jnp.where(lax.broadcasted_iota(jnp.int32, (rr, tm), 0) == dest_i, 1.0, 0.0).astype(BF16)
    hs_ref[0] = jnp.dot(perm, h2b, preferred_element_type=F32).astype(BF16)
    g128 = jnp.concatenate(g4 + [jnp.zeros((LANES - GROUP_SIZE, tm), F32)], axis=0)
    ghi = g128.astype(BF16)
    glo = (g128 - ghi.astype(F32)).astype(BF16)
    gs_ref[0] = (lax.dot_general(perm, ghi, nt, preferred_element_type=F32)
                 + lax.dot_general(perm, glo, nt, preferred_element_type=F32))


def outproj(x, hg_out, fox_o, mod, fox_gain, w_out, gain2, wr_hi, wr_lo, b_router, s, ls):
    b, l, d = x.shape
    nblk = (b // s) * (l // ls)
    nl = l // ls
    tm = s * ls
    row = lambda i, j: (i, j, 0)
    const = lambda i, j: (0, 0)
    blk = lambda i, j: (i * nl + j, 0, 0)
    return pl.pallas_call(
        _outproj_kernel,
        out_shape=[jax.ShapeDtypeStruct((b, l, d), F32),
                   jax.ShapeDtypeStruct((nblk, MOE_ROWS, d), BF16),
                   jax.ShapeDtypeStruct((nblk, MOE_ROWS, LANES), F32),
                   jax.ShapeDtypeStruct((nblk, 1, tm), jnp.int32),
                   jax.ShapeDtypeStruct((nblk, 1, LANES), jnp.int32)],
        grid=(b // s, nl),
        in_specs=[pl.BlockSpec((s, ls, d), row),
                  pl.BlockSpec((s, ls, HG_WIDTH), row),
                  pl.BlockSpec((s, ls, FOX_WIDTH), row),
                  pl.BlockSpec((s, N_MOD, d), lambda i, j: (i, 0, 0)),
                  pl.BlockSpec((1, FOX_WIDTH), const),
                  pl.BlockSpec(w_out.shape, const, pipeline_mode=pl.Buffered(1)),
                  pl.BlockSpec((1, d), const),
                  pl.BlockSpec((N_EXPERTS, d), const),
                  pl.BlockSpec((N_EXPERTS, d), const),
                  pl.BlockSpec((N_EXPERTS, 1), const)],
        out_specs=[pl.BlockSpec((s, ls, d), row),
                   pl.BlockSpec((1, MOE_ROWS, d), blk),
                   pl.BlockSpec((1, MOE_ROWS, LANES), blk),
                   pl.BlockSpec((1, 1, tm), blk),
                   pl.BlockSpec((1, 1, LANES), blk)],
        compiler_params=_params(("parallel", "parallel")),
        name="outproj",
    )(x, hg_out, fox_o, mod, fox_gain, w_out, gain2, wr_hi, wr_lo, b_router)


def _moe_tables(cnt, rows_per_step):
    nblk = cnt.shape[0]
    tiles_per_step = rows_per_step // MOE_TILE
    max_tiles = nblk * (BLOCK_TOKENS // MOE_TILE + N_GROUPS)
    n_steps = -(-max_tiles // tiles_per_step) + N_GROUPS
    npair = N_GROUPS * nblk
    start_gm = (jnp.cumsum(cnt, axis=1) - cnt).T.reshape(-1)
    cnt_gm = cnt.T.reshape(-1)
    csum = jnp.cumsum(cnt_gm)
    tau = jnp.arange(max_tiles, dtype=jnp.int32)
    pair = jnp.minimum(jnp.sum((csum[None, :] <= tau[:, None]).astype(jnp.int32), axis=1), npair - 1)
    hot = (pair[:, None] == jnp.arange(npair, dtype=jnp.int32)[None, :]).astype(jnp.int32)
    k = tau - jnp.sum(hot * (csum - cnt_gm)[None, :], axis=1)
    valid = tau < csum[-1]
    tile_blk = jnp.where(valid, pair % nblk, 0).astype(jnp.int32)
    tile_row = jnp.where(valid, (jnp.sum(hot * start_gm[None, :], axis=1) + k) * MOE_TILE, 0).astype(jnp.int32)
    tot = jnp.sum(cnt, axis=0)
    gstart = jnp.cumsum(tot) - tot
    nsteps = (tot + tiles_per_step - 1) // tiles_per_step
    send = jnp.cumsum(nsteps)
    s = jnp.arange(n_steps, dtype=jnp.int32)
    sg = jnp.minimum(jnp.sum((send[None, :] <= s[:, None]).astype(jnp.int32), axis=1), N_GROUPS - 1)
    ghot = (sg[:, None] == jnp.arange(N_GROUPS, dtype=jnp.int32)[None, :]).astype(jnp.int32)
    pick = lambda v: jnp.sum(ghot * v[None, :], axis=1)
    first = pick(gstart) + (s - pick(send - nsteps)) * tiles_per_step
    num = jnp.clip(pick(gstart + tot) - first, 0, tiles_per_step)
    first = jnp.where(num > 0, first, 0)
    return sg.astype(jnp.int32), first.astype(jnp.int32), num.astype(jnp.int32), tile_blk, tile_row


def _moe_kernel(sg_ref, first_ref, num_ref, tblk_ref, trow_ref,
                hs_a, gs_a, hs_b, gs_b, wg_ref, wu_ref, wd_ref, yinit_a, yinit_b, ys_a, ys_b,
                hbuf, gbuf, ybuf, sem_in, sem_out):
    del sg_ref, yinit_a, yinit_b
    s = pl.program_id(0)
    ns = pl.num_programs(0)
    slot = lax.rem(s, 2)
    na = hs_a.shape[0]

    def tile(k):
        return pl.ds(pl.multiple_of(k * MOE_TILE, MOE_TILE), MOE_TILE)

    def src(step, k):
        t = first_ref[step] + k
        return tblk_ref[t], pl.ds(pl.multiple_of(trow_ref[t], MOE_TILE), MOE_TILE)

    class _Either:
        def __init__(self, blk, make):
            self.blk, self.make = blk, make

        def start(self):
            @pl.when(self.blk < na)
            def _():
                self.make(0, self.blk).start()

            @pl.when(self.blk >= na)
            def _():
                self.make(1, self.blk - na).start()

        def wait(self):
            self.make(0, 0).wait()

    def h_copy(step, sl, k):
        blk, rows = src(step, k)
        return _Either(blk, lambda g, b: pltpu.make_async_copy(
            (hs_a, hs_b)[g].at[b, rows, :], hbuf.at[sl, tile(k), :], sem_in.at[sl, 0]))

    def g_copy(step, sl, k):
        blk, rows = src(step, k)
        return _Either(blk, lambda g, b: pltpu.make_async_copy(
            (gs_a, gs_b)[g].at[b, rows, :], gbuf.at[sl, tile(k), :], sem_in.at[sl, 1]))

    def y_copy(step, sl, k):
        blk, rows = src(step, k)
        return _Either(blk, lambda g, b: pltpu.make_async_copy(
            ybuf.at[sl, tile(k), :], (ys_a, ys_b)[g].at[b, rows, :], sem_out.at[sl]))

    def each_tile(step, fn):
        def body(k, c):
            fn(k)
            return c
        lax.fori_loop(0, num_ref[step], body, 0)

    def start_gather(step, sl):
        def fn(k):
            h_copy(step, sl, k).start()
            g_copy(step, sl, k).start()
        each_tile(step, fn)

    def wait_gather(step, sl):
        def fn(k):
            h_copy(step, sl, k).wait()
            g_copy(step, sl, k).wait()
        each_tile(step, fn)

    @pl.when(s == 0)
    def _():
        hbuf[...] = jnp.zeros_like(hbuf)
        gbuf[...] = jnp.zeros_like(gbuf)
        start_gather(0, 0)

    wait_gather(s, slot)

    @pl.when(s + 1 < ns)
    def _():
        start_gather(s + 1, 1 - slot)

    @pl.when(s >= 2)
    def _():
        each_tile(s - 2, lambda k: y_copy(s - 2, slot, k).wait())

    @pl.when(num_ref[s] > 0)
    def _():
        h = hbuf[slot]
        g = gbuf[slot]
        acc = jnp.zeros((h.shape[0], wd_ref.shape[2]), F32)
        for k in range(GROUP_SIZE):
            a = jnp.dot(h, wg_ref[k].astype(BF16), preferred_element_type=F32)
            u = jnp.dot(h, wu_ref[k].astype(BF16), preferred_element_type=F32)
            he = (_silu(a) * u * g[:, k:k + 1]).astype(BF16)
            acc = acc + jnp.dot(he, wd_ref[k].astype(BF16), preferred_element_type=F32)
        ybuf[slot] = acc.astype(ybuf.dtype)

    each_tile(s, lambda k: y_copy(s, slot, k).start())

    @pl.when(s == ns - 1)
    def _():
        each_tile(s, lambda k: y_copy(s, slot, k).wait())

        @pl.when(s >= 1)
        def _():
            each_tile(s - 1, lambda k: y_copy(s - 1, 1 - slot, k).wait())


def moe_sparse(hs_a, gs_a, hs_b, gs_b, cnt, wg, wu, wd, layer):
    _, rr, d = hs_a.shape
    de = wg.shape[2]
    rows = MOE_STEP_ROWS
    sg, first, num, tile_blk, tile_row = _moe_tables(cnt, rows)
    n_steps = sg.shape[0]
    wmap = lambda s, sg, *_: (layer * N_GROUPS + sg[s], 0, 0)
    grid_spec = pltpu.PrefetchScalarGridSpec(
        num_scalar_prefetch=5,
        grid=(n_steps,),
        in_specs=[pl.BlockSpec(memory_space=pl.ANY)] * 4
        + [pl.BlockSpec((GROUP_SIZE, d, de), wmap, pipeline_mode=pl.Buffered(1)),
           pl.BlockSpec((GROUP_SIZE, d, de), wmap, pipeline_mode=pl.Buffered(1)),
           pl.BlockSpec((GROUP_SIZE, de, d), wmap, pipeline_mode=pl.Buffered(1))]
        + [pl.BlockSpec(memory_space=pl.ANY)] * 2,
        out_specs=[pl.BlockSpec(memory_space=pl.ANY)] * 2,
        scratch_shapes=[pltpu.VMEM((2, rows, d), BF16),
                        pltpu.VMEM((2, rows, LANES), F32),
                        pltpu.VMEM((2, rows, d), BF16),
                        pltpu.SemaphoreType.DMA((2, 2)),
                        pltpu.SemaphoreType.DMA((2,))])
    return pl.pallas_call(
        _moe_kernel,
        out_shape=[jax.ShapeDtypeStruct(hs_a.shape, BF16), jax.ShapeDtypeStruct(hs_b.shape, BF16)],
        grid_spec=grid_spec,
        input_output_aliases={12: 0, 13: 1},
        compiler_params=_params(("arbitrary",)),
        name="moe",
    )(sg, first, num, tile_blk, tile_row, hs_a, gs_a, hs_b, gs_b, wg, wu, wd,
      jnp.zeros(hs_a.shape, BF16), jnp.zeros(hs_b.shape, BF16))


def _combine_kernel(x_ref, ys_ref, dest_ref, mod_ref, fgain_ref, o_ref, *, final):
    s, ls, d = x_ref.shape
    tm = s * ls
    rr = ys_ref.shape[1]
    dcol = jnp.broadcast_to(dest_ref[0].astype(F32), (LANES, tm)).T
    lane = lax.broadcasted_iota(jnp.int32, (tm, LANES), 1).astype(F32)
    unperm = jnp.concatenate(
        [jnp.where(dcol == lane + float(c * LANES), 1.0, 0.0).astype(BF16) for c in range(rr // LANES)],
        axis=1)
    y = jnp.dot(unperm, ys_ref[0], preferred_element_type=F32)
    x = x_ref[...] + mod_ref[:, 5:6, :] * y.reshape(s, ls, d)
    if final:
        x = x * lax.rsqrt(jnp.mean(x * x, axis=-1, keepdims=True) + EPS) * fgain_ref[...]
    o_ref[...] = x


def combine(x, ys, dest, mod, final_gain, s, ls, final):
    b, l, d = x.shape
    nl = l // ls
    tm = s * ls
    row = lambda i, j: (i, j, 0)
    blk = lambda i, j: (i * nl + j, 0, 0)
    return pl.pallas_call(
        functools.partial(_combine_kernel, final=final),
        out_shape=jax.ShapeDtypeStruct((b, l, d), F32),
        grid=(b // s, nl),
        in_specs=[pl.BlockSpec((s, ls, d), row),
                  pl.BlockSpec((1, ys.shape[1], d), blk),
                  pl.BlockSpec((1, 1, tm), blk),
                  pl.BlockSpec((s, N_MOD, d), lambda i, j: (i, 0, 0)),
                  pl.BlockSpec((1, d), lambda i, j: (0, 0))],
        out_specs=pl.BlockSpec((s, ls, d), row),
        compiler_params=_params(("parallel", "parallel")),
        name="combine",
    )(x, ys, dest, mod, final_gain)


def _tile(b, l):
    ls = min(l, BLOCK_TOKENS)
    s = BLOCK_TOKENS // ls
    assert s * ls == BLOCK_TOKENS and b % s == 0 and l % ls == 0
    assert BLOCK_TOKENS + N_GROUPS * MOE_TILE <= MOE_ROWS
    return s, ls


def _mixer(x, mod, l, p, hg_state0, fox_past, kv_prev):
    depth = p['norm_mix_gain'].shape[0]
    b, seq, d = x.shape
    s, ls = _tile(b, seq)
    sample = fox_past is not None
    hg_in, fq, kbuf, vbuf, *k16v16, ff = inproj(x, mod, p['norm_mix_gain'][l], p['w_in'][l], s, ls,
                                                l, depth, kv_prev, sample)

    hg_out, hg_state = hgrn(hg_in, hg_state0, p['hg_lower_bounds'], p['hg_norm_gain'][l], l)

    if not sample:
        tq = min(seq, BLOCK_TOKENS)
        hg_out, fq = lax.optimization_barrier((hg_out, fq))
        qp, kp, vt, lf_pad = foxpack(fq, kbuf, vbuf, l, ff, p['fox_f_bias_pad'][l], tq)
        fox_o = fox_prompt(qp, kp, vt, tq)
        logf = lf_pad[:, :, :FOX_HEADS]
    else:
        kcache, vcache, plogf = fox_past
        past = kcache.shape[4]
        fft = jnp.swapaxes(ff[:, :, :FOX_HEADS], 1, 2)
        logft, _ = forget_cumsum(fft, p['fox_f_bias'][l], True, seq)
        tot = past + seq
        pad = (-tot) % LANES
        allt = jnp.concatenate([jnp.swapaxes(plogf[l], 1, 2), logft,
                                jnp.zeros((b, FOX_HEADS, pad), F32)], axis=2)
        _, ct = forget_cumsum(allt, p['fox_f_bias'][l], False, LANES)
        ct = ct * LOG2E
        cq = jnp.swapaxes(ct[:, :, past:tot], 1, 2)
        fox_o = fox_sample(fq, kcache, vcache, l, k16v16[0], k16v16[1], cq, ct)
        logf = jnp.swapaxes(logft, 1, 2)

    routed = outproj(x, hg_out, fox_o, mod, p['fox_out_gain'][l], p['w_out'][l],
                     p['norm_ffn_gain'][l], p['wr_hi'], p['wr_lo'], p['b_router'], s, ls)
    return routed, (kbuf, vbuf), hg_state, logf


def _ffn(routed_p, routed_s, mod_p, mod_s, l, p, final):
    xm_p, hs_p, gs_p, dest_p, cnt_p = routed_p
    xm_s, hs_s, gs_s, dest_s, cnt_s = routed_s
    cnt = jnp.concatenate([cnt_p, cnt_s], axis=0)[:, 0, :N_GROUPS]
    ys_p, ys_s = moe_sparse(hs_p, gs_p, hs_s, gs_s, cnt, p['w_exp_gate'], p['w_exp_up'], p['w_exp_down'], l)
    outs = []
    for xm, ys, dest, mod in ((xm_p, ys_p, dest_p, mod_p), (xm_s, ys_s, dest_s, mod_s)):
        s, ls = _tile(xm.shape[0], xm.shape[1])
        outs.append(combine(xm, ys, dest, mod, p['final_norm_gain'], s, ls, final))
    return outs


def kernel(x_prompt, x_sample, cache_fox_k, cache_fox_v, cache_fox_logf, state_hgrn, c_prompt, c_sample,
           norm_mix_gain, norm_ffn_gain, w_ada, b_ada, w_in, hg_lower_bounds, hg_norm_gain,
           fox_f_bias, fox_out_gain, w_out, w_router, b_router, w_exp_gate, w_exp_up, w_exp_down,
           final_norm_gain):
    depth, d = norm_mix_gain.shape
    bp = x_prompt.shape[0]
    n_in = w_in.shape[2]
    n_pad = 4 * HG_WIDTH + 3 * FOX_WIDTH + LANES - n_in
    wr_t = w_router.T
    wr_hi = wr_t.astype(BF16)
    p = {
        'norm_mix_gain': norm_mix_gain.reshape(depth, 1, d),
        'norm_ffn_gain': norm_ffn_gain.reshape(depth, 1, d),
        'w_in': jnp.pad(w_in, ((0, 0), (0, 0), (0, n_pad))).astype(BF16),
        'hg_lower_bounds': hg_lower_bounds,
        'hg_norm_gain': hg_norm_gain.reshape(depth, 1, HG_WIDTH),
        'fox_f_bias': fox_f_bias.reshape(depth, FOX_HEADS, 1),
        'fox_f_bias_pad': jnp.pad(fox_f_bias, ((0, 0), (0, LANES - FOX_HEADS))).reshape(depth, 1, LANES),
        'fox_out_gain': fox_out_gain.reshape(depth, 1, FOX_WIDTH),
        'w_out': w_out.astype(BF16),
        'wr_hi': wr_hi,
        'wr_lo': (wr_t - wr_hi.astype(F32)).astype(BF16),
        'b_router': b_router.reshape(N_EXPERTS, 1),
        'w_exp_gate': w_exp_gate.reshape((depth * N_EXPERTS,) + w_exp_gate.shape[2:]),
        'w_exp_up': w_exp_up.reshape((depth * N_EXPERTS,) + w_exp_up.shape[2:]),
        'w_exp_down': w_exp_down.reshape((depth * N_EXPERTS,) + w_exp_down.shape[2:]),
        'final_norm_gain': final_norm_gain.reshape(1, d),
    }
    mods = ada_mod(jnp.concatenate([c_prompt, c_sample], axis=0), w_ada, b_ada)
    mods = mods.reshape(depth, -1, N_MOD, d)
    bs, lsq = x_sample.shape[:2]
    lp = x_prompt.shape[1]
    past = cache_fox_k.shape[2]
    fox_past = (jnp.transpose(cache_fox_k, (0, 1, 3, 4, 2)), jnp.transpose(cache_fox_v, (0, 1, 3, 4, 2)),
                cache_fox_logf)

    xp, xs = x_prompt, x_sample
    kv_p = kv_s = None
    st_p, st_s, lf_p, lf_s = [], [], [], []
    zero_state = jnp.zeros((bp, HG_HEADS, HG_DIM, HG_DIM), F32)
    for l in range(depth):
        routed_p, kv_p, st, lf = _mixer(xp, mods[l, :bp], l, p, zero_state, None, kv_p)
        st_p.append(st)
        lf_p.append(lf)
        routed_s, kv_s, st, lf = _mixer(xs, mods[l, bp:], l, p, state_hgrn[l], fox_past, kv_s)
        st_s.append(st)
        lf_s.append(lf)
        xp, xs = _ffn(routed_p, routed_s, mods[l, :bp], mods[l, bp:], l, p, l == depth - 1)
    heads = lambda a, b, l: a.reshape(depth, b, l, FOX_HEADS, FOX_DIM)
    return (xp, xs,
            jnp.stack(st_p), heads(kv_p[0], bp, lp), heads(kv_p[1], bp, lp), jnp.stack(lf_p),
            jnp.stack(st_s), heads(kv_s[0], bs, lsq), heads(kv_s[1], bs, lsq), jnp.stack(lf_s))
```

```python
import functools
import math

import numpy as np
import jax
import jax.numpy as jnp
from jax import lax
from jax.experimental import pallas as pl
from jax.experimental.pallas import tpu as pltpu

HG_HEADS = 4
HG_DIM = 128
HG_WIDTH = HG_HEADS * HG_DIM
FOX_HEADS = 8
FOX_DIM = 64
FOX_WIDTH = FOX_HEADS * FOX_DIM
N_EXPERTS = 16
N_GROUPS = 4
GROUP_SIZE = N_EXPERTS // N_GROUPS
TOP_K = 2
N_MOD = 6
EPS = 1e-6

LANES = 128
HG_CHUNK = 128
HG_SUB = 16
HG_STREAMS = 4
MOE_TILE = 16
BLOCK_TOKENS = 512
MOE_ROWS = 640
MOE_STEP_ROWS = 1024
VMEM_LIMIT = 56 * 1024 * 1024

F32 = jnp.float32
BF16 = jnp.bfloat16
NEG = -1e30
LOG2E = math.log2(math.e)


def _params(sem, vmem=VMEM_LIMIT):
    return pltpu.CompilerParams(dimension_semantics=sem, vmem_limit_bytes=vmem)


def _split3(x):
    hi = x.astype(BF16)
    r1 = x - hi.astype(F32)
    mid = r1.astype(BF16)
    lo = (r1 - mid.astype(F32)).astype(BF16)
    return hi, mid, lo


def _sigmoid(x):
    return 1.0 / (1.0 + jnp.exp(-x))


def _log_sigmoid(x):
    return jnp.minimum(x, 0.0) - jnp.log(1.0 + jnp.exp(-jnp.abs(x)))


def _silu(x):
    return x * _sigmoid(x)


def _ada_kernel(c_ref, w_ref, b_ref, o_ref):
    s = _silu(c_ref[...]).astype(BF16)
    o_ref[0] = jnp.dot(s, w_ref[0].astype(BF16), preferred_element_type=F32) + b_ref[0]


def ada_mod(c, w_ada, b_ada):
    depth, d, n = w_ada.shape
    nb = c.shape[0]
    tn = 1536
    return pl.pallas_call(
        _ada_kernel,
        out_shape=jax.ShapeDtypeStruct((depth, nb, n), F32),
        grid=(depth, n // tn),
        in_specs=[pl.BlockSpec((nb, d), lambda l, j: (0, 0)),
                  pl.BlockSpec((1, d, tn), lambda l, j: (l, 0, j)),
                  pl.BlockSpec((1, 1, tn), lambda l, j: (l, 0, j))],
        out_specs=pl.BlockSpec((1, nb, tn), lambda l, j: (l, 0, j)),
        compiler_params=_params(("arbitrary", "arbitrary")),
        name="ada_mod",
    )(c, w_ada, b_ada.reshape(depth, 1, n))


def _inproj_kernel(x_ref, mod_ref, gain_ref, w_ref, *refs, has_prev, emit16):
    outs = refs[2:] if has_prev else refs
    hg_ref, fq_ref, fk32_ref, fv32_ref = outs[:4]
    ff_ref = outs[-1]
    s, ls, d = x_ref.shape
    x = x_ref[...]
    y = x * lax.rsqrt(jnp.mean(x * x, axis=-1, keepdims=True) + EPS) * gain_ref[...]
    h = y * (1.0 + mod_ref[:, 1:2, :]) + mod_ref[:, 0:1, :]
    hb = h.reshape(s * ls, d).astype(BF16)

    def proj(lo, hi):
        return jnp.dot(hb, w_ref[:, lo:hi], preferred_element_type=F32)

    c0 = 4 * HG_WIDTH
    hg_ref[...] = proj(0, c0).reshape(s, ls, c0)
    fq = proj(c0, c0 + FOX_WIDTH) * (FOX_DIM ** -0.5 * LOG2E)
    fq_ref[...] = fq.reshape(s, ls, FOX_WIDTH).astype(BF16)
    fk = proj(c0 + FOX_WIDTH, c0 + 2 * FOX_WIDTH).reshape(s, ls, FOX_WIDTH)
    fv = proj(c0 + 2 * FOX_WIDTH, c0 + 3 * FOX_WIDTH).reshape(s, ls, FOX_WIDTH)
    for slot in range(fk32_ref.shape[0]):
        fk32_ref[slot] = fk
        fv32_ref[slot] = fv
    if emit16:
        outs[4][...] = fk.astype(BF16)
        outs[5][...] = fv.astype(BF16)
    ff_ref[...] = proj(c0 + 3 * FOX_WIDTH, c0 + 3 * FOX_WIDTH + LANES).reshape(s, ls, LANES)


def inproj(x, mod, gain, w_pad, s, ls, layer, depth, kv_prev, emit16):
    b, l, d = x.shape
    n = w_pad.shape[1]
    row = lambda i, j: (i, j, 0)
    lrow = lambda i, j: (layer, i, j, 0)
    shapes = [jax.ShapeDtypeStruct((b, l, 4 * HG_WIDTH), F32),
              jax.ShapeDtypeStruct((b, l, FOX_WIDTH), BF16),
              jax.ShapeDtypeStruct((depth, b, l, FOX_WIDTH), F32),
              jax.ShapeDtypeStruct((depth, b, l, FOX_WIDTH), F32)]
    slots = depth if kv_prev is None else 1
    if kv_prev is None:
        assert layer == 0
    specs = [pl.BlockSpec((s, ls, 4 * HG_WIDTH), row),
             pl.BlockSpec((s, ls, FOX_WIDTH), row),
             pl.BlockSpec((slots, s, ls, FOX_WIDTH), lrow),
             pl.BlockSpec((slots, s, ls, FOX_WIDTH), lrow)]
    if emit16:
        shapes += [jax.ShapeDtypeStruct((b, l, FOX_WIDTH), BF16)] * 2
        specs += [pl.BlockSpec((s, ls, FOX_WIDTH), row)] * 2
    shapes.append(jax.ShapeDtypeStruct((b, l, LANES), F32))
    specs.append(pl.BlockSpec((s, ls, LANES), row))
    in_specs = [pl.BlockSpec((s, ls, d), row),
                pl.BlockSpec((s, N_MOD, d), lambda i, j: (i, 0, 0)),
                pl.BlockSpec((1, d), lambda i, j: (0, 0)),
                pl.BlockSpec((d, n), lambda i, j: (0, 0), pipeline_mode=pl.Buffered(1))]
    args = [x, mod, gain, w_pad]
    aliases = {}
    if kv_prev is not None:
        in_specs += [pl.BlockSpec(memory_space=pl.ANY)] * 2
        args += list(kv_prev)
        aliases = {4: 2, 5: 3}
    return pl.pallas_call(
        functools.partial(_inproj_kernel, has_prev=kv_prev is not None, emit16=emit16),
        out_shape=shapes,
        grid=(b // s, l // ls),
        in_specs=in_specs,
        out_specs=specs,
        input_output_aliases=aliases,
        compiler_params=_params(("parallel", "parallel")),
        name="inproj",
    )(*args)


def _cumsum_kernel(x_ref, bias_ref, lf_ref, c_ref, carry_ref, *, apply_ls):
    @pl.when(pl.program_id(1) == 0)
    def _():
        carry_ref[...] = jnp.zeros_like(carry_ref)

    bb, nh, tc = x_ref.shape
    x = x_ref[...].reshape(bb * nh, tc)
    lf = _log_sigmoid(x + bias_ref[...]) if apply_ls else x
    lf_ref[...] = lf.reshape(bb, nh, tc)
    r = lax.broadcasted_iota(jnp.int32, (tc, tc), 0)
    c = lax.broadcasted_iota(jnp.int32, (tc, tc), 1)
    tri = jnp.where(r <= c, 1.0, 0.0).astype(BF16)
    hi, mid, lo = _split3(lf)
    tot = (jnp.dot(hi, tri, preferred_element_type=F32)
           + jnp.dot(mid, tri, preferred_element_type=F32)
           + jnp.dot(lo, tri, preferred_element_type=F32)) + carry_ref[:, 0:1]
    c_ref[...] = tot.reshape(bb, nh, tc)
    carry_ref[...] = jnp.broadcast_to(tot[:, tc - 1:tc], carry_ref.shape)


def forget_cumsum(xt, bias, apply_ls, tc):
    b, h, l = xt.shape
    blk = pl.BlockSpec((b, h, tc), lambda i, j: (0, 0, j))
    return pl.pallas_call(
        functools.partial(_cumsum_kernel, apply_ls=apply_ls),
        out_shape=[jax.ShapeDtypeStruct((b, h, l), F32)] * 2,
        grid=(1, l // tc),
        in_specs=[blk, pl.BlockSpec((b * h, 1), lambda i, j: (0, 0))],
        out_specs=[blk, blk],
        scratch_shapes=[pltpu.VMEM((b * h, LANES), F32)],
        compiler_params=_params(("arbitrary", "arbitrary")),
        name="forget_cumsum",
    )(xt, jnp.tile(bias, (b, 1)))


HG_HALVES = (8, 4, 2, 1)


def _hgrn_sum_matrix():
    c = HG_CHUNK
    t = np.arange(c)[:, None]
    u = np.arange(c)[None, :]
    mats = [(u <= t)]
    for w in HG_HALVES:
        pos = t % (2 * w)
        mid = t - pos + w - 1
        upper = pos >= w
        mats.append(np.where(upper, (u > mid) & (u <= t), (u > t) & (u <= mid)))
    return jnp.asarray(np.concatenate(mats, axis=0).astype(np.float32), BF16)


def _hgrn_kernel(hin_ref, s0_ref, lbp_ref, gain_ref, sums_ref, out_ref, sfin_ref, st_ref, *, layer):
    t = pl.program_id(1)
    nt = pl.num_programs(1)
    c = HG_CHUNK
    nsub = c // HG_SUB
    nb = hin_ref.shape[0]
    rows_in = hin_ref.shape[1]

    def pad_rows(a):
        if rows_in == c:
            return a
        return jnp.concatenate([a, jnp.zeros((c - rows_in, a.shape[1]), a.dtype)], axis=0)

    @pl.when(t == 0)
    def _():
        for bi in range(nb):
            for h in range(HG_HEADS):
                st_ref[bi, h] = s0_ref[bi, h].T

    lbp = lbp_ref[...]
    e = jnp.exp(lbp - jnp.max(lbp, axis=0, keepdims=True))
    p = e / jnp.sum(e, axis=0, keepdims=True)
    acc = p[0:1]
    first = acc
    for i in range(1, layer + 1):
        acc = acc + p[i:i + 1]
    lb_all = acc - first

    ri = lax.broadcasted_iota(jnp.int32, (c, c), 0)
    ci = lax.broadcasted_iota(jnp.int32, (c, c), 1)
    ones = jnp.ones((HG_DIM, HG_DIM), BF16)
    nt_dims = (((1,), (1,)), ((), ()))

    streams = [(bi, h) for bi in range(nb) for h in range(HG_HEADS)]
    q, kk, v, b, sums, att = {}, {}, {}, {}, {}, {}
    for bi, h in streams:
        lb = lb_all[:, h * HG_DIM:(h + 1) * HG_DIM]
        hq = hin_ref[bi, :, h * HG_DIM:(h + 1) * HG_DIM]
        hf = hin_ref[bi, :, HG_WIDTH + h * HG_DIM:HG_WIDTH + (h + 1) * HG_DIM]
        hi = hin_ref[bi, :, 2 * HG_WIDTH + h * HG_DIM:2 * HG_WIDTH + (h + 1) * HG_DIM]
        q[bi, h] = pad_rows(_silu(hq) * (HG_DIM ** -0.5))
        a1 = jnp.log(lb)
        e = jnp.exp(-jnp.abs(hf))
        den = 1.0 + e
        a2 = jnp.log1p(-lb) + (jnp.minimum(hf, 0.0) - jnp.log(den))
        mx = jnp.maximum(a1, a2)
        lf = pad_rows(mx + jnp.log(jnp.exp(a1 - mx) + jnp.exp(a2 - mx)))
        kk[bi, h] = pad_rows((1.0 - lb) * (jnp.where(hf > 0.0, e, 1.0) / den))
        v[bi, h] = pad_rows(hi)
        lhi, lmid, llo = _split3(lf)
        sums[bi, h] = (jnp.dot(sums_ref[...], lhi, preferred_element_type=F32)
                       + jnp.dot(sums_ref[...], lmid, preferred_element_type=F32)
                       + jnp.dot(sums_ref[...], llo, preferred_element_type=F32))
        b[bi, h] = sums[bi, h][0:c]

    for key in streams:
        att[key] = jnp.where(ri == ci, jnp.dot((q[key] * kk[key]).astype(BF16), ones,
                                               preferred_element_type=F32), 0.0)
    for lvl, w in enumerate(HG_HALVES):
        shift = (2 * w).bit_length() - 1
        pair = ((ri >> shift) == (ci >> shift)) & ((ri & (2 * w - 1)) >= w) & ((ci & (2 * w - 1)) < w)
        for key in streams:
            dec = jnp.exp(sums[key][(lvl + 1) * c:(lvl + 2) * c])
            sc = lax.dot_general((q[key] * dec).astype(BF16), (kk[key] * dec).astype(BF16), nt_dims,
                                 preferred_element_type=F32)
            att[key] = att[key] + jnp.where(pair, sc, 0.0)

    khat = {key: None for key in streams}
    r_prev = {key: jnp.zeros((1, HG_DIM), F32) for key in streams}
    qhat_blocks = {key: [] for key in streams}
    cross = {key: [] for key in streams}
    for i in range(nsub):
        r0 = i * HG_SUB
        for key in streams:
            bb = b[key][r0:r0 + HG_SUB]
            r_next = b[key][r0 + HG_SUB - 1:r0 + HG_SUB]
            qt = q[key][r0:r0 + HG_SUB] * jnp.exp(bb - r_prev[key])
            if khat[key] is None:
                cross[key].append(jnp.zeros((HG_SUB, c), F32))
            else:
                kfull = jnp.concatenate([khat[key], jnp.zeros((c - r0, HG_DIM), F32)], axis=0)
                cross[key].append(lax.dot_general(qt.astype(BF16), kfull.astype(BF16), nt_dims,
                                                  preferred_element_type=F32))
            qhat_blocks[key].append(qt * jnp.exp(r_prev[key]))
            kt = kk[key][r0:r0 + HG_SUB] * jnp.exp(r_next - bb)
            if khat[key] is None:
                khat[key] = kt
            else:
                khat[key] = jnp.concatenate([khat[key] * jnp.exp(r_next - r_prev[key]), kt], axis=0)
            r_prev[key] = r_next

    for bi, h in streams:
        key = (bi, h)
        cols = slice(h * HG_DIM, (h + 1) * HG_DIM)
        st = st_ref[bi, h]
        a = att[key] + jnp.concatenate(cross[key], axis=0)
        qhat = jnp.concatenate(qhat_blocks[key], axis=0).astype(BF16)
        o = jnp.dot(a.astype(BF16), v[key].astype(BF16), preferred_element_type=F32)
        o = o + lax.dot_general(qhat, st.astype(BF16), nt_dims, preferred_element_type=F32)
        st_ref[bi, h] = st * jnp.exp(r_prev[key]) + jnp.dot(
            v[key].T.astype(BF16), khat[key].astype(BF16), preferred_element_type=F32)

        hg = hin_ref[bi, :, 3 * HG_WIDTH + h * HG_DIM:3 * HG_WIDTH + (h + 1) * HG_DIM]
        o = o[:rows_in]
        o = o * lax.rsqrt(jnp.mean(o * o, axis=-1, keepdims=True) + EPS)
        o = o * gain_ref[:, cols] * _silu(hg)
        out_ref[bi, :, cols] = o.astype(out_ref.dtype)

    @pl.when(t == nt - 1)
    def _():
        for bi, h in streams:
            sfin_ref[bi, h] = st_ref[bi, h].T


def hgrn(hin, s0, lbp, gain, layer):
    b, l, w = hin.shape
    c = min(HG_CHUNK, l)
    assert l % c == 0 and c % HG_SUB == 0
    nt = l // c
    nb = HG_STREAMS
    assert b % nb == 0
    sums = _hgrn_sum_matrix()
    return pl.pallas_call(
        functools.partial(_hgrn_kernel, layer=layer),
        out_shape=[jax.ShapeDtypeStruct((b, l, HG_WIDTH), BF16),
                   jax.ShapeDtypeStruct((b, HG_HEADS, HG_DIM, HG_DIM), F32)],
        grid=(b // nb, nt),
        in_specs=[pl.BlockSpec((nb, c, w), lambda i, j: (i, j, 0)),
                  pl.BlockSpec((nb, HG_HEADS, HG_DIM, HG_DIM), lambda i, j: (i, 0, 0, 0)),
                  pl.BlockSpec(lbp.shape, lambda i, j: (0, 0)),
                  pl.BlockSpec((1, HG_WIDTH), lambda i, j: (0, 0)),
                  pl.BlockSpec(sums.shape, lambda i, j: (0, 0))],
        out_specs=[pl.BlockSpec((nb, c, HG_WIDTH), lambda i, j: (i, j, 0)),
                   pl.BlockSpec((nb, HG_HEADS, HG_DIM, HG_DIM), lambda i, j: (i, 0, 0, 0))],
        scratch_shapes=[pltpu.VMEM((nb, HG_HEADS, HG_DIM, HG_DIM), F32)],
        compiler_params=_params(("parallel", "arbitrary")),
        name="hgrn",
    )(hin, s0, lbp, gain, sums)


AUG_ONE = FOX_DIM
AUG_NEG = FOX_DIM + 3
ONE_LANE = FOX_HEADS
FOX_GROUP = 4
FOX_STREAMS = 2
FOX_VROWS = FOX_DIM + 16


def _placement():
    pk = np.zeros((3 * LANES, FOX_HEADS * LANES), np.float32)
    pq = np.zeros((3 * LANES, FOX_HEADS * LANES), np.float32)
    for h in range(FOX_HEADS):
        for part in range(3):
            pk[part * LANES + h, h * LANES + AUG_NEG + part] = -1.0
            pq[part * LANES + h, h * LANES + AUG_ONE + part] = 1.0
            pk[ONE_LANE, h * LANES + AUG_ONE + part] = 1.0
            pq[ONE_LANE, h * LANES + AUG_NEG + part] = 1.0
    return jnp.asarray(pk, BF16), jnp.asarray(pq, BF16)


def _foxpack_kernel(q_ref, k_ref, v_ref, ff_ref, bias_ref, pk_ref, pq_ref,
                    qp_ref, kp_ref, vt_ref, lf_ref, carry_ref):
    @pl.when(pl.program_id(1) == 0)
    def _():
        carry_ref[...] = jnp.zeros_like(carry_ref)

    tm = q_ref.shape[1]
    lane = lax.broadcasted_iota(jnp.int32, (tm, LANES), 1)
    lf = jnp.where(lane < FOX_HEADS, _log_sigmoid(ff_ref[0] + bias_ref[...]), 0.0)
    lf_ref[0] = lf
    ri = lax.broadcasted_iota(jnp.int32, (tm, tm), 0)
    ci = lax.broadcasted_iota(jnp.int32, (tm, tm), 1)
    tril = jnp.where(ci <= ri, 1.0, 0.0).astype(BF16)
    hi, mid, lo = _split3(lf)
    c = (jnp.dot(tril, hi, preferred_element_type=F32)
         + jnp.dot(tril, mid, preferred_element_type=F32)
         + jnp.dot(tril, lo, preferred_element_type=F32)) + carry_ref[0:1, :]
    carry_ref[...] = jnp.broadcast_to(c[tm - 1:tm, :], carry_ref.shape)

    chi, cmid, clo = _split3(c * LOG2E)
    chi = jnp.where(lane == ONE_LANE, 1.0, chi.astype(F32)).astype(BF16)
    cterms = jnp.concatenate([chi, cmid, clo], axis=1)
    augk = jnp.dot(cterms, pk_ref[...], preferred_element_type=F32)
    augq = jnp.dot(cterms, pq_ref[...], preferred_element_type=F32)
    for h in range(FOX_HEADS):
        blk = slice((h // 2) * LANES, (h // 2 + 1) * LANES)
        hcols = slice(h * LANES, (h + 1) * LANES)
        kh = k_ref[0, 0, :, blk]
        qh = q_ref[0, :, blk].astype(F32)
        if h % 2:
            kh = pltpu.roll(kh, FOX_DIM, axis=1)
            qh = pltpu.roll(qh, FOX_DIM, axis=1)
        kp_ref[0, h] = jnp.where(lane < FOX_DIM, kh, augk[:, hcols]).astype(BF16)
        qp_ref[0, h] = jnp.where(lane < FOX_DIM, qh, augq[:, hcols]).astype(BF16)
    vt = v_ref[0, 0].T
    tail = jnp.where(lax.broadcasted_iota(jnp.int32, (FOX_VROWS - FOX_DIM, tm), 0) == 0, 1.0, 0.0)
    for h in range(FOX_HEADS):
        vt_ref[0, h] = jnp.concatenate([vt[h * FOX_DIM:(h + 1) * FOX_DIM], tail], axis=0).astype(BF16)


def foxpack(fq, kbuf, vbuf, layer, ff, bias_pad, tm):
    b, l, w = fq.shape
    pk, pq = _placement()
    row = lambda i, j: (i, j, 0)
    lrow = lambda i, j: (layer, i, j, 0)
    const = lambda i, j: (0, 0)
    head_blk = pl.BlockSpec((1, FOX_HEADS, tm, LANES), lambda i, j: (i, 0, j, 0))
    return pl.pallas_call(
        _foxpack_kernel,
        out_shape=[jax.ShapeDtypeStruct((b, FOX_HEADS, l, LANES), BF16),
                   jax.ShapeDtypeStruct((b, FOX_HEADS, l, LANES), BF16),
                   jax.ShapeDtypeStruct((b, FOX_HEADS, FOX_VROWS, l), BF16),
                   jax.ShapeDtypeStruct((b, l, LANES), F32)],
        grid=(b, l // tm),
        in_specs=[pl.BlockSpec((1, tm, w), row),
                  pl.BlockSpec((1, 1, tm, w), lrow),
                  pl.BlockSpec((1, 1, tm, w), lrow),
                  pl.BlockSpec((1, tm, LANES), row),
                  pl.BlockSpec((1, LANES), const),
                  pl.BlockSpec(pk.shape, const),
                  pl.BlockSpec(pq.shape, const)],
        out_specs=[head_blk, head_blk,
                   pl.BlockSpec((1, FOX_HEADS, FOX_VROWS, tm), lambda i, j: (i, 0, 0, j)),
                   pl.BlockSpec((1, tm, LANES), row)],
        scratch_shapes=[pltpu.VMEM((8, LANES), F32)],
        compiler_params=_params(("parallel", "arbitrary")),
        name="foxpack",
    )(fq, kbuf, vbuf, ff, bias_pad, pk, pq)


def _fox_prompt_kernel(qi_ref, kj_ref, q_ref, k_ref, vt_ref, o_ref, m_ref, acc_ref):
    t = pl.program_id(1)
    i = qi_ref[t]
    j = kj_ref[t]
    nb = q_ref.shape[0]
    tq = q_ref.shape[2]
    tk = k_ref.shape[2]
    assert tq == tk
    nt = (((1,), (1,)), ((), ()))

    @pl.when(j == 0)
    def _():
        m_ref[...] = jnp.full_like(m_ref, NEG)
        acc_ref[...] = jnp.zeros_like(acc_ref)

    def step(diagonal):
        if diagonal:
            keep = (lax.broadcasted_iota(jnp.int32, (tk, tq), 0)
                    <= lax.broadcasted_iota(jnp.int32, (tk, tq), 1))
        items = [(bi, h) for bi in range(nb) for h in range(FOX_HEADS)]
        for g0 in range(0, len(items), FOX_GROUP):
            group = items[g0:g0 + FOX_GROUP]
            st = {(bi, h): lax.dot_general(k_ref[bi, h], q_ref[bi, h], nt, preferred_element_type=F32)
                  for bi, h in group}
            if diagonal:
                st = {it: jnp.where(keep, s, NEG) for it, s in st.items()}
            m_old = {(bi, h): m_ref[bi, h:h + 1, :] for bi, h in group}
            m_new = {it: jnp.maximum(m_old[it], jnp.max(st[it], axis=0, keepdims=True)) for it in group}
            p = {it: jnp.exp2((st[it] - m_new[it]).astype(BF16)) for it in group}
            for bi, h in group:
                alpha = jnp.exp2(m_old[bi, h] - m_new[bi, h])
                acc_ref[bi, h] = alpha * acc_ref[bi, h] + jnp.dot(vt_ref[bi, h], p[bi, h],
                                                                  preferred_element_type=F32)
                m_ref[bi, h:h + 1, :] = m_new[bi, h]

    @pl.when(j < i)
    def _():
        step(False)

    @pl.when(j == i)
    def _():
        step(True)
        for bi in range(nb):
            outs = [acc_ref[bi, h, 0:FOX_DIM, :] / acc_ref[bi, h, FOX_DIM:FOX_DIM + 1, :]
                    for h in range(FOX_HEADS)]
            o_ref[bi] = jnp.concatenate(outs, axis=0).T.astype(o_ref.dtype)


def fox_prompt(qp, kp, vt, tq):
    b, nh, l, _ = qp.shape
    w = nh * FOX_DIM
    nq = l // tq
    tk = tq
    assert nq * tq == l
    qi = np.asarray([i for i in range(nq) for _ in range(i + 1)], np.int32)
    kj = np.asarray([j for i in range(nq) for j in range(i + 1)], np.int32)
    nb = FOX_STREAMS
    assert b % nb == 0
    grid_spec = pltpu.PrefetchScalarGridSpec(
        num_scalar_prefetch=2,
        grid=(b // nb, len(qi)),
        in_specs=[pl.BlockSpec((nb, nh, tq, LANES), lambda bi, t, qi, kj: (bi, 0, qi[t], 0)),
                  pl.BlockSpec((nb, nh, tk, LANES), lambda bi, t, qi, kj: (bi, 0, kj[t], 0)),
                  pl.BlockSpec((nb, nh, FOX_VROWS, tk), lambda bi, t, qi, kj: (bi, 0, 0, kj[t]))],
        out_specs=pl.BlockSpec((nb, tq, w), lambda bi, t, qi, kj: (bi, qi[t], 0)),
        scratch_shapes=[pltpu.VMEM((nb, FOX_HEADS, tq), F32),
                        pltpu.VMEM((nb, FOX_HEADS, FOX_VROWS, tq), F32)])
    return pl.pallas_call(
        _fox_prompt_kernel,
        out_shape=jax.ShapeDtypeStruct((b, l, w), BF16),
        grid_spec=grid_spec,
        compiler_params=_params(("parallel", "arbitrary")),
        name="fox_prompt",
    )(jnp.asarray(qi), jnp.asarray(kj), qp, kp, vt)


def _fox_sample_kernel(q_ref, kp_ref, vp_ref, kn_ref, vn_ref, cq_ref, ck_ref, o_ref):
    lq = q_ref.shape[1]
    past = kp_ref.shape[4]
    ri = lax.broadcasted_iota(jnp.int32, (lq, lq), 0)
    ci = lax.broadcasted_iota(jnp.int32, (lq, lq), 1)
    nt = (((1,), (1,)), ((), ()))
    for h in range(FOX_HEADS):
        cols = slice(h * FOX_DIM, (h + 1) * FOX_DIM)
        q = q_ref[0, :, cols]
        cq = cq_ref[0, :, h:h + 1]
        sp = jnp.dot(q, kp_ref[0, 0, h].astype(BF16), preferred_element_type=F32)
        sp = sp + cq - ck_ref[0, h:h + 1, 0:past]
        sn = lax.dot_general(q, kn_ref[0, :, cols], nt, preferred_element_type=F32)
        sn = sn + cq - ck_ref[0, h:h + 1, past:past + lq]
        sn = jnp.where(ci <= ri, sn, NEG)
        m = jnp.maximum(jnp.max(sp, axis=-1, keepdims=True), jnp.max(sn, axis=-1, keepdims=True))
        pp = jnp.exp2(sp - m)
        pn = jnp.exp2(sn - m)
        den = jnp.sum(pp, axis=-1, keepdims=True) + jnp.sum(pn, axis=-1, keepdims=True)
        o = (lax.dot_general(pp.astype(BF16), vp_ref[0, 0, h].astype(BF16), nt, preferred_element_type=F32)
             + jnp.dot(pn.astype(BF16), vn_ref[0, :, cols], preferred_element_type=F32))
        o_ref[0, :, cols] = (o / den).astype(o_ref.dtype)


def fox_sample(q, kcache_t, vcache_t, layer, kn, vn, cq, ckt):
    b, lq, w = q.shape
    past = kcache_t.shape[4]
    lc = ckt.shape[2]
    new = lambda i: (i, 0, 0)
    old = lambda i: (layer, i, 0, 0, 0)
    cache_blk = pl.BlockSpec((1, 1, FOX_HEADS, FOX_DIM, past), old)
    return pl.pallas_call(
        _fox_sample_kernel,
        out_shape=jax.ShapeDtypeStruct((b, lq, w), BF16),
        grid=(b,),
        in_specs=[pl.BlockSpec((1, lq, w), new),
                  cache_blk,
                  cache_blk,
                  pl.BlockSpec((1, lq, w), new),
                  pl.BlockSpec((1, lq, w), new),
                  pl.BlockSpec((1, lq, FOX_HEADS), new),
                  pl.BlockSpec((1, FOX_HEADS, lc), new)],
        out_specs=pl.BlockSpec((1, lq, w), new),
        compiler_params=_params(("parallel",)),
        name="fox_sample",
    )(q, kcache_t, vcache_t, kn, vn, cq, ckt)


def _outproj_kernel(x_ref, hg_ref, fo_ref, mod_ref, fgain_ref, w_ref, gain2_ref,
                    wrh_ref, wrl_ref, br_ref, xo_ref, hs_ref, gs_ref, dest_ref, cnt_ref):
    s, ls, d = x_ref.shape
    tm = s * ls
    fo = fo_ref[...].astype(F32)
    fn = fo * lax.rsqrt(jnp.mean(fo * fo, axis=-1, keepdims=True) + EPS) * fgain_ref[...]
    mixed = (jnp.dot(hg_ref[...].reshape(tm, HG_WIDTH), w_ref[0:HG_WIDTH, :],
                     preferred_element_type=F32)
             + jnp.dot(fn.reshape(tm, FOX_WIDTH).astype(BF16), w_ref[HG_WIDTH:, :],
                       preferred_element_type=F32))
    x = x_ref[...] + mod_ref[:, 2:3, :] * mixed.reshape(s, ls, d)
    xo_ref[...] = x
    y = x * lax.rsqrt(jnp.mean(x * x, axis=-1, keepdims=True) + EPS) * gain2_ref[...]
    h2 = (y * (1.0 + mod_ref[:, 4:5, :]) + mod_ref[:, 3:4, :]).reshape(tm, d)
    h2b = h2.astype(BF16)

    nt = (((1,), (1,)), ((), ()))
    logits = lax.dot_general(wrh_ref[...], h2b, nt, preferred_element_type=F32) \
        + lax.dot_general(wrl_ref[...], h2b, nt, preferred_element_type=F32)
    z = jnp.exp(logits - jnp.max(logits, axis=0, keepdims=True))
    probs = z / jnp.sum(z, axis=0, keepdims=True)
    sel = probs + br_ref[...]
    rows = [sel[e:e + 1, :] for e in range(N_EXPERTS)]
    prow = [probs[e:e + 1, :] for e in range(N_EXPERTS)]

    def beats(a, ia, b_, ib):
        return jnp.where(a >= b_, 1.0, 0.0) if ia < ib else jnp.where(a > b_, 1.0, 0.0)

    top = []
    gscore = []
    for g in range(N_GROUPS):
        ids = range(g * GROUP_SIZE, (g + 1) * GROUP_SIZE)
        sc = jnp.zeros_like(rows[0])
        for e in ids:
            cnt = jnp.zeros_like(rows[0])
            for o in ids:
                if o != e:
                    cnt = cnt + beats(rows[o], o, rows[e], e)
            flag = jnp.where(cnt < TOP_K, 1.0, 0.0)
            top.append(flag)
            sc = sc + flag * rows[e]
        gscore.append(sc)
    chosen = []
    for g in range(N_GROUPS):
        cnt = jnp.zeros_like(rows[0])
        for o in range(N_GROUPS):
            if o != g:
                cnt = cnt + beats(gscore[o], o, gscore[g], g)
        chosen.append(jnp.where(cnt < 1.0, 1.0, 0.0))
    wts = [prow[e] * top[e] * chosen[e // GROUP_SIZE] for e in range(N_EXPERTS)]
    den = wts[0]
    for e in range(1, N_EXPERTS):
        den = den + wts[e]
    g4 = []
    for k in range(GROUP_SIZE):
        gk = wts[k]
        for g in range(1, N_GROUPS):
            gk = gk + wts[g * GROUP_SIZE + k]
        g4.append(gk / den)

    rr = hs_ref.shape[1]
    ki = lax.broadcasted_iota(jnp.int32, (tm, tm), 0)
    ji = lax.broadcasted_iota(jnp.int32, (tm, tm), 1)
    before = jnp.where(ki < ji, 1.0, 0.0).astype(BF16)
    chosen4 = jnp.concatenate(chosen, axis=0)
    rank = jnp.dot(chosen4.astype(BF16), before, preferred_element_type=F32)
    lane = lax.broadcasted_iota(jnp.int32, (1, LANES), 1)
    start = jnp.zeros((1, 1), F32)
    dest = jnp.zeros((1, tm), F32)
    cnt_row = jnp.zeros((1, LANES), F32)
    for g in range(N_GROUPS):
        n_g = jnp.sum(chosen[g], axis=1, keepdims=True)
        tiles_g = jnp.floor((n_g + (MOE_TILE - 1)) * (1.0 / MOE_TILE))
        dest = dest + chosen[g] * (start + rank[g:g + 1, :])
        cnt_row = cnt_row + jnp.where(lane == g, tiles_g, 0.0)
        start = start + tiles_g * MOE_TILE
    dest_i = dest.astype(jnp.int32)
    dest_ref[0] = dest_i
    cnt_ref[0] = cnt_row.astype(jnp.int32)
    perm = jnp.where(lax.broadcasted_iota(jnp.int32, (rr, tm), 0) == dest_i, 1.0, 0.0).astype(BF16)
    hs_ref[0] = jnp.dot(perm, h2b, preferred_element_type=F32).astype(BF16)
    g128 = jnp.concatenate(g4 + [jnp.zeros((LANES - GROUP_SIZE, tm), F32)], axis=0)
    ghi = g128.astype(BF16)
    glo = (g128 - ghi.astype(F32)).astype(BF16)
    gs_ref[0] = (lax.dot_general(perm, ghi, nt, preferred_element_type=F32)
                 + lax.dot_general(perm, glo, nt, preferred_element_type=F32))


def outproj(x, hg_out, fox_o, mod, fox_gain, w_out, gain2, wr_hi, wr_lo, b_router, s, ls):
    b, l, d = x.shape
    nblk = (b // s) * (l // ls)
    nl = l // ls
    tm = s * ls
    row = lambda i, j: (i, j, 0)
    const = lambda i, j: (0, 0)
    blk = lambda i, j: (i * nl + j, 0, 0)
    return pl.pallas_call(
        _outproj_kernel,
        out_shape=[jax.ShapeDtypeStruct((b, l, d), F32),
                   jax.ShapeDtypeStruct((nblk, MOE_ROWS, d), BF16),
                   jax.ShapeDtypeStruct((nblk, MOE_ROWS, LANES), F32),
                   jax.ShapeDtypeStruct((nblk, 1, tm), jnp.int32),
                   jax.ShapeDtypeStruct((nblk, 1, LANES), jnp.int32)],
        grid=(b // s, nl),
        in_specs=[pl.BlockSpec((s, ls, d), row),
                  pl.BlockSpec((s, ls, HG_WIDTH), row),
                  pl.BlockSpec((s, ls, FOX_WIDTH), row),
                  pl.BlockSpec((s, N_MOD, d), lambda i, j: (i, 0, 0)),
                  pl.BlockSpec((1, FOX_WIDTH), const),
                  pl.BlockSpec(w_out.shape, const, pipeline_mode=pl.Buffered(1)),
                  pl.BlockSpec((1, d), const),
                  pl.BlockSpec((N_EXPERTS, d), const),
                  pl.BlockSpec((N_EXPERTS, d), const),
                  pl.BlockSpec((N_EXPERTS, 1), const)],
        out_specs=[pl.BlockSpec((s, ls, d), row),
                   pl.BlockSpec((1, MOE_ROWS, d), blk),
                   pl.BlockSpec((1, MOE_ROWS, LANES), blk),
                   pl.BlockSpec((1, 1, tm), blk),
                   pl.BlockSpec((1, 1, LANES), blk)],
        compiler_params=_params(("parallel", "parallel")),
        name="outproj",
    )(x, hg_out, fox_o, mod, fox_gain, w_out, gain2, wr_hi, wr_lo, b_router)


def _moe_tables(cnt, rows_per_step):
    nblk = cnt.shape[0]
    tiles_per_step = rows_per_step // MOE_TILE
    max_tiles = nblk * (BLOCK_TOKENS // MOE_TILE + N_GROUPS)
    n_steps = -(-max_tiles // tiles_per_step) + N_GROUPS
    npair = N_GROUPS * nblk
    start_gm = (jnp.cumsum(cnt, axis=1) - cnt).T.reshape(-1)
    cnt_gm = cnt.T.reshape(-1)
    csum = jnp.cumsum(cnt_gm)
    tau = jnp.arange(max_tiles, dtype=jnp.int32)
    pair = jnp.minimum(jnp.sum((csum[None, :] <= tau[:, None]).astype(jnp.int32), axis=1), npair - 1)
    hot = (pair[:, None] == jnp.arange(npair, dtype=jnp.int32)[None, :]).astype(jnp.int32)
    k = tau - jnp.sum(hot * (csum - cnt_gm)[None, :], axis=1)
    valid = tau < csum[-1]
    tile_blk = jnp.where(valid, pair % nblk, 0).astype(jnp.int32)
    tile_row = jnp.where(valid, (jnp.sum(hot * start_gm[None, :], axis=1) + k) * MOE_TILE, 0).astype(jnp.int32)
    tot = jnp.sum(cnt, axis=0)
    gstart = jnp.cumsum(tot) - tot
    nsteps = (tot + tiles_per_step - 1) // tiles_per_step
    send = jnp.cumsum(nsteps)
    s = jnp.arange(n_steps, dtype=jnp.int32)
    sg = jnp.minimum(jnp.sum((send[None, :] <= s[:, None]).astype(jnp.int32), axis=1), N_GROUPS - 1)
    ghot = (sg[:, None] == jnp.arange(N_GROUPS, dtype=jnp.int32)[None, :]).astype(jnp.int32)
    pick = lambda v: jnp.sum(ghot * v[None, :], axis=1)
    first = pick(gstart) + (s - pick(send - nsteps)) * tiles_per_step
    num = jnp.clip(pick(gstart + tot) - first, 0, tiles_per_step)
    first = jnp.where(num > 0, first, 0)
    return sg.astype(jnp.int32), first.astype(jnp.int32), num.astype(jnp.int32), tile_blk, tile_row


def _moe_kernel(sg_ref, first_ref, num_ref, tblk_ref, trow_ref,
                hs_a, gs_a, hs_b, gs_b, wg_ref, wu_ref, wd_ref, yinit_a, yinit_b, ys_a, ys_b,
                hbuf, gbuf, ybuf, sem_in, sem_out):
    del sg_ref, yinit_a, yinit_b
    s = pl.program_id(0)
    ns = pl.num_programs(0)
    slot = lax.rem(s, 2)
    na = hs_a.shape[0]

    def tile(k):
        return pl.ds(pl.multiple_of(k * MOE_TILE, MOE_TILE), MOE_TILE)

    def src(step, k):
        t = first_ref[step] + k
        return tblk_ref[t], pl.ds(pl.multiple_of(trow_ref[t], MOE_TILE), MOE_TILE)

    class _Either:
        def __init__(self, blk, make):
            self.blk, self.make = blk, make

        def start(self):
            @pl.when(self.blk < na)
            def _():
                self.make(0, self.blk).start()

            @pl.when(self.blk >= na)
            def _():
                self.make(1, self.blk - na).start()

        def wait(self):
            self.make(0, 0).wait()

    def h_copy(step, sl, k):
        blk, rows = src(step, k)
        return _Either(blk, lambda g, b: pltpu.make_async_copy(
            (hs_a, hs_b)[g].at[b, rows, :], hbuf.at[sl, tile(k), :], sem_in.at[sl, 0]))

    def g_copy(step, sl, k):
        blk, rows = src(step, k)
        return _Either(blk, lambda g, b: pltpu.make_async_copy(
            (gs_a, gs_b)[g].at[b, rows, :], gbuf.at[sl, tile(k), :], sem_in.at[sl, 1]))

    def y_copy(step, sl, k):
        blk, rows = src(step, k)
        return _Either(blk, lambda g, b: pltpu.make_async_copy(
            ybuf.at[sl, tile(k), :], (ys_a, ys_b)[g].at[b, rows, :], sem_out.at[sl]))

    def each_tile(step, fn):
        def body(k, c):
            fn(k)
            return c
        lax.fori_loop(0, num_ref[step], body, 0)

    def start_gather(step, sl):
        def fn(k):
            h_copy(step, sl, k).start()
            g_copy(step, sl, k).start()
        each_tile(step, fn)

    def wait_gather(step, sl):
        def fn(k):
            h_copy(step, sl, k).wait()
            g_copy(step, sl, k).wait()
        each_tile(step, fn)

    @pl.when(s == 0)
    def _():
        hbuf[...] = jnp.zeros_like(hbuf)
        gbuf[...] = jnp.zeros_like(gbuf)
        start_gather(0, 0)

    wait_gather(s, slot)

    @pl.when(s + 1 < ns)
    def _():
        start_gather(s + 1, 1 - slot)

    @pl.when(s >= 2)
    def _():
        each_tile(s - 2, lambda k: y_copy(s - 2, slot, k).wait())

    @pl.when(num_ref[s] > 0)
    def _():
        h = hbuf[slot]
        g = gbuf[slot]
        acc = jnp.zeros((h.shape[0], wd_ref.shape[2]), F32)
        for k in range(GROUP_SIZE):
            a = jnp.dot(h, wg_ref[k].astype(BF16), preferred_element_type=F32)
            u = jnp.dot(h, wu_ref[k].astype(BF16), preferred_element_type=F32)
            he = (_silu(a) * u * g[:, k:k + 1]).astype(BF16)
            acc = acc + jnp.dot(he, wd_ref[k].astype(BF16), preferred_element_type=F32)
        ybuf[slot] = acc.astype(ybuf.dtype)

    each_tile(s, lambda k: y_copy(s, slot, k).start())

    @pl.when(s == ns - 1)
    def _():
        each_tile(s, lambda k: y_copy(s, slot, k).wait())

        @pl.when(s >= 1)
        def _():
            each_tile(s - 1, lambda k: y_copy(s - 1, 1 - slot, k).wait())


def moe_sparse(hs_a, gs_a, hs_b, gs_b, cnt, wg, wu, wd, layer):
    _, rr, d = hs_a.shape
    de = wg.shape[2]
    rows = MOE_STEP_ROWS
    sg, first, num, tile_blk, tile_row = _moe_tables(cnt, rows)
    n_steps = sg.shape[0]
    wmap = lambda s, sg, *_: (layer * N_GROUPS + sg[s], 0, 0)
    grid_spec = pltpu.PrefetchScalarGridSpec(
        num_scalar_prefetch=5,
        grid=(n_steps,),
        in_specs=[pl.BlockSpec(memory_space=pl.ANY)] * 4
        + [pl.BlockSpec((GROUP_SIZE, d, de), wmap, pipeline_mode=pl.Buffered(1)),
           pl.BlockSpec((GROUP_SIZE, d, de), wmap, pipeline_mode=pl.Buffered(1)),
           pl.BlockSpec((GROUP_SIZE, de, d), wmap, pipeline_mode=pl.Buffered(1))]
        + [pl.BlockSpec(memory_space=pl.ANY)] * 2,
        out_specs=[pl.BlockSpec(memory_space=pl.ANY)] * 2,
        scratch_shapes=[pltpu.VMEM((2, rows, d), BF16),
                        pltpu.VMEM((2, rows, LANES), F32),
                        pltpu.VMEM((2, rows, d), BF16),
                        pltpu.SemaphoreType.DMA((2, 2)),
                        pltpu.SemaphoreType.DMA((2,))])
    return pl.pallas_call(
        _moe_kernel,
        out_shape=[jax.ShapeDtypeStruct(hs_a.shape, BF16), jax.ShapeDtypeStruct(hs_b.shape, BF16)],
        grid_spec=grid_spec,
        input_output_aliases={12: 0, 13: 1},
        compiler_params=_params(("arbitrary",)),
        name="moe",
    )(sg, first, num, tile_blk, tile_row, hs_a, gs_a, hs_b, gs_b, wg, wu, wd,
      jnp.zeros(hs_a.shape, BF16), jnp.zeros(hs_b.shape, BF16))


def _combine_kernel(x_ref, ys_ref, dest_ref, mod_ref, fgain_ref, o_ref, *, final):
    s, ls, d = x_ref.shape
    tm = s * ls
    rr = ys_ref.shape[1]
    dcol = jnp.broadcast_to(dest_ref[0].astype(F32), (LANES, tm)).T
    lane = lax.broadcasted_iota(jnp.int32, (tm, LANES), 1).astype(F32)
    unperm = jnp.concatenate(
        [jnp.where(dcol == lane + float(c * LANES), 1.0, 0.0).astype(BF16) for c in range(rr // LANES)],
        axis=1)
    y = jnp.dot(unperm, ys_ref[0], preferred_element_type=F32)
    x = x_ref[...] + mod_ref[:, 5:6, :] * y.reshape(s, ls, d)
    if final:
        x = x * lax.rsqrt(jnp.mean(x * x, axis=-1, keepdims=True) + EPS) * fgain_ref[...]
    o_ref[...] = x


def combine(x, ys, dest, mod, final_gain, s, ls, final):
    b, l, d = x.shape
    nl = l // ls
    tm = s * ls
    row = lambda i, j: (i, j, 0)
    blk = lambda i, j: (i * nl + j, 0, 0)
    return pl.pallas_call(
        functools.partial(_combine_kernel, final=final),
        out_shape=jax.ShapeDtypeStruct((b, l, d), F32),
        grid=(b // s, nl),
        in_specs=[pl.BlockSpec((s, ls, d), row),
                  pl.BlockSpec((1, ys.shape[1], d), blk),
                  pl.BlockSpec((1, 1, tm), blk),
                  pl.BlockSpec((s, N_MOD, d), lambda i, j: (i, 0, 0)),
                  pl.BlockSpec((1, d), lambda i, j: (0, 0))],
        out_specs=pl.BlockSpec((s, ls, d), row),
        compiler_params=_params(("parallel", "parallel")),
        name="combine",
    )(x, ys, dest, mod, final_gain)


def _tile(b, l):
    ls = min(l, BLOCK_TOKENS)
    s = BLOCK_TOKENS // ls
    assert s * ls == BLOCK_TOKENS and b % s == 0 and l % ls == 0
    assert BLOCK_TOKENS + N_GROUPS * MOE_TILE <= MOE_ROWS
    return s, ls


def _mixer(x, mod, l, p, hg_state0, fox_past, kv_prev):
    depth = p['norm_mix_gain'].shape[0]
    b, seq, d = x.shape
    s, ls = _tile(b, seq)
    sample = fox_past is not None
    hg_in, fq, kbuf, vbuf, *k16v16, ff = inproj(x, mod, p['norm_mix_gain'][l], p['w_in'][l], s, ls,
                                                l, depth, kv_prev, sample)

    hg_out, hg_state = hgrn(hg_in, hg_state0, p['hg_lower_bounds'], p['hg_norm_gain'][l], l)

    if not sample:
        tq = min(seq, BLOCK_TOKENS)
        hg_out, fq = lax.optimization_barrier((hg_out, fq))
        qp, kp, vt, lf_pad = foxpack(fq, kbuf, vbuf, l, ff, p['fox_f_bias_pad'][l], tq)
        fox_o = fox_prompt(qp, kp, vt, tq)
        logf = lf_pad[:, :, :FOX_HEADS]
    else:
        kcache, vcache, plogf = fox_past
        past = kcache.shape[4]
        fft = jnp.swapaxes(ff[:, :, :FOX_HEADS], 1, 2)
        logft, _ = forget_cumsum(fft, p['fox_f_bias'][l], True, seq)
        tot = past + seq
        pad = (-tot) % LANES
        allt = jnp.concatenate([jnp.swapaxes(plogf[l], 1, 2), logft,
                                jnp.zeros((b, FOX_HEADS, pad), F32)], axis=2)
        _, ct = forget_cumsum(allt, p['fox_f_bias'][l], False, LANES)
        ct = ct * LOG2E
        cq = jnp.swapaxes(ct[:, :, past:tot], 1, 2)
        fox_o = fox_sample(fq, kcache, vcache, l, k16v16[0], k16v16[1], cq, ct)
        logf = jnp.swapaxes(logft, 1, 2)

    routed = outproj(x, hg_out, fox_o, mod, p['fox_out_gain'][l], p['w_out'][l],
                     p['norm_ffn_gain'][l], p['wr_hi'], p['wr_lo'], p['b_router'], s, ls)
    return routed, (kbuf, vbuf), hg_state, logf


def _ffn(routed_p, routed_s, mod_p, mod_s, l, p, final):
    xm_p, hs_p, gs_p, dest_p, cnt_p = routed_p
    xm_s, hs_s, gs_s, dest_s, cnt_s = routed_s
    cnt = jnp.concatenate([cnt_p, cnt_s], axis=0)[:, 0, :N_GROUPS]
    ys_p, ys_s = moe_sparse(hs_p, gs_p, hs_s, gs_s, cnt, p['w_exp_gate'], p['w_exp_up'], p['w_exp_down'], l)
    outs = []
    for xm, ys, dest, mod in ((xm_p, ys_p, dest_p, mod_p), (xm_s, ys_s, dest_s, mod_s)):
        s, ls = _tile(xm.shape[0], xm.shape[1])
        outs.append(combine(xm, ys, dest, mod, p['final_norm_gain'], s, ls, final))
    return outs


def kernel(x_prompt, x_sample, cache_fox_k, cache_fox_v, cache_fox_logf, state_hgrn, c_prompt, c_sample,
           norm_mix_gain, norm_ffn_gain, w_ada, b_ada, w_in, hg_lower_bounds, hg_norm_gain,
           fox_f_bias, fox_out_gain, w_out, w_router, b_router, w_exp_gate, w_exp_up, w_exp_down,
           final_norm_gain):
    depth, d = norm_mix_gain.shape
    bp = x_prompt.shape[0]
    n_in = w_in.shape[2]
    n_pad = 4 * HG_WIDTH + 3 * FOX_WIDTH + LANES - n_in
    wr_t = w_router.T
    wr_hi = wr_t.astype(BF16)
    p = {
        'norm_mix_gain': norm_mix_gain.reshape(depth, 1, d),
        'norm_ffn_gain': norm_ffn_gain.reshape(depth, 1, d),
        'w_in': jnp.pad(w_in, ((0, 0), (0, 0), (0, n_pad))).astype(BF16),
        'hg_lower_bounds': hg_lower_bounds,
        'hg_norm_gain': hg_norm_gain.reshape(depth, 1, HG_WIDTH),
        'fox_f_bias': fox_f_bias.reshape(depth, FOX_HEADS, 1),
        'fox_f_bias_pad': jnp.pad(fox_f_bias, ((0, 0), (0, LANES - FOX_HEADS))).reshape(depth, 1, LANES),
        'fox_out_gain': fox_out_gain.reshape(depth, 1, FOX_WIDTH),
        'w_out': w_out.astype(BF16),
        'wr_hi': wr_hi,
        'wr_lo': (wr_t - wr_hi.astype(F32)).astype(BF16),
        'b_router': b_router.reshape(N_EXPERTS, 1),
        'w_exp_gate': w_exp_gate.reshape((depth * N_EXPERTS,) + w_exp_gate.shape[2:]),
        'w_exp_up': w_exp_up.reshape((depth * N_EXPERTS,) + w_exp_up.shape[2:]),
        'w_exp_down': w_exp_down.reshape((depth * N_EXPERTS,) + w_exp_down.shape[2:]),
        'final_norm_gain': final_norm_gain.reshape(1, d),
    }
    mods = ada_mod(jnp.concatenate([c_prompt, c_sample], axis=0), w_ada, b_ada)
    mods = mods.reshape(depth, -1, N_MOD, d)
    bs, lsq = x_sample.shape[:2]
    lp = x_prompt.shape[1]
    past = cache_fox_k.shape[2]
    fox_past = (jnp.transpose(cache_fox_k, (0, 1, 3, 4, 2)), jnp.transpose(cache_fox_v, (0, 1, 3, 4, 2)),
                cache_fox_logf)

    xp, xs = x_prompt, x_sample
    kv_p = kv_s = None
    st_p, st_s, lf_p, lf_s = [], [], [], []
    zero_state = jnp.zeros((bp, HG_HEADS, HG_DIM, HG_DIM), F32)
    for l in range(depth):
        routed_p, kv_p, st, lf = _mixer(xp, mods[l, :bp], l, p, zero_state, None, kv_p)
        st_p.append(st)
        lf_p.append(lf)
        routed_s, kv_s, st, lf = _mixer(xs, mods[l, bp:], l, p, state_hgrn[l], fox_past, kv_s)
        st_s.append(st)
        lf_s.append(lf)
        xp, xs = _ffn(routed_p, routed_s, mods[l, :bp], mods[l, bp:], l, p, l == depth - 1)
    heads = lambda a, b, l: a.reshape(depth, b, l, FOX_HEADS, FOX_DIM)
    return (xp, xs,
            jnp.stack(st_p), heads(kv_p[0], bp, lp), heads(kv_p[1], bp, lp), jnp.stack(lf_p),
            jnp.stack(st_s), heads(kv_s[0], bs, lsq), heads(kv_s[1], bs, lsq), jnp.stack(lf_s))
```

```python
import functools
import math

import numpy as np
import jax
import jax.numpy as jnp
from jax import lax
from jax.experimental import pallas as pl
from jax.experimental.pallas import tpu as pltpu

HG_HEADS = 4
HG_DIM = 128
HG_WIDTH = HG_HEADS * HG_DIM
FOX_HEADS = 8
FOX_DIM = 64
FOX_WIDTH = FOX_HEADS * FOX_DIM
N_EXPERTS = 16
N_GROUPS = 4
GROUP_SIZE = N_EXPERTS // N_GROUPS
TOP_K = 2
N_MOD = 6
EPS = 1e-6

LANES = 128
HG_CHUNK = 128
HG_SUB = 16
HG_STREAMS = 8
MOE_TILE = 16
BLOCK_TOKENS = 512
MOE_ROWS = 640
MOE_STEP_ROWS = 1024
VMEM_LIMIT = 56 * 1024 * 1024

F32 = jnp.float32
BF16 = jnp.bfloat16
NEG = -1e30
LOG2E = math.log2(math.e)


def _params(sem, vmem=VMEM_LIMIT):
    return pltpu.CompilerParams(dimension_semantics=sem, vmem_limit_bytes=vmem)


def _split3(x):
    hi = x.astype(BF16)
    r1 = x - hi.astype(F32)
    mid = r1.astype(BF16)
    lo = (r1 - mid.astype(F32)).astype(BF16)
    return hi, mid, lo


def _sigmoid(x):
    return 1.0 / (1.0 + jnp.exp(-x))


def _log_sigmoid(x):
    return jnp.minimum(x, 0.0) - jnp.log(1.0 + jnp.exp(-jnp.abs(x)))


def _silu(x):
    return x * _sigmoid(x)


def _ada_kernel(c_ref, w_ref, b_ref, o_ref):
    s = _silu(c_ref[...]).astype(BF16)
    o_ref[0] = jnp.dot(s, w_ref[0].astype(BF16), preferred_element_type=F32) + b_ref[0]


def ada_mod(c, w_ada, b_ada):
    depth, d, n = w_ada.shape
    nb = c.shape[0]
    tn = 1536
    return pl.pallas_call(
        _ada_kernel,
        out_shape=jax.ShapeDtypeStruct((depth, nb, n), F32),
        grid=(depth, n // tn),
        in_specs=[pl.BlockSpec((nb, d), lambda l, j: (0, 0)),
                  pl.BlockSpec((1, d, tn), lambda l, j: (l, 0, j)),
                  pl.BlockSpec((1, 1, tn), lambda l, j: (l, 0, j))],
        out_specs=pl.BlockSpec((1, nb, tn), lambda l, j: (l, 0, j)),
        compiler_params=_params(("arbitrary", "arbitrary")),
        name="ada_mod",
    )(c, w_ada, b_ada.reshape(depth, 1, n))


def _inproj_kernel(x_ref, mod_ref, gain_ref, w_ref, *refs, has_prev, emit16):
    outs = refs[2:] if has_prev else refs
    hg_ref, fq_ref, fk32_ref, fv32_ref = outs[:4]
    ff_ref = outs[-1]
    s, ls, d = x_ref.shape
    x = x_ref[...]
    y = x * lax.rsqrt(jnp.mean(x * x, axis=-1, keepdims=True) + EPS) * gain_ref[...]
    h = y * (1.0 + mod_ref[:, 1:2, :]) + mod_ref[:, 0:1, :]
    hb = h.reshape(s * ls, d).astype(BF16)

    def proj(lo, hi):
        return jnp.dot(hb, w_ref[:, lo:hi], preferred_element_type=F32)

    c0 = 4 * HG_WIDTH
    hg_ref[...] = proj(0, c0).reshape(s, ls, c0)
    fq = proj(c0, c0 + FOX_WIDTH) * (FOX_DIM ** -0.5 * LOG2E)
    fq_ref[...] = fq.reshape(s, ls, FOX_WIDTH).astype(BF16)
    fk = proj(c0 + FOX_WIDTH, c0 + 2 * FOX_WIDTH).reshape(s, ls, FOX_WIDTH)
    fv = proj(c0 + 2 * FOX_WIDTH, c0 + 3 * FOX_WIDTH).reshape(s, ls, FOX_WIDTH)
    for slot in range(fk32_ref.shape[0]):
        fk32_ref[slot] = fk
        fv32_ref[slot] = fv
    if emit16:
        outs[4][...] = fk.astype(BF16)
        outs[5][...] = fv.astype(BF16)
    ff_ref[...] = proj(c0 + 3 * FOX_WIDTH, c0 + 3 * FOX_WIDTH + LANES).reshape(s, ls, LANES)


def inproj(x, mod, gain, w_pad, s, ls, layer, depth, kv_prev, emit16):
    b, l, d = x.shape
    n = w_pad.shape[1]
    row = lambda i, j: (i, j, 0)
    lrow = lambda i, j: (layer, i, j, 0)
    shapes = [jax.ShapeDtypeStruct((b, l, 4 * HG_WIDTH), F32),
              jax.ShapeDtypeStruct((b, l, FOX_WIDTH), BF16),
              jax.ShapeDtypeStruct((depth, b, l, FOX_WIDTH), F32),
              jax.ShapeDtypeStruct((depth, b, l, FOX_WIDTH), F32)]
    slots = depth if kv_prev is None else 1
    if kv_prev is None:
        assert layer == 0
    specs = [pl.BlockSpec((s, ls, 4 * HG_WIDTH), row),
             pl.BlockSpec((s, ls, FOX_WIDTH), row),
             pl.BlockSpec((slots, s, ls, FOX_WIDTH), lrow),
             pl.BlockSpec((slots, s, ls, FOX_WIDTH), lrow)]
    if emit16:
        shapes += [jax.ShapeDtypeStruct((b, l, FOX_WIDTH), BF16)] * 2
        specs += [pl.BlockSpec((s, ls, FOX_WIDTH), row)] * 2
    shapes.append(jax.ShapeDtypeStruct((b, l, LANES), F32))
    specs.append(pl.BlockSpec((s, ls, LANES), row))
    in_specs = [pl.BlockSpec((s, ls, d), row),
                pl.BlockSpec((s, N_MOD, d), lambda i, j: (i, 0, 0)),
                pl.BlockSpec((1, d), lambda i, j: (0, 0)),
                pl.BlockSpec((d, n), lambda i, j: (0, 0), pipeline_mode=pl.Buffered(1))]
    args = [x, mod, gain, w_pad]
    aliases = {}
    if kv_prev is not None:
        in_specs += [pl.BlockSpec(memory_space=pl.ANY)] * 2
        args += list(kv_prev)
        aliases = {4: 2, 5: 3}
    return pl.pallas_call(
        functools.partial(_inproj_kernel, has_prev=kv_prev is not None, emit16=emit16),
        out_shape=shapes,
        grid=(b // s, l // ls),
        in_specs=in_specs,
        out_specs=specs,
        input_output_aliases=aliases,
        compiler_params=_params(("parallel", "parallel")),
        name="inproj",
    )(*args)


def _cumsum_kernel(x_ref, bias_ref, lf_ref, c_ref, carry_ref, *, apply_ls):
    @pl.when(pl.program_id(1) == 0)
    def _():
        carry_ref[...] = jnp.zeros_like(carry_ref)

    bb, nh, tc = x_ref.shape
    x = x_ref[...].reshape(bb * nh, tc)
    lf = _log_sigmoid(x + bias_ref[...]) if apply_ls else x
    lf_ref[...] = lf.reshape(bb, nh, tc)
    r = lax.broadcasted_iota(jnp.int32, (tc, tc), 0)
    c = lax.broadcasted_iota(jnp.int32, (tc, tc), 1)
    tri = jnp.where(r <= c, 1.0, 0.0).astype(BF16)
    hi, mid, lo = _split3(lf)
    tot = (jnp.dot(hi, tri, preferred_element_type=F32)
           + jnp.dot(mid, tri, preferred_element_type=F32)
           + jnp.dot(lo, tri, preferred_element_type=F32)) + carry_ref[:, 0:1]
    c_ref[...] = tot.reshape(bb, nh, tc)
    carry_ref[...] = jnp.broadcast_to(tot[:, tc - 1:tc], carry_ref.shape)


def forget_cumsum(xt, bias, apply_ls, tc):
    b, h, l = xt.shape
    blk = pl.BlockSpec((b, h, tc), lambda i, j: (0, 0, j))
    return pl.pallas_call(
        functools.partial(_cumsum_kernel, apply_ls=apply_ls),
        out_shape=[jax.ShapeDtypeStruct((b, h, l), F32)] * 2,
        grid=(1, l // tc),
        in_specs=[blk, pl.BlockSpec((b * h, 1), lambda i, j: (0, 0))],
        out_specs=[blk, blk],
        scratch_shapes=[pltpu.VMEM((b * h, LANES), F32)],
        compiler_params=_params(("arbitrary", "arbitrary")),
        name="forget_cumsum",
    )(xt, jnp.tile(bias, (b, 1)))


HG_HALVES = (8, 4, 2, 1)


def _hgrn_sum_matrix():
    c = HG_CHUNK
    t = np.arange(c)[:, None]
    u = np.arange(c)[None, :]
    mats = [(u <= t)]
    for w in HG_HALVES:
        pos = t % (2 * w)
        mid = t - pos + w - 1
        upper = pos >= w
        mats.append(np.where(upper, (u > mid) & (u <= t), (u > t) & (u <= mid)))
    return jnp.asarray(np.concatenate(mats, axis=0).astype(np.float32), BF16)


def _hgrn_kernel(hin_ref, s0_ref, lbp_ref, gain_ref, sums_ref, out_ref, sfin_ref, st_ref, *, layer):
    t = pl.program_id(1)
    nt = pl.num_programs(1)
    c = HG_CHUNK
    nsub = c // HG_SUB
    nb = hin_ref.shape[0]
    rows_in = hin_ref.shape[1]

    def pad_rows(a):
        if rows_in == c:
            return a
        return jnp.concatenate([a, jnp.zeros((c - rows_in, a.shape[1]), a.dtype)], axis=0)

    @pl.when(t == 0)
    def _():
        for bi in range(nb):
            for h in range(HG_HEADS):
                st_ref[bi, h] = s0_ref[bi, h].T

    lbp = lbp_ref[...]
    e = jnp.exp(lbp - jnp.max(lbp, axis=0, keepdims=True))
    p = e / jnp.sum(e, axis=0, keepdims=True)
    acc = p[0:1]
    first = acc
    for i in range(1, layer + 1):
        acc = acc + p[i:i + 1]
    lb_all = acc - first

    ri = lax.broadcasted_iota(jnp.int32, (c, c), 0)
    ci = lax.broadcasted_iota(jnp.int32, (c, c), 1)
    ones = jnp.ones((HG_DIM, HG_DIM), BF16)
    nt_dims = (((1,), (1,)), ((), ()))

    streams = [(bi, h) for bi in range(nb) for h in range(HG_HEADS)]
    q, kk, v, b, sums, att = {}, {}, {}, {}, {}, {}
    for bi, h in streams:
        lb = lb_all[:, h * HG_DIM:(h + 1) * HG_DIM]
        hq = hin_ref[bi, :, h * HG_DIM:(h + 1) * HG_DIM]
        hf = hin_ref[bi, :, HG_WIDTH + h * HG_DIM:HG_WIDTH + (h + 1) * HG_DIM]
        hi = hin_ref[bi, :, 2 * HG_WIDTH + h * HG_DIM:2 * HG_WIDTH + (h + 1) * HG_DIM]
        q[bi, h] = pad_rows(_silu(hq) * (HG_DIM ** -0.5))
        a1 = jnp.log(lb)
        e = jnp.exp(-jnp.abs(hf))
        den = 1.0 + e
        a2 = jnp.log1p(-lb) + (jnp.minimum(hf, 0.0) - jnp.log(den))
        mx = jnp.maximum(a1, a2)
        lf = pad_rows(mx + jnp.log(jnp.exp(a1 - mx) + jnp.exp(a2 - mx)))
        kk[bi, h] = pad_rows((1.0 - lb) * (jnp.where(hf > 0.0, e, 1.0) / den))
        v[bi, h] = pad_rows(hi)
        lhi, lmid, llo = _split3(lf)
        sums[bi, h] = (jnp.dot(sums_ref[...], lhi, preferred_element_type=F32)
                       + jnp.dot(sums_ref[...], lmid, preferred_element_type=F32)
                       + jnp.dot(sums_ref[...], llo, preferred_element_type=F32))
        b[bi, h] = sums[bi, h][0:c]

    for key in streams:
        att[key] = jnp.where(ri == ci, jnp.dot((q[key] * kk[key]).astype(BF16), ones,
                                               preferred_element_type=F32), 0.0)
    for lvl, w in enumerate(HG_HALVES):
        shift = (2 * w).bit_length() - 1
        pair = ((ri >> shift) == (ci >> shift)) & ((ri & (2 * w - 1)) >= w) & ((ci & (2 * w - 1)) < w)
        for key in streams:
            dec = jnp.exp(sums[key][(lvl + 1) * c:(lvl + 2) * c])
            sc = lax.dot_general((q[key] * dec).astype(BF16), (kk[key] * dec).astype(BF16), nt_dims,
                                 preferred_element_type=F32)
            att[key] = att[key] + jnp.where(pair, sc, 0.0)

    khat = {key: None for key in streams}
    r_prev = {key: jnp.zeros((1, HG_DIM), F32) for key in streams}
    qhat_blocks = {key: [] for key in streams}
    cross = {key: [] for key in streams}
    for i in range(nsub):
        r0 = i * HG_SUB
        for key in streams:
            bb = b[key][r0:r0 + HG_SUB]
            r_next = b[key][r0 + HG_SUB - 1:r0 + HG_SUB]
            qt = q[key][r0:r0 + HG_SUB] * jnp.exp(bb - r_prev[key])
            if khat[key] is None:
                cross[key].append(jnp.zeros((HG_SUB, c), F32))
            else:
                kfull = jnp.concatenate([khat[key], jnp.zeros((c - r0, HG_DIM), F32)], axis=0)
                cross[key].append(lax.dot_general(qt.astype(BF16), kfull.astype(BF16), nt_dims,
                                                  preferred_element_type=F32))
            qhat_blocks[key].append(qt * jnp.exp(r_prev[key]))
            kt = kk[key][r0:r0 + HG_SUB] * jnp.exp(r_next - bb)
            if khat[key] is None:
                khat[key] = kt
            else:
                khat[key] = jnp.concatenate([khat[key] * jnp.exp(r_next - r_prev[key]), kt], axis=0)
            r_prev[key] = r_next

    for bi, h in streams:
        key = (bi, h)
        cols = slice(h * HG_DIM, (h + 1) * HG_DIM)
        st = st_ref[bi, h]
        a = att[key] + jnp.concatenate(cross[key], axis=0)
        qhat = jnp.concatenate(qhat_blocks[key], axis=0).astype(BF16)
        o = jnp.dot(a.astype(BF16), v[key].astype(BF16), preferred_element_type=F32)
        o = o + lax.dot_general(qhat, st.astype(BF16), nt_dims, preferred_element_type=F32)
        st_ref[bi, h] = st * jnp.exp(r_prev[key]) + jnp.dot(
            v[key].T.astype(BF16), khat[key].astype(BF16), preferred_element_type=F32)

        hg = hin_ref[bi, :, 3 * HG_WIDTH + h * HG_DIM:3 * HG_WIDTH + (h + 1) * HG_DIM]
        o = o[:rows_in]
        o = o * lax.rsqrt(jnp.mean(o * o, axis=-1, keepdims=True) + EPS)
        o = o * gain_ref[:, cols] * _silu(hg)
        out_ref[bi, :, cols] = o.astype(out_ref.dtype)

    @pl.when(t == nt - 1)
    def _():
        for bi, h in streams:
            sfin_ref[bi, h] = st_ref[bi, h].T


def hgrn(hin, s0, lbp, gain, layer):
    b, l, w = hin.shape
    c = min(HG_CHUNK, l)
    assert l % c == 0 and c % HG_SUB == 0
    nt = l // c
    nb = HG_STREAMS
    assert b % nb == 0
    sums = _hgrn_sum_matrix()
    return pl.pallas_call(
        functools.partial(_hgrn_kernel, layer=layer),
        out_shape=[jax.ShapeDtypeStruct((b, l, HG_WIDTH), BF16),
                   jax.ShapeDtypeStruct((b, HG_HEADS, HG_DIM, HG_DIM), F32)],
        grid=(b // nb, nt),
        in_specs=[pl.BlockSpec((nb, c, w), lambda i, j: (i, j, 0)),
                  pl.BlockSpec((nb, HG_HEADS, HG_DIM, HG_DIM), lambda i, j: (i, 0, 0, 0)),
                  pl.BlockSpec(lbp.shape, lambda i, j: (0, 0)),
                  pl.BlockSpec((1, HG_WIDTH), lambda i, j: (0, 0)),
                  pl.BlockSpec(sums.shape, lambda i, j: (0, 0))],
        out_specs=[pl.BlockSpec((nb, c, HG_WIDTH), lambda i, j: (i, j, 0)),
                   pl.BlockSpec((nb, HG_HEADS, HG_DIM, HG_DIM), lambda i, j: (i, 0, 0, 0))],
        scratch_shapes=[pltpu.VMEM((nb, HG_HEADS, HG_DIM, HG_DIM), F32)],
        compiler_params=_params(("parallel", "arbitrary")),
        name="hgrn",
    )(hin, s0, lbp, gain, sums)


AUG_ONE = FOX_DIM
AUG_NEG = FOX_DIM + 3
ONE_LANE = FOX_HEADS
FOX_GROUP = 4
FOX_STREAMS = 4
FOX_VROWS = FOX_DIM + 16


def _placement():
    pk = np.zeros((3 * LANES, FOX_HEADS * LANES), np.float32)
    pq = np.zeros((3 * LANES, FOX_HEADS * LANES), np.float32)
    for h in range(FOX_HEADS):
        for part in range(3):
            pk[part * LANES + h, h * LANES + AUG_NEG + part] = -1.0
            pq[part * LANES + h, h * LANES + AUG_ONE + part] = 1.0
            pk[ONE_LANE, h * LANES + AUG_ONE + part] = 1.0
            pq[ONE_LANE, h * LANES + AUG_NEG + part] = 1.0
    return jnp.asarray(pk, BF16), jnp.asarray(pq, BF16)


def _foxpack_kernel(q_ref, k_ref, v_ref, ff_ref, bias_ref, pk_ref, pq_ref,
                    qp_ref, kp_ref, vt_ref, lf_ref, carry_ref):
    @pl.when(pl.program_id(1) == 0)
    def _():
        carry_ref[...] = jnp.zeros_like(carry_ref)

    tm = q_ref.shape[1]
    lane = lax.broadcasted_iota(jnp.int32, (tm, LANES), 1)
    lf = jnp.where(lane < FOX_HEADS, _log_sigmoid(ff_ref[0] + bias_ref[...]), 0.0)
    lf_ref[0] = lf
    ri = lax.broadcasted_iota(jnp.int32, (tm, tm), 0)
    ci = lax.broadcasted_iota(jnp.int32, (tm, tm), 1)
    tril = jnp.where(ci <= ri, 1.0, 0.0).astype(BF16)
    hi, mid, lo = _split3(lf)
    c = (jnp.dot(tril, hi, preferred_element_type=F32)
         + jnp.dot(tril, mid, preferred_element_type=F32)
         + jnp.dot(tril, lo, preferred_element_type=F32)) + carry_ref[0:1, :]
    carry_ref[...] = jnp.broadcast_to(c[tm - 1:tm, :], carry_ref.shape)

    chi, cmid, clo = _split3(c * LOG2E)
    chi = jnp.where(lane == ONE_LANE, 1.0, chi.astype(F32)).astype(BF16)
    cterms = jnp.concatenate([chi, cmid, clo], axis=1)
    augk = jnp.dot(cterms, pk_ref[...], preferred_element_type=F32)
    augq = jnp.dot(cterms, pq_ref[...], preferred_element_type=F32)
    for h in range(FOX_HEADS):
        blk = slice((h // 2) * LANES, (h // 2 + 1) * LANES)
        hcols = slice(h * LANES, (h + 1) * LANES)
        kh = k_ref[0, 0, :, blk]
        qh = q_ref[0, :, blk].astype(F32)
        if h % 2:
            kh = pltpu.roll(kh, FOX_DIM, axis=1)
            qh = pltpu.roll(qh, FOX_DIM, axis=1)
        kp_ref[0, h] = jnp.where(lane < FOX_DIM, kh, augk[:, hcols]).astype(BF16)
        qp_ref[0, h] = jnp.where(lane < FOX_DIM, qh, augq[:, hcols]).astype(BF16)
    vt = v_ref[0, 0].T
    tail = jnp.where(lax.broadcasted_iota(jnp.int32, (FOX_VROWS - FOX_DIM, tm), 0) == 0, 1.0, 0.0)
    for h in range(FOX_HEADS):
        vt_ref[0, h] = jnp.concatenate([vt[h * FOX_DIM:(h + 1) * FOX_DIM], tail], axis=0).astype(BF16)


def foxpack(fq, kbuf, vbuf, layer, ff, bias_pad, tm):
    b, l, w = fq.shape
    pk, pq = _placement()
    row = lambda i, j: (i, j, 0)
    lrow = lambda i, j: (layer, i, j, 0)
    const = lambda i, j: (0, 0)
    head_blk = pl.BlockSpec((1, FOX_HEADS, tm, LANES), lambda i, j: (i, 0, j, 0))
    return pl.pallas_call(
        _foxpack_kernel,
        out_shape=[jax.ShapeDtypeStruct((b, FOX_HEADS, l, LANES), BF16),
                   jax.ShapeDtypeStruct((b, FOX_HEADS, l, LANES), BF16),
                   jax.ShapeDtypeStruct((b, FOX_HEADS, FOX_VROWS, l), BF16),
                   jax.ShapeDtypeStruct((b, l, LANES), F32)],
        grid=(b, l // tm),
        in_specs=[pl.BlockSpec((1, tm, w), row),
                  pl.BlockSpec((1, 1, tm, w), lrow),
                  pl.BlockSpec((1, 1, tm, w), lrow),
                  pl.BlockSpec((1, tm, LANES), row),
                  pl.BlockSpec((1, LANES), const),
                  pl.BlockSpec(pk.shape, const),
                  pl.BlockSpec(pq.shape, const)],
        out_specs=[head_blk, head_blk,
                   pl.BlockSpec((1, FOX_HEADS, FOX_VROWS, tm), lambda i, j: (i, 0, 0, j)),
                   pl.BlockSpec((1, tm, LANES), row)],
        scratch_shapes=[pltpu.VMEM((8, LANES), F32)],
        compiler_params=_params(("parallel", "arbitrary")),
        name="foxpack",
    )(fq, kbuf, vbuf, ff, bias_pad, pk, pq)


def _fox_prompt_kernel(qi_ref, kj_ref, q_ref, k_ref, vt_ref, o_ref, m_ref, acc_ref):
    t = pl.program_id(1)
    i = qi_ref[t]
    j = kj_ref[t]
    nb = q_ref.shape[0]
    tq = q_ref.shape[2]
    tk = k_ref.shape[2]
    assert tq == tk
    nt = (((1,), (1,)), ((), ()))

    @pl.when(j == 0)
    def _():
        m_ref[...] = jnp.full_like(m_ref, NEG)
        acc_ref[...] = jnp.zeros_like(acc_ref)

    def step(diagonal):
        if diagonal:
            keep = (lax.broadcasted_iota(jnp.int32, (tk, tq), 0)
                    <= lax.broadcasted_iota(jnp.int32, (tk, tq), 1))
        items = [(bi, h) for bi in range(nb) for h in range(FOX_HEADS)]
        for g0 in range(0, len(items), FOX_GROUP):
            group = items[g0:g0 + FOX_GROUP]
            st = {(bi, h): lax.dot_general(k_ref[bi, h], q_ref[bi, h], nt, preferred_element_type=F32)
                  for bi, h in group}
            if diagonal:
                st = {it: jnp.where(keep, s, NEG) for it, s in st.items()}
            m_old = {(bi, h): m_ref[bi, h:h + 1, :] for bi, h in group}
            m_new = {it: jnp.maximum(m_old[it], jnp.max(st[it], axis=0, keepdims=True)) for it in group}
            p = {it: jnp.exp2((st[it] - m_new[it]).astype(BF16)) for it in group}
            for bi, h in group:
                alpha = jnp.exp2(m_old[bi, h] - m_new[bi, h])
                acc_ref[bi, h] = alpha * acc_ref[bi, h] + jnp.dot(vt_ref[bi, h], p[bi, h],
                                                                  preferred_element_type=F32)
                m_ref[bi, h:h + 1, :] = m_new[bi, h]

    @pl.when(j < i)
    def _():
        step(False)

    @pl.when(j == i)
    def _():
        step(True)
        for bi in range(nb):
            outs = [acc_ref[bi, h, 0:FOX_DIM, :] / acc_ref[bi, h, FOX_DIM:FOX_DIM + 1, :]
                    for h in range(FOX_HEADS)]
            o_ref[bi] = jnp.concatenate(outs, axis=0).T.astype(o_ref.dtype)


def fox_prompt(qp, kp, vt, tq):
    b, nh, l, _ = qp.shape
    w = nh * FOX_DIM
    nq = l // tq
    tk = tq
    assert nq * tq == l
    qi = np.asarray([i for i in range(nq) for _ in range(i + 1)], np.int32)
    kj = np.asarray([j for i in range(nq) for j in range(i + 1)], np.int32)
    nb = FOX_STREAMS
    assert b % nb == 0
    grid_spec = pltpu.PrefetchScalarGridSpec(
        num_scalar_prefetch=2,
        grid=(b // nb, len(qi)),
        in_specs=[pl.BlockSpec((nb, nh, tq, LANES), lambda bi, t, qi, kj: (bi, 0, qi[t], 0)),
                  pl.BlockSpec((nb, nh, tk, LANES), lambda bi, t, qi, kj: (bi, 0, kj[t], 0)),
                  pl.BlockSpec((nb, nh, FOX_VROWS, tk), lambda bi, t, qi, kj: (bi, 0, 0, kj[t]))],
        out_specs=pl.BlockSpec((nb, tq, w), lambda bi, t, qi, kj: (bi, qi[t], 0)),
        scratch_shapes=[pltpu.VMEM((nb, FOX_HEADS, tq), F32),
                        pltpu.VMEM((nb, FOX_HEADS, FOX_VROWS, tq), F32)])
    return pl.pallas_call(
        _fox_prompt_kernel,
        out_shape=jax.ShapeDtypeStruct((b, l, w), BF16),
        grid_spec=grid_spec,
        compiler_params=_params(("parallel", "arbitrary")),
        name="fox_prompt",
    )(jnp.asarray(qi), jnp.asarray(kj), qp, kp, vt)


def _fox_sample_kernel(q_ref, kp_ref, vp_ref, kn_ref, vn_ref, cq_ref, ck_ref, o_ref):
    lq = q_ref.shape[1]
    past = kp_ref.shape[4]
    ri = lax.broadcasted_iota(jnp.int32, (lq, lq), 0)
    ci = lax.broadcasted_iota(jnp.int32, (lq, lq), 1)
    nt = (((1,), (1,)), ((), ()))
    for h in range(FOX_HEADS):
        cols = slice(h * FOX_DIM, (h + 1) * FOX_DIM)
        q = q_ref[0, :, cols]
        cq = cq_ref[0, :, h:h + 1]
        sp = jnp.dot(q, kp_ref[0, 0, h].astype(BF16), preferred_element_type=F32)
        sp = sp + cq - ck_ref[0, h:h + 1, 0:past]
        sn = lax.dot_general(q, kn_ref[0, :, cols], nt, preferred_element_type=F32)
        sn = sn + cq - ck_ref[0, h:h + 1, past:past + lq]
        sn = jnp.where(ci <= ri, sn, NEG)
        m = jnp.maximum(jnp.max(sp, axis=-1, keepdims=True), jnp.max(sn, axis=-1, keepdims=True))
        pp = jnp.exp2(sp - m)
        pn = jnp.exp2(sn - m)
        den = jnp.sum(pp, axis=-1, keepdims=True) + jnp.sum(pn, axis=-1, keepdims=True)
        o = (lax.dot_general(pp.astype(BF16), vp_ref[0, 0, h].astype(BF16), nt, preferred_element_type=F32)
             + jnp.dot(pn.astype(BF16), vn_ref[0, :, cols], preferred_element_type=F32))
        o_ref[0, :, cols] = (o / den).astype(o_ref.dtype)


def fox_sample(q, kcache_t, vcache_t, layer, kn, vn, cq, ckt):
    b, lq, w = q.shape
    past = kcache_t.shape[4]
    lc = ckt.shape[2]
    new = lambda i: (i, 0, 0)
    old = lambda i: (layer, i, 0, 0, 0)
    cache_blk = pl.BlockSpec((1, 1, FOX_HEADS, FOX_DIM, past), old)
    return pl.pallas_call(
        _fox_sample_kernel,
        out_shape=jax.ShapeDtypeStruct((b, lq, w), BF16),
        grid=(b,),
        in_specs=[pl.BlockSpec((1, lq, w), new),
                  cache_blk,
                  cache_blk,
                  pl.BlockSpec((1, lq, w), new),
                  pl.BlockSpec((1, lq, w), new),
                  pl.BlockSpec((1, lq, FOX_HEADS), new),
                  pl.BlockSpec((1, FOX_HEADS, lc), new)],
        out_specs=pl.BlockSpec((1, lq, w), new),
        compiler_params=_params(("parallel",)),
        name="fox_sample",
    )(q, kcache_t, vcache_t, kn, vn, cq, ckt)


def _outproj_kernel(x_ref, hg_ref, fo_ref, mod_ref, fgain_ref, w_ref, gain2_ref,
                    wrh_ref, wrl_ref, br_ref, xo_ref, hs_ref, gs_ref, dest_ref, cnt_ref):
    s, ls, d = x_ref.shape
    tm = s * ls
    fo = fo_ref[...].astype(F32)
    fn = fo * lax.rsqrt(jnp.mean(fo * fo, axis=-1, keepdims=True) + EPS) * fgain_ref[...]
    mixed = (jnp.dot(hg_ref[...].reshape(tm, HG_WIDTH), w_ref[0:HG_WIDTH, :],
                     preferred_element_type=F32)
             + jnp.dot(fn.reshape(tm, FOX_WIDTH).astype(BF16), w_ref[HG_WIDTH:, :],
                       preferred_element_type=F32))
    x = x_ref[...] + mod_ref[:, 2:3, :] * mixed.reshape(s, ls, d)
    xo_ref[...] = x
    y = x * lax.rsqrt(jnp.mean(x * x, axis=-1, keepdims=True) + EPS) * gain2_ref[...]
    h2 = (y * (1.0 + mod_ref[:, 4:5, :]) + mod_ref[:, 3:4, :]).reshape(tm, d)
    h2b = h2.astype(BF16)

    nt = (((1,), (1,)), ((), ()))
    logits = lax.dot_general(wrh_ref[...], h2b, nt, preferred_element_type=F32) \
        + lax.dot_general(wrl_ref[...], h2b, nt, preferred_element_type=F32)
    z = jnp.exp(logits - jnp.max(logits, axis=0, keepdims=True))
    probs = z / jnp.sum(z, axis=0, keepdims=True)
    sel = probs + br_ref[...]
    rows = [sel[e:e + 1, :] for e in range(N_EXPERTS)]
    prow = [probs[e:e + 1, :] for e in range(N_EXPERTS)]

    def beats(a, ia, b_, ib):
        return jnp.where(a >= b_, 1.0, 0.0) if ia < ib else jnp.where(a > b_, 1.0, 0.0)

    top = []
    gscore = []
    for g in range(N_GROUPS):
        ids = range(g * GROUP_SIZE, (g + 1) * GROUP_SIZE)
        sc = jnp.zeros_like(rows[0])
        for e in ids:
            cnt = jnp.zeros_like(rows[0])
            for o in ids:
                if o != e:
                    cnt = cnt + beats(rows[o], o, rows[e], e)
            flag = jnp.where(cnt < TOP_K, 1.0, 0.0)
            top.append(flag)
            sc = sc + flag * rows[e]
        gscore.append(sc)
    chosen = []
    for g in range(N_GROUPS):
        cnt = jnp.zeros_like(rows[0])
        for o in range(N_GROUPS):
            if o != g:
                cnt = cnt + beats(gscore[o], o, gscore[g], g)
        chosen.append(jnp.where(cnt < 1.0, 1.0, 0.0))
    wts = [prow[e] * top[e] * chosen[e // GROUP_SIZE] for e in range(N_EXPERTS)]
    den = wts[0]
    for e in range(1, N_EXPERTS):
        den = den + wts[e]
    g4 = []
    for k in range(GROUP_SIZE):
        gk = wts[k]
        for g in range(1, N_GROUPS):
            gk = gk + wts[g * GROUP_SIZE + k]
        g4.append(gk / den)

    rr = hs_ref.shape[1]
    ki = lax.broadcasted_iota(jnp.int32, (tm, tm), 0)
    ji = lax.broadcasted_iota(jnp.int32, (tm, tm), 1)
    before = jnp.where(ki < ji, 1.0, 0.0).astype(BF16)
    chosen4 = jnp.concatenate(chosen, axis=0)
    rank = jnp.dot(chosen4.astype(BF16), before, preferred_element_type=F32)
    lane = lax.broadcasted_iota(jnp.int32, (1, LANES), 1)
    start = jnp.zeros((1, 1), F32)
    dest = jnp.zeros((1, tm), F32)
    cnt_row = jnp.zeros((1, LANES), F32)
    for g in range(N_GROUPS):
        n_g = jnp.sum(chosen[g], axis=1, keepdims=True)
        tiles_g = jnp.floor((n_g + (MOE_TILE - 1)) * (1.0 / MOE_TILE))
        dest = dest + chosen[g] * (start + rank[g:g + 1, :])
        cnt_row = cnt_row + jnp.where(lane == g, tiles_g, 0.0)
        start = start + tiles_g * MOE_TILE
    dest_i = dest.astype(jnp.int32)
    dest_ref[0] = dest_i
    cnt_ref[0] = cnt_row.astype(jnp.int32)
    perm = jnp.where(lax.broadcasted_iota(jnp.int32, (rr, tm), 0) == dest_i, 1.0, 0.0).astype(BF16)
    hs_ref[0] = jnp.dot(perm, h2b, preferred_element_type=F32).astype(BF16)
    g128 = jnp.concatenate(g4 + [jnp.zeros((LANES - GROUP_SIZE, tm), F32)], axis=0)
    ghi = g128.astype(BF16)
    glo = (g128 - ghi.astype(F32)).astype(BF16)
    gs_ref[0] = (lax.dot_general(perm, ghi, nt, preferred_element_type=F32)
                 + lax.dot_general(perm, glo, nt, preferred_element_type=F32))


def outproj(x, hg_out, fox_o, mod, fox_gain, w_out, gain2, wr_hi, wr_lo, b_router, s, ls):
    b, l, d = x.shape
    nblk = (b // s) * (l // ls)
    nl = l // ls
    tm = s * ls
    row = lambda i, j: (i, j, 0)
    const = lambda i, j: (0, 0)
    blk = lambda i, j: (i * nl + j, 0, 0)
    return pl.pallas_call(
        _outproj_kernel,
        out_shape=[jax.ShapeDtypeStruct((b, l, d), F32),
                   jax.ShapeDtypeStruct((nblk, MOE_ROWS, d), BF16),
                   jax.ShapeDtypeStruct((nblk, MOE_ROWS, LANES), F32),
                   jax.ShapeDtypeStruct((nblk, 1, tm), jnp.int32),
                   jax.ShapeDtypeStruct((nblk, 1, LANES), jnp.int32)],
        grid=(b // s, nl),
        in_specs=[pl.BlockSpec((s, ls, d), row),
                  pl.BlockSpec((s, ls, HG_WIDTH), row),
                  pl.BlockSpec((s, ls, FOX_WIDTH), row),
                  pl.BlockSpec((s, N_MOD, d), lambda i, j: (i, 0, 0)),
                  pl.BlockSpec((1, FOX_WIDTH), const),
                  pl.BlockSpec(w_out.shape, const, pipeline_mode=pl.Buffered(1)),
                  pl.BlockSpec((1, d), const),
                  pl.BlockSpec((N_EXPERTS, d), const),
                  pl.BlockSpec((N_EXPERTS, d), const),
                  pl.BlockSpec((N_EXPERTS, 1), const)],
        out_specs=[pl.BlockSpec((s, ls, d), row),
                   pl.BlockSpec((1, MOE_ROWS, d), blk),
                   pl.BlockSpec((1, MOE_ROWS, LANES), blk),
                   pl.BlockSpec((1, 1, tm), blk),
                   pl.BlockSpec((1, 1, LANES), blk)],
        compiler_params=_params(("parallel", "parallel")),
        name="outproj",
    )(x, hg_out, fox_o, mod, fox_gain, w_out, gain2, wr_hi, wr_lo, b_router)


def _moe_tables(cnt, rows_per_step):
    nblk = cnt.shape[0]
    tiles_per_step = rows_per_step // MOE_TILE
    max_tiles = nblk * (BLOCK_TOKENS // MOE_TILE + N_GROUPS)
    n_steps = -(-max_tiles // tiles_per_step) + N_GROUPS
    npair = N_GROUPS * nblk
    start_gm = (jnp.cumsum(cnt, axis=1) - cnt).T.reshape(-1)
    cnt_gm = cnt.T.reshape(-1)
    csum = jnp.cumsum(cnt_gm)
    tau = jnp.arange(max_tiles, dtype=jnp.int32)
    pair = jnp.minimum(jnp.sum((csum[None, :] <= tau[:, None]).astype(jnp.int32), axis=1), npair - 1)
    hot = (pair[:, None] == jnp.arange(npair, dtype=jnp.int32)[None, :]).astype(jnp.int32)
    k = tau - jnp.sum(hot * (csum - cnt_gm)[None, :], axis=1)
    valid = tau < csum[-1]
    tile_blk = jnp.where(valid, pair % nblk, 0).astype(jnp.int32)
    tile_row = jnp.where(valid, (jnp.sum(hot * start_gm[None, :], axis=1) + k) * MOE_TILE, 0).astype(jnp.int32)
    tot = jnp.sum(cnt, axis=0)
    gstart = jnp.cumsum(tot) - tot
    nsteps = (tot + tiles_per_step - 1) // tiles_per_step
    send = jnp.cumsum(nsteps)
    s = jnp.arange(n_steps, dtype=jnp.int32)
    sg = jnp.minimum(jnp.sum((send[None, :] <= s[:, None]).astype(jnp.int32), axis=1), N_GROUPS - 1)
    ghot = (sg[:, None] == jnp.arange(N_GROUPS, dtype=jnp.int32)[None, :]).astype(jnp.int32)
    pick = lambda v: jnp.sum(ghot * v[None, :], axis=1)
    first = pick(gstart) + (s - pick(send - nsteps)) * tiles_per_step
    num = jnp.clip(pick(gstart + tot) - first, 0, tiles_per_step)
    first = jnp.where(num > 0, first, 0)
    return sg.astype(jnp.int32), first.astype(jnp.int32), num.astype(jnp.int32), tile_blk, tile_row


def _moe_kernel(sg_ref, first_ref, num_ref, tblk_ref, trow_ref,
                hs_a, gs_a, hs_b, gs_b, wg_ref, wu_ref, wd_ref, yinit_a, yinit_b, ys_a, ys_b,
                hbuf, gbuf, ybuf, sem_in, sem_out):
    del sg_ref, yinit_a, yinit_b
    s = pl.program_id(0)
    ns = pl.num_programs(0)
    slot = lax.rem(s, 2)
    na = hs_a.shape[0]

    def tile(k):
        return pl.ds(pl.multiple_of(k * MOE_TILE, MOE_TILE), MOE_TILE)

    def src(step, k):
        t = first_ref[step] + k
        return tblk_ref[t], pl.ds(pl.multiple_of(trow_ref[t], MOE_TILE), MOE_TILE)

    class _Either:
        def __init__(self, blk, make):
            self.blk, self.make = blk, make

        def start(self):
            @pl.when(self.blk < na)
            def _():
                self.make(0, self.blk).start()

            @pl.when(self.blk >= na)
            def _():
                self.make(1, self.blk - na).start()

        def wait(self):
            self.make(0, 0).wait()

    def h_copy(step, sl, k):
        blk, rows = src(step, k)
        return _Either(blk, lambda g, b: pltpu.make_async_copy(
            (hs_a, hs_b)[g].at[b, rows, :], hbuf.at[sl, tile(k), :], sem_in.at[sl, 0]))

    def g_copy(step, sl, k):
        blk, rows = src(step, k)
        return _Either(blk, lambda g, b: pltpu.make_async_copy(
            (gs_a, gs_b)[g].at[b, rows, :], gbuf.at[sl, tile(k), :], sem_in.at[sl, 1]))

    def y_copy(step, sl, k):
        blk, rows = src(step, k)
        return _Either(blk, lambda g, b: pltpu.make_async_copy(
            ybuf.at[sl, tile(k), :], (ys_a, ys_b)[g].at[b, rows, :], sem_out.at[sl]))

    def each_tile(step, fn):
        def body(k, c):
            fn(k)
            return c
        lax.fori_loop(0, num_ref[step], body, 0)

    def start_gather(step, sl):
        def fn(k):
            h_copy(step, sl, k).start()
            g_copy(step, sl, k).start()
        each_tile(step, fn)

    def wait_gather(step, sl):
        def fn(k):
            h_copy(step, sl, k).wait()
            g_copy(step, sl, k).wait()
        each_tile(step, fn)

    @pl.when(s == 0)
    def _():
        hbuf[...] = jnp.zeros_like(hbuf)
        gbuf[...] = jnp.zeros_like(gbuf)
        start_gather(0, 0)

    wait_gather(s, slot)

    @pl.when(s + 1 < ns)
    def _():
        start_gather(s + 1, 1 - slot)

    @pl.when(s >= 2)
    def _():
        each_tile(s - 2, lambda k: y_copy(s - 2, slot, k).wait())

    @pl.when(num_ref[s] > 0)
    def _():
        h = hbuf[slot]
        g = gbuf[slot]
        acc = jnp.zeros((h.shape[0], wd_ref.shape[2]), F32)
        for k in range(GROUP_SIZE):
            a = jnp.dot(h, wg_ref[k].astype(BF16), preferred_element_type=F32)
            u = jnp.dot(h, wu_ref[k].astype(BF16), preferred_element_type=F32)
            he = (_silu(a) * u * g[:, k:k + 1]).astype(BF16)
            acc = acc + jnp.dot(he, wd_ref[k].astype(BF16), preferred_element_type=F32)
        ybuf[slot] = acc.astype(ybuf.dtype)

    each_tile(s, lambda k: y_copy(s, slot, k).start())

    @pl.when(s == ns - 1)
    def _():
        each_tile(s, lambda k: y_copy(s, slot, k).wait())

        @pl.when(s >= 1)
        def _():
            each_tile(s - 1, lambda k: y_copy(s - 1, 1 - slot, k).wait())


def moe_sparse(hs_a, gs_a, hs_b, gs_b, cnt, wg, wu, wd, layer):
    _, rr, d = hs_a.shape
    de = wg.shape[2]
    rows = MOE_STEP_ROWS
    sg, first, num, tile_blk, tile_row = _moe_tables(cnt, rows)
    n_steps = sg.shape[0]
    wmap = lambda s, sg, *_: (layer * N_GROUPS + sg[s], 0, 0)
    grid_spec = pltpu.PrefetchScalarGridSpec(
        num_scalar_prefetch=5,
        grid=(n_steps,),
        in_specs=[pl.BlockSpec(memory_space=pl.ANY)] * 4
        + [pl.BlockSpec((GROUP_SIZE, d, de), wmap, pipeline_mode=pl.Buffered(1)),
           pl.BlockSpec((GROUP_SIZE, d, de), wmap, pipeline_mode=pl.Buffered(1)),
           pl.BlockSpec((GROUP_SIZE, de, d), wmap, pipeline_mode=pl.Buffered(1))]
        + [pl.BlockSpec(memory_space=pl.ANY)] * 2,
        out_specs=[pl.BlockSpec(memory_space=pl.ANY)] * 2,
        scratch_shapes=[pltpu.VMEM((2, rows, d), BF16),
                        pltpu.VMEM((2, rows, LANES), F32),
                        pltpu.VMEM((2, rows, d), BF16),
                        pltpu.SemaphoreType.DMA((2, 2)),
                        pltpu.SemaphoreType.DMA((2,))])
    return pl.pallas_call(
        _moe_kernel,
        out_shape=[jax.ShapeDtypeStruct(hs_a.shape, BF16), jax.ShapeDtypeStruct(hs_b.shape, BF16)],
        grid_spec=grid_spec,
        input_output_aliases={12: 0, 13: 1},
        compiler_params=_params(("arbitrary",)),
        name="moe",
    )(sg, first, num, tile_blk, tile_row, hs_a, gs_a, hs_b, gs_b, wg, wu, wd,
      jnp.zeros(hs_a.shape, BF16), jnp.zeros(hs_b.shape, BF16))


def _combine_kernel(x_ref, ys_ref, dest_ref, mod_ref, fgain_ref, o_ref, *, final):
    s, ls, d = x_ref.shape
    tm = s * ls
    rr = ys_ref.shape[1]
    dcol = jnp.broadcast_to(dest_ref[0].astype(F32), (LANES, tm)).T
    lane = lax.broadcasted_iota(jnp.int32, (tm, LANES), 1).astype(F32)
    unperm = jnp.concatenate(
        [jnp.where(dcol == lane + float(c * LANES), 1.0, 0.0).astype(BF16) for c in range(rr // LANES)],
        axis=1)
    y = jnp.dot(unperm, ys_ref[0], preferred_element_type=F32)
    x = x_ref[...] + mod_ref[:, 5:6, :] * y.reshape(s, ls, d)
    if final:
        x = x * lax.rsqrt(jnp.mean(x * x, axis=-1, keepdims=True) + EPS) * fgain_ref[...]
    o_ref[...] = x


def combine(x, ys, dest, mod, final_gain, s, ls, final):
    b, l, d = x.shape
    nl = l // ls
    tm = s * ls
    row = lambda i, j: (i, j, 0)
    blk = lambda i, j: (i * nl + j, 0, 0)
    return pl.pallas_call(
        functools.partial(_combine_kernel, final=final),
        out_shape=jax.ShapeDtypeStruct((b, l, d), F32),
        grid=(b // s, nl),
        in_specs=[pl.BlockSpec((s, ls, d), row),
                  pl.BlockSpec((1, ys.shape[1], d), blk),
                  pl.BlockSpec((1, 1, tm), blk),
                  pl.BlockSpec((s, N_MOD, d), lambda i, j: (i, 0, 0)),
                  pl.BlockSpec((1, d), lambda i, j: (0, 0))],
        out_specs=pl.BlockSpec((s, ls, d), row),
        compiler_params=_params(("parallel", "parallel")),
        name="combine",
    )(x, ys, dest, mod, final_gain)


def _tile(b, l):
    ls = min(l, BLOCK_TOKENS)
    s = BLOCK_TOKENS // ls
    assert s * ls == BLOCK_TOKENS and b % s == 0 and l % ls == 0
    assert BLOCK_TOKENS + N_GROUPS * MOE_TILE <= MOE_ROWS
    return s, ls


def _mixer(x, mod, l, p, hg_state0, fox_past, kv_prev):
    depth = p['norm_mix_gain'].shape[0]
    b, seq, d = x.shape
    s, ls = _tile(b, seq)
    sample = fox_past is not None
    hg_in, fq, kbuf, vbuf, *k16v16, ff = inproj(x, mod, p['norm_mix_gain'][l], p['w_in'][l], s, ls,
                                                l, depth, kv_prev, sample)

    hg_out, hg_state = hgrn(hg_in, hg_state0, p['hg_lower_bounds'], p['hg_norm_gain'][l], l)

    if not sample:
        tq = min(seq, BLOCK_TOKENS)
        hg_out, fq = lax.optimization_barrier((hg_out, fq))
        qp, kp, vt, lf_pad = foxpack(fq, kbuf, vbuf, l, ff, p['fox_f_bias_pad'][l], tq)
        fox_o = fox_prompt(qp, kp, vt, tq)
        logf = lf_pad[:, :, :FOX_HEADS]
    else:
        kcache, vcache, plogf = fox_past
        past = kcache.shape[4]
        fft = jnp.swapaxes(ff[:, :, :FOX_HEADS], 1, 2)
        logft, _ = forget_cumsum(fft, p['fox_f_bias'][l], True, seq)
        tot = past + seq
        pad = (-tot) % LANES
        allt = jnp.concatenate([jnp.swapaxes(plogf[l], 1, 2), logft,
                                jnp.zeros((b, FOX_HEADS, pad), F32)], axis=2)
        _, ct = forget_cumsum(allt, p['fox_f_bias'][l], False, LANES)
        ct = ct * LOG2E
        cq = jnp.swapaxes(ct[:, :, past:tot], 1, 2)
        fox_o = fox_sample(fq, kcache, vcache, l, k16v16[0], k16v16[1], cq, ct)
        logf = jnp.swapaxes(logft, 1, 2)

    routed = outproj(x, hg_out, fox_o, mod, p['fox_out_gain'][l], p['w_out'][l],
                     p['norm_ffn_gain'][l], p['wr_hi'], p['wr_lo'], p['b_router'], s, ls)
    return routed, (kbuf, vbuf), hg_state, logf


def _ffn(routed_p, routed_s, mod_p, mod_s, l, p, final):
    xm_p, hs_p, gs_p, dest_p, cnt_p = routed_p
    xm_s, hs_s, gs_s, dest_s, cnt_s = routed_s
    cnt = jnp.concatenate([cnt_p, cnt_s], axis=0)[:, 0, :N_GROUPS]
    ys_p, ys_s = moe_sparse(hs_p, gs_p, hs_s, gs_s, cnt, p['w_exp_gate'], p['w_exp_up'], p['w_exp_down'], l)
    outs = []
    for xm, ys, dest, mod in ((xm_p, ys_p, dest_p, mod_p), (xm_s, ys_s, dest_s, mod_s)):
        s, ls = _tile(xm.shape[0], xm.shape[1])
        outs.append(combine(xm, ys, dest, mod, p['final_norm_gain'], s, ls, final))
    return outs


def kernel(x_prompt, x_sample, cache_fox_k, cache_fox_v, cache_fox_logf, state_hgrn, c_prompt, c_sample,
           norm_mix_gain, norm_ffn_gain, w_ada, b_ada, w_in, hg_lower_bounds, hg_norm_gain,
           fox_f_bias, fox_out_gain, w_out, w_router, b_router, w_exp_gate, w_exp_up, w_exp_down,
           final_norm_gain):
    depth, d = norm_mix_gain.shape
    bp = x_prompt.shape[0]
    n_in = w_in.shape[2]
    n_pad = 4 * HG_WIDTH + 3 * FOX_WIDTH + LANES - n_in
    wr_t = w_router.T
    wr_hi = wr_t.astype(BF16)
    p = {
        'norm_mix_gain': norm_mix_gain.reshape(depth, 1, d),
        'norm_ffn_gain': norm_ffn_gain.reshape(depth, 1, d),
        'w_in': jnp.pad(w_in, ((0, 0), (0, 0), (0, n_pad))).astype(BF16),
        'hg_lower_bounds': hg_lower_bounds,
        'hg_norm_gain': hg_norm_gain.reshape(depth, 1, HG_WIDTH),
        'fox_f_bias': fox_f_bias.reshape(depth, FOX_HEADS, 1),
        'fox_f_bias_pad': jnp.pad(fox_f_bias, ((0, 0), (0, LANES - FOX_HEADS))).reshape(depth, 1, LANES),
        'fox_out_gain': fox_out_gain.reshape(depth, 1, FOX_WIDTH),
        'w_out': w_out.astype(BF16),
        'wr_hi': wr_hi,
        'wr_lo': (wr_t - wr_hi.astype(F32)).astype(BF16),
        'b_router': b_router.reshape(N_EXPERTS, 1),
        'w_exp_gate': w_exp_gate.reshape((depth * N_EXPERTS,) + w_exp_gate.shape[2:]),
        'w_exp_up': w_exp_up.reshape((depth * N_EXPERTS,) + w_exp_up.shape[2:]),
        'w_exp_down': w_exp_down.reshape((depth * N_EXPERTS,) + w_exp_down.shape[2:]),
        'final_norm_gain': final_norm_gain.reshape(1, d),
    }
    mods = ada_mod(jnp.concatenate([c_prompt, c_sample], axis=0), w_ada, b_ada)
    mods = mods.reshape(depth, -1, N_MOD, d)
    bs, lsq = x_sample.shape[:2]
    lp = x_prompt.shape[1]
    past = cache_fox_k.shape[2]
    fox_past = (jnp.transpose(cache_fox_k, (0, 1, 3, 4, 2)), jnp.transpose(cache_fox_v, (0, 1, 3, 4, 2)),
                cache_fox_logf)

    xp, xs = x_prompt, x_sample
    kv_p = kv_s = None
    st_p, st_s, lf_p, lf_s = [], [], [], []
    zero_state = jnp.zeros((bp, HG_HEADS, HG_DIM, HG_DIM), F32)
    for l in range(depth):
        routed_p, kv_p, st, lf = _mixer(xp, mods[l, :bp], l, p, zero_state, None, kv_p)
        st_p.append(st)
        lf_p.append(lf)
        routed_s, kv_s, st, lf = _mixer(xs, mods[l, bp:], l, p, state_hgrn[l], fox_past, kv_s)
        st_s.append(st)
        lf_s.append(lf)
        xp, xs = _ffn(routed_p, routed_s, mods[l, :bp], mods[l, bp:], l, p, l == depth - 1)
    heads = lambda a, b, l: a.reshape(depth, b, l, FOX_HEADS, FOX_DIM)
    return (xp, xs,
            jnp.stack(st_p), heads(kv_p[0], bp, lp), heads(kv_p[1], bp, lp), jnp.stack(lf_p),
            jnp.stack(st_s), heads(kv_s[0], bs, lsq), heads(kv_s[1], bs, lsq), jnp.stack(lf_s))
```

```python
import functools
import math

import numpy as np
import jax
import jax.numpy as jnp
from jax import lax
from jax.experimental import pallas as pl
from jax.experimental.pallas import tpu as pltpu

HG_HEADS = 4
HG_DIM = 128
HG_WIDTH = HG_HEADS * HG_DIM
FOX_HEADS = 8
FOX_DIM = 64
FOX_WIDTH = FOX_HEADS * FOX_DIM
N_EXPERTS = 16
N_GROUPS = 4
GROUP_SIZE = N_EXPERTS // N_GROUPS
TOP_K = 2
N_MOD = 6
EPS = 1e-6

LANES = 128
HG_CHUNK = 128
HG_SUB = 16
HG_STREAMS = 8
MOE_TILE = 16
BLOCK_TOKENS = 512
MOE_ROWS = 640
MOE_STEP_ROWS = 1024
VMEM_LIMIT = 56 * 1024 * 1024

F32 = jnp.float32
BF16 = jnp.bfloat16
NEG = -1e30
LOG2E = math.log2(math.e)


def _params(sem, vmem=VMEM_LIMIT):
    return pltpu.CompilerParams(dimension_semantics=sem, vmem_limit_bytes=vmem)


def _split3(x):
    hi = x.astype(BF16)
    r1 = x - hi.astype(F32)
    mid = r1.astype(BF16)
    lo = (r1 - mid.astype(F32)).astype(BF16)
    return hi, mid, lo


def _sigmoid(x):
    return 1.0 / (1.0 + jnp.exp(-x))


def _log_sigmoid(x):
    return jnp.minimum(x, 0.0) - jnp.log(1.0 + jnp.exp(-jnp.abs(x)))


def _silu(x):
    return x * _sigmoid(x)


def _ada_kernel(c_ref, w_ref, b_ref, o_ref):
    s = _silu(c_ref[...]).astype(BF16)
    o_ref[0] = jnp.dot(s, w_ref[0].astype(BF16), preferred_element_type=F32) + b_ref[0]


def ada_mod(c, w_ada, b_ada):
    depth, d, n = w_ada.shape
    nb = c.shape[0]
    tn = 1536
    return pl.pallas_call(
        _ada_kernel,
        out_shape=jax.ShapeDtypeStruct((depth, nb, n), F32),
        grid=(depth, n // tn),
        in_specs=[pl.BlockSpec((nb, d), lambda l, j: (0, 0)),
                  pl.BlockSpec((1, d, tn), lambda l, j: (l, 0, j)),
                  pl.BlockSpec((1, 1, tn), lambda l, j: (l, 0, j))],
        out_specs=pl.BlockSpec((1, nb, tn), lambda l, j: (l, 0, j)),
        compiler_params=_params(("arbitrary", "arbitrary")),
        name="ada_mod",
    )(c, w_ada, b_ada.reshape(depth, 1, n))


def _inproj_kernel(x_ref, mod_ref, gain_ref, w_ref, *refs, has_prev, emit16):
    outs = refs[2:] if has_prev else refs
    hg_ref, fq_ref, fk32_ref, fv32_ref = outs[:4]
    ff_ref = outs[-1]
    s, ls, d = x_ref.shape
    x = x_ref[...]
    y = x * lax.rsqrt(jnp.mean(x * x, axis=-1, keepdims=True) + EPS) * gain_ref[...]
    h = y * (1.0 + mod_ref[:, 1:2, :]) + mod_ref[:, 0:1, :]
    hb = h.reshape(s * ls, d).astype(BF16)

    def proj(lo, hi):
        return jnp.dot(hb, w_ref[:, lo:hi], preferred_element_type=F32)

    c0 = 4 * HG_WIDTH
    hg_ref[...] = proj(0, c0).reshape(s, ls, c0)
    fq = proj(c0, c0 + FOX_WIDTH) * (FOX_DIM ** -0.5 * LOG2E)
    fq_ref[...] = fq.reshape(s, ls, FOX_WIDTH).astype(BF16)
    fk = proj(c0 + FOX_WIDTH, c0 + 2 * FOX_WIDTH).reshape(s, ls, FOX_WIDTH)
    fv = proj(c0 + 2 * FOX_WIDTH, c0 + 3 * FOX_WIDTH).reshape(s, ls, FOX_WIDTH)
    for slot in range(fk32_ref.shape[0]):
        fk32_ref[slot] = fk
        fv32_ref[slot] = fv
    if emit16:
        outs[4][...] = fk.astype(BF16)
        outs[5][...] = fv.astype(BF16)
    ff_ref[...] = proj(c0 + 3 * FOX_WIDTH, c0 + 3 * FOX_WIDTH + LANES).reshape(s, ls, LANES)


def inproj(x, mod, gain, w_pad, s, ls, layer, depth, kv_prev, emit16):
    b, l, d = x.shape
    n = w_pad.shape[1]
    row = lambda i, j: (i, j, 0)
    lrow = lambda i, j: (layer, i, j, 0)
    shapes = [jax.ShapeDtypeStruct((b, l, 4 * HG_WIDTH), F32),
              jax.ShapeDtypeStruct((b, l, FOX_WIDTH), BF16),
              jax.ShapeDtypeStruct((depth, b, l, FOX_WIDTH), F32),
              jax.ShapeDtypeStruct((depth, b, l, FOX_WIDTH), F32)]
    slots = depth if kv_prev is None else 1
    if kv_prev is None:
        assert layer == 0
    specs = [pl.BlockSpec((s, ls, 4 * HG_WIDTH), row),
             pl.BlockSpec((s, ls, FOX_WIDTH), row),
             pl.BlockSpec((slots, s, ls, FOX_WIDTH), lrow),
             pl.BlockSpec((slots, s, ls, FOX_WIDTH), lrow)]
    if emit16:
        shapes += [jax.ShapeDtypeStruct((b, l, FOX_WIDTH), BF16)] * 2
        specs += [pl.BlockSpec((s, ls, FOX_WIDTH), row)] * 2
    shapes.append(jax.ShapeDtypeStruct((b, l, LANES), F32))
    specs.append(pl.BlockSpec((s, ls, LANES), row))
    in_specs = [pl.BlockSpec((s, ls, d), row),
                pl.BlockSpec((s, N_MOD, d), lambda i, j: (i, 0, 0)),
                pl.BlockSpec((1, d), lambda i, j: (0, 0)),
                pl.BlockSpec((d, n), lambda i, j: (0, 0), pipeline_mode=pl.Buffered(1))]
    args = [x, mod, gain, w_pad]
    aliases = {}
    if kv_prev is not None:
        in_specs += [pl.BlockSpec(memory_space=pl.ANY)] * 2
        args += list(kv_prev)
        aliases = {4: 2, 5: 3}
    return pl.pallas_call(
        functools.partial(_inproj_kernel, has_prev=kv_prev is not None, emit16=emit16),
        out_shape=shapes,
        grid=(b // s, l // ls),
        in_specs=in_specs,
        out_specs=specs,
        input_output_aliases=aliases,
        compiler_params=_params(("parallel", "parallel")),
        name="inproj",
    )(*args)


def _cumsum_kernel(x_ref, bias_ref, lf_ref, c_ref, carry_ref, *, apply_ls):
    @pl.when(pl.program_id(1) == 0)
    def _():
        carry_ref[...] = jnp.zeros_like(carry_ref)

    bb, nh, tc = x_ref.shape
    x = x_ref[...].reshape(bb * nh, tc)
    lf = _log_sigmoid(x + bias_ref[...]) if apply_ls else x
    lf_ref[...] = lf.reshape(bb, nh, tc)
    r = lax.broadcasted_iota(jnp.int32, (tc, tc), 0)
    c = lax.broadcasted_iota(jnp.int32, (tc, tc), 1)
    tri = jnp.where(r <= c, 1.0, 0.0).astype(BF16)
    hi, mid, lo = _split3(lf)
    tot = (jnp.dot(hi, tri, preferred_element_type=F32)
           + jnp.dot(mid, tri, preferred_element_type=F32)
           + jnp.dot(lo, tri, preferred_element_type=F32)) + carry_ref[:, 0:1]
    c_ref[...] = tot.reshape(bb, nh, tc)
    carry_ref[...] = jnp.broadcast_to(tot[:, tc - 1:tc], carry_ref.shape)


def forget_cumsum(xt, bias, apply_ls, tc):
    b, h, l = xt.shape
    blk = pl.BlockSpec((b, h, tc), lambda i, j: (0, 0, j))
    return pl.pallas_call(
        functools.partial(_cumsum_kernel, apply_ls=apply_ls),
        out_shape=[jax.ShapeDtypeStruct((b, h, l), F32)] * 2,
        grid=(1, l // tc),
        in_specs=[blk, pl.BlockSpec((b * h, 1), lambda i, j: (0, 0))],
        out_specs=[blk, blk],
        scratch_shapes=[pltpu.VMEM((b * h, LANES), F32)],
        compiler_params=_params(("arbitrary", "arbitrary")),
        name="forget_cumsum",
    )(xt, jnp.tile(bias, (b, 1)))


HG_HALVES = (8, 4, 2, 1)


def _hgrn_sum_matrix():
    c = HG_CHUNK
    t = np.arange(c)[:, None]
    u = np.arange(c)[None, :]
    mats = [(u <= t)]
    for w in HG_HALVES:
        pos = t % (2 * w)
        mid = t - pos + w - 1
        upper = pos >= w
        mats.append(np.where(upper, (u > mid) & (u <= t), (u > t) & (u <= mid)))
    return jnp.asarray(np.concatenate(mats, axis=0).astype(np.float32), BF16)


def _hgrn_kernel(hin_ref, s0_ref, lbp_ref, gain_ref, sums_ref, out_ref, sfin_ref, st_ref, *, layer):
    t = pl.program_id(1)
    nt = pl.num_programs(1)
    c = HG_CHUNK
    nsub = c // HG_SUB
    nb = hin_ref.shape[0]
    rows_in = hin_ref.shape[1]

    def pad_rows(a):
        if rows_in == c:
            return a
        return jnp.concatenate([a, jnp.zeros((c - rows_in, a.shape[1]), a.dtype)], axis=0)

    @pl.when(t == 0)
    def _():
        for bi in range(nb):
            for h in range(HG_HEADS):
                st_ref[bi, h] = s0_ref[bi, h].T

    lbp = lbp_ref[...]
    e = jnp.exp(lbp - jnp.max(lbp, axis=0, keepdims=True))
    p = e / jnp.sum(e, axis=0, keepdims=True)
    acc = p[0:1]
    first = acc
    for i in range(1, layer + 1):
        acc = acc + p[i:i + 1]
    lb_all = acc - first

    ri = lax.broadcasted_iota(jnp.int32, (c, c), 0)
    ci = lax.broadcasted_iota(jnp.int32, (c, c), 1)
    ones = jnp.ones((HG_DIM, HG_DIM), BF16)
    nt_dims = (((1,), (1,)), ((), ()))

    streams = [(bi, h) for bi in range(nb) for h in range(HG_HEADS)]
    q, kk, v, b, sums, att = {}, {}, {}, {}, {}, {}
    for bi, h in streams:
        lb = lb_all[:, h * HG_DIM:(h + 1) * HG_DIM]
        hq = hin_ref[bi, :, h * HG_DIM:(h + 1) * HG_DIM]
        hf = hin_ref[bi, :, HG_WIDTH + h * HG_DIM:HG_WIDTH + (h + 1) * HG_DIM]
        hi = hin_ref[bi, :, 2 * HG_WIDTH + h * HG_DIM:2 * HG_WIDTH + (h + 1) * HG_DIM]
        q[bi, h] = pad_rows(_silu(hq) * (HG_DIM ** -0.5))
        a1 = jnp.log(lb)
        e = jnp.exp(-jnp.abs(hf))
        den = 1.0 + e
        a2 = jnp.log1p(-lb) + (jnp.minimum(hf, 0.0) - jnp.log(den))
        mx = jnp.maximum(a1, a2)
        lf = pad_rows(mx + jnp.log(jnp.exp(a1 - mx) + jnp.exp(a2 - mx)))
        kk[bi, h] = pad_rows((1.0 - lb) * (jnp.where(hf > 0.0, e, 1.0) / den))
        v[bi, h] = pad_rows(hi)
        lhi, lmid, llo = _split3(lf)
        sums[bi, h] = (jnp.dot(sums_ref[...], lhi, preferred_element_type=F32)
                       + jnp.dot(sums_ref[...], lmid, preferred_element_type=F32)
                       + jnp.dot(sums_ref[...], llo, preferred_element_type=F32))
        b[bi, h] = sums[bi, h][0:c]

    for key in streams:
        att[key] = jnp.where(ri == ci, jnp.dot((q[key] * kk[key]).astype(BF16), ones,
                                               preferred_element_type=F32), 0.0)
    for lvl, w in enumerate(HG_HALVES):
        shift = (2 * w).bit_length() - 1
        pair = ((ri >> shift) == (ci >> shift)) & ((ri & (2 * w - 1)) >= w) & ((ci & (2 * w - 1)) < w)
        for key in streams:
            dec = jnp.exp(sums[key][(lvl + 1) * c:(lvl + 2) * c])
            sc = lax.dot_general((q[key] * dec).astype(BF16), (kk[key] * dec).astype(BF16), nt_dims,
                                 preferred_element_type=F32)
            att[key] = att[key] + jnp.where(pair, sc, 0.0)

    khat = {key: None for key in streams}
    r_prev = {key: jnp.zeros((1, HG_DIM), F32) for key in streams}
    qhat_blocks = {key: [] for key in streams}
    cross = {key: [] for key in streams}
    for i in range(nsub):
        r0 = i * HG_SUB
        for key in streams:
            bb = b[key][r0:r0 + HG_SUB]
            r_next = b[key][r0 + HG_SUB - 1:r0 + HG_SUB]
            qt = q[key][r0:r0 + HG_SUB] * jnp.exp(bb - r_prev[key])
            if khat[key] is None:
                cross[key].append(jnp.zeros((HG_SUB, c), F32))
            else:
                kfull = jnp.concatenate([khat[key], jnp.zeros((c - r0, HG_DIM), F32)], axis=0)
                cross[key].append(lax.dot_general(qt.astype(BF16), kfull.astype(BF16), nt_dims,
                                                  preferred_element_type=F32))
            qhat_blocks[key].append(qt * jnp.exp(r_prev[key]))
            kt = kk[key][r0:r0 + HG_SUB] * jnp.exp(r_next - bb)
            if khat[key] is None:
                khat[key] = kt
            else:
                khat[key] = jnp.concatenate([khat[key] * jnp.exp(r_next - r_prev[key]), kt], axis=0)
            r_prev[key] = r_next

    for bi, h in streams:
        key = (bi, h)
        cols = slice(h * HG_DIM, (h + 1) * HG_DIM)
        st = st_ref[bi, h]
        a = att[key] + jnp.concatenate(cross[key], axis=0)
        qhat = jnp.concatenate(qhat_blocks[key], axis=0).astype(BF16)
        o = jnp.dot(a.astype(BF16), v[key].astype(BF16), preferred_element_type=F32)
        o = o + lax.dot_general(qhat, st.astype(BF16), nt_dims, preferred_element_type=F32)
        st_ref[bi, h] = st * jnp.exp(r_prev[key]) + jnp.dot(
            v[key].T.astype(BF16), khat[key].astype(BF16), preferred_element_type=F32)

        hg = hin_ref[bi, :, 3 * HG_WIDTH + h * HG_DIM:3 * HG_WIDTH + (h + 1) * HG_DIM]
        o = o[:rows_in]
        o = o * lax.rsqrt(jnp.mean(o * o, axis=-1, keepdims=True) + EPS)
        o = o * gain_ref[:, cols] * _silu(hg)
        out_ref[bi, :, cols] = o.astype(out_ref.dtype)

    @pl.when(t == nt - 1)
    def _():
        for bi, h in streams:
            sfin_ref[bi, h] = st_ref[bi, h].T


def hgrn(hin, s0, lbp, gain, layer):
    b, l, w = hin.shape
    c = min(HG_CHUNK, l)
    assert l % c == 0 and c % HG_SUB == 0
    nt = l // c
    nb = HG_STREAMS
    assert b % nb == 0
    sums = _hgrn_sum_matrix()
    return pl.pallas_call(
        functools.partial(_hgrn_kernel, layer=layer),
        out_shape=[jax.ShapeDtypeStruct((b, l, HG_WIDTH), BF16),
                   jax.ShapeDtypeStruct((b, HG_HEADS, HG_DIM, HG_DIM), F32)],
        grid=(b // nb, nt),
        in_specs=[pl.BlockSpec((nb, c, w), lambda i, j: (i, j, 0)),
                  pl.BlockSpec((nb, HG_HEADS, HG_DIM, HG_DIM), lambda i, j: (i, 0, 0, 0)),
                  pl.BlockSpec(lbp.shape, lambda i, j: (0, 0)),
                  pl.BlockSpec((1, HG_WIDTH), lambda i, j: (0, 0)),
                  pl.BlockSpec(sums.shape, lambda i, j: (0, 0))],
        out_specs=[pl.BlockSpec((nb, c, HG_WIDTH), lambda i, j: (i, j, 0)),
                   pl.BlockSpec((nb, HG_HEADS, HG_DIM, HG_DIM), lambda i, j: (i, 0, 0, 0))],
        scratch_shapes=[pltpu.VMEM((nb, HG_HEADS, HG_DIM, HG_DIM), F32)],
        compiler_params=_params(("parallel", "arbitrary")),
        name="hgrn",
    )(hin, s0, lbp, gain, sums)


AUG_ONE = FOX_DIM
AUG_NEG = FOX_DIM + 3
ONE_LANE = FOX_HEADS
FOX_GROUP = 4
FOX_STREAMS = 4
FOX_VROWS = FOX_DIM + 16


def _placement():
    pk = np.zeros((3 * LANES, FOX_HEADS * LANES), np.float32)
    pq = np.zeros((3 * LANES, FOX_HEADS * LANES), np.float32)
    for h in range(FOX_HEADS):
        for part in range(3):
            pk[part * LANES + h, h * LANES + AUG_NEG + part] = -1.0
            pq[part * LANES + h, h * LANES + AUG_ONE + part] = 1.0
            pk[ONE_LANE, h * LANES + AUG_ONE + part] = 1.0
            pq[ONE_LANE, h * LANES + AUG_NEG + part] = 1.0
    return jnp.asarray(pk, BF16), jnp.asarray(pq, BF16)


def _foxpack_kernel(q_ref, k_ref, v_ref, ff_ref, bias_ref, pk_ref, pq_ref,
                    qp_ref, kp_ref, vt_ref, lf_ref, carry_ref):
    @pl.when(pl.program_id(1) == 0)
    def _():
        carry_ref[...] = jnp.zeros_like(carry_ref)

    tm = q_ref.shape[1]
    lane = lax.broadcasted_iota(jnp.int32, (tm, LANES), 1)
    lf = jnp.where(lane < FOX_HEADS, _log_sigmoid(ff_ref[0] + bias_ref[...]), 0.0)
    lf_ref[0] = lf
    ri = lax.broadcasted_iota(jnp.int32, (tm, tm), 0)
    ci = lax.broadcasted_iota(jnp.int32, (tm, tm), 1)
    tril = jnp.where(ci <= ri, 1.0, 0.0).astype(BF16)
    hi, mid, lo = _split3(lf)
    c = (jnp.dot(tril, hi, preferred_element_type=F32)
         + jnp.dot(tril, mid, preferred_element_type=F32)
         + jnp.dot(tril, lo, preferred_element_type=F32)) + carry_ref[0:1, :]
    carry_ref[...] = jnp.broadcast_to(c[tm - 1:tm, :], carry_ref.shape)

    chi, cmid, clo = _split3(c * LOG2E)
    chi = jnp.where(lane == ONE_LANE, 1.0, chi.astype(F32)).astype(BF16)
    cterms = jnp.concatenate([chi, cmid, clo], axis=1)
    augk = jnp.dot(cterms, pk_ref[...], preferred_element_type=F32)
    augq = jnp.dot(cterms, pq_ref[...], preferred_element_type=F32)
    for h in range(FOX_HEADS):
        blk = slice((h // 2) * LANES, (h // 2 + 1) * LANES)
        hcols = slice(h * LANES, (h + 1) * LANES)
        kh = k_ref[0, 0, :, blk]
        qh = q_ref[0, :, blk].astype(F32)
        if h % 2:
            kh = pltpu.roll(kh, FOX_DIM, axis=1)
            qh = pltpu.roll(qh, FOX_DIM, axis=1)
        kp_ref[0, h] = jnp.where(lane < FOX_DIM, kh, augk[:, hcols]).astype(BF16)
        qp_ref[0, h] = jnp.where(lane < FOX_DIM, qh, augq[:, hcols]).astype(BF16)
    vt = v_ref[0, 0].T
    tail = jnp.where(lax.broadcasted_iota(jnp.int32, (FOX_VROWS - FOX_DIM, tm), 0) == 0, 1.0, 0.0)
    for h in range(FOX_HEADS):
        vt_ref[0, h] = jnp.concatenate([vt[h * FOX_DIM:(h + 1) * FOX_DIM], tail], axis=0).astype(BF16)


def foxpack(fq, kbuf, vbuf, layer, ff, bias_pad, tm):
    b, l, w = fq.shape
    pk, pq = _placement()
    row = lambda i, j: (i, j, 0)
    lrow = lambda i, j: (layer, i, j, 0)
    const = lambda i, j: (0, 0)
    head_blk = pl.BlockSpec((1, FOX_HEADS, tm, LANES), lambda i, j: (i, 0, j, 0))
    return pl.pallas_call(
        _foxpack_kernel,
        out_shape=[jax.ShapeDtypeStruct((b, FOX_HEADS, l, LANES), BF16),
                   jax.ShapeDtypeStruct((b, FOX_HEADS, l, LANES), BF16),
                   jax.ShapeDtypeStruct((b, FOX_HEADS, FOX_VROWS, l), BF16),
                   jax.ShapeDtypeStruct((b, l, LANES), F32)],
        grid=(b, l // tm),
        in_specs=[pl.BlockSpec((1, tm, w), row),
                  pl.BlockSpec((1, 1, tm, w), lrow),
                  pl.BlockSpec((1, 1, tm, w), lrow),
                  pl.BlockSpec((1, tm, LANES), row),
                  pl.BlockSpec((1, LANES), const),
                  pl.BlockSpec(pk.shape, const),
                  pl.BlockSpec(pq.shape, const)],
        out_specs=[head_blk, head_blk,
                   pl.BlockSpec((1, FOX_HEADS, FOX_VROWS, tm), lambda i, j: (i, 0, 0, j)),
                   pl.BlockSpec((1, tm, LANES), row)],
        scratch_shapes=[pltpu.VMEM((8, LANES), F32)],
        compiler_params=_params(("parallel", "arbitrary")),
        name="foxpack",
    )(fq, kbuf, vbuf, ff, bias_pad, pk, pq)


def _fox_prompt_kernel(qi_ref, kj_ref, q_ref, k_ref, vt_ref, o_ref, m_ref, acc_ref):
    t = pl.program_id(1)
    i = qi_ref[t]
    j = kj_ref[t]
    nb = q_ref.shape[0]
    tq = q_ref.shape[2]
    tk = k_ref.shape[2]
    assert tq == tk
    nt = (((1,), (1,)), ((), ()))

    @pl.when(j == 0)
    def _():
        m_ref[...] = jnp.full_like(m_ref, NEG)
        acc_ref[...] = jnp.zeros_like(acc_ref)

    def step(diagonal):
        if diagonal:
            keep = (lax.broadcasted_iota(jnp.int32, (tk, tq), 0)
                    <= lax.broadcasted_iota(jnp.int32, (tk, tq), 1))
        items = [(bi, h) for bi in range(nb) for h in range(FOX_HEADS)]
        for g0 in range(0, len(items), FOX_GROUP):
            group = items[g0:g0 + FOX_GROUP]
            st = {(bi, h): lax.dot_general(k_ref[bi, h], q_ref[bi, h], nt, preferred_element_type=F32)
                  for bi, h in group}
            if diagonal:
                st = {it: jnp.where(keep, s, NEG) for it, s in st.items()}
            m_old = {(bi, h): m_ref[bi, h:h + 1, :] for bi, h in group}
            m_new = {it: jnp.maximum(m_old[it], jnp.max(st[it], axis=0, keepdims=True)) for it in group}
            p = {it: jnp.exp2((st[it] - m_new[it]).astype(BF16)) for it in group}
            for bi, h in group:
                alpha = jnp.exp2(m_old[bi, h] - m_new[bi, h])
                acc_ref[bi, h] = alpha * acc_ref[bi, h] + jnp.dot(vt_ref[bi, h], p[bi, h],
                                                                  preferred_element_type=F32)
                m_ref[bi, h:h + 1, :] = m_new[bi, h]

    @pl.when(j < i)
    def _():
        step(False)

    @pl.when(j == i)
    def _():
        step(True)
        for bi in range(nb):
            outs = [acc_ref[bi, h, 0:FOX_DIM, :] / acc_ref[bi, h, FOX_DIM:FOX_DIM + 1, :]
                    for h in range(FOX_HEADS)]
            o_ref[bi] = jnp.concatenate(outs, axis=0).T.astype(o_ref.dtype)


def fox_prompt(qp, kp, vt, tq):
    b, nh, l, _ = qp.shape
    w = nh * FOX_DIM
    nq = l // tq
    tk = tq
    assert nq * tq == l
    qi = np.asarray([i for i in range(nq) for _ in range(i + 1)], np.int32)
    kj = np.asarray([j for i in range(nq) for j in range(i + 1)], np.int32)
    nb = FOX_STREAMS
    assert b % nb == 0
    grid_spec = pltpu.PrefetchScalarGridSpec(
        num_scalar_prefetch=2,
        grid=(b // nb, len(qi)),
        in_specs=[pl.BlockSpec((nb, nh, tq, LANES), lambda bi, t, qi, kj: (bi, 0, qi[t], 0)),
                  pl.BlockSpec((nb, nh, tk, LANES), lambda bi, t, qi, kj: (bi, 0, kj[t], 0)),
                  pl.BlockSpec((nb, nh, FOX_VROWS, tk), lambda bi, t, qi, kj: (bi, 0, 0, kj[t]))],
        out_specs=pl.BlockSpec((nb, tq, w), lambda bi, t, qi, kj: (bi, qi[t], 0)),
        scratch_shapes=[pltpu.VMEM((nb, FOX_HEADS, tq), F32),
                        pltpu.VMEM((nb, FOX_HEADS, FOX_VROWS, tq), F32)])
    return pl.pallas_call(
        _fox_prompt_kernel,
        out_shape=jax.ShapeDtypeStruct((b, l, w), BF16),
        grid_spec=grid_spec,
        compiler_params=_params(("parallel", "arbitrary")),
        name="fox_prompt",
    )(jnp.asarray(qi), jnp.asarray(kj), qp, kp, vt)


def _fox_sample_kernel(q_ref, kp_ref, vp_ref, kn_ref, vn_ref, cq_ref, ck_ref, o_ref):
    lq = q_ref.shape[1]
    past = kp_ref.shape[4]
    ri = lax.broadcasted_iota(jnp.int32, (lq, lq), 0)
    ci = lax.broadcasted_iota(jnp.int32, (lq, lq), 1)
    nt = (((1,), (1,)), ((), ()))
    for h in range(FOX_HEADS):
        cols = slice(h * FOX_DIM, (h + 1) * FOX_DIM)
        q = q_ref[0, :, cols]
        cq = cq_ref[0, :, h:h + 1]
        sp = jnp.dot(q, kp_ref[0, 0, h].astype(BF16), preferred_element_type=F32)
        sp = sp + cq - ck_ref[0, h:h + 1, 0:past]
        sn = lax.dot_general(q, kn_ref[0, :, cols], nt, preferred_element_type=F32)
        sn = sn + cq - ck_ref[0, h:h + 1, past:past + lq]
        sn = jnp.where(ci <= ri, sn, NEG)
        m = jnp.maximum(jnp.max(sp, axis=-1, keepdims=True), jnp.max(sn, axis=-1, keepdims=True))
        pp = jnp.exp2(sp - m)
        pn = jnp.exp2(sn - m)
        den = jnp.sum(pp, axis=-1, keepdims=True) + jnp.sum(pn, axis=-1, keepdims=True)
        o = (lax.dot_general(pp.astype(BF16), vp_ref[0, 0, h].astype(BF16), nt, preferred_element_type=F32)
             + jnp.dot(pn.astype(BF16), vn_ref[0, :, cols], preferred_element_type=F32))
        o_ref[0, :, cols] = (o / den).astype(o_ref.dtype)


def fox_sample(q, kcache_t, vcache_t, layer, kn, vn, cq, ckt):
    b, lq, w = q.shape
    past = kcache_t.shape[4]
    lc = ckt.shape[2]
    new = lambda i: (i, 0, 0)
    old = lambda i: (layer, i, 0, 0, 0)
    cache_blk = pl.BlockSpec((1, 1, FOX_HEADS, FOX_DIM, past), old)
    return pl.pallas_call(
        _fox_sample_kernel,
        out_shape=jax.ShapeDtypeStruct((b, lq, w), BF16),
        grid=(b,),
        in_specs=[pl.BlockSpec((1, lq, w), new),
                  cache_blk,
                  cache_blk,
                  pl.BlockSpec((1, lq, w), new),
                  pl.BlockSpec((1, lq, w), new),
                  pl.BlockSpec((1, lq, FOX_HEADS), new),
                  pl.BlockSpec((1, FOX_HEADS, lc), new)],
        out_specs=pl.BlockSpec((1, lq, w), new),
        compiler_params=_params(("parallel",)),
        name="fox_sample",
    )(q, kcache_t, vcache_t, kn, vn, cq, ckt)


def _outproj_kernel(x_ref, hg_ref, fo_ref, mod_ref, fgain_ref, w_ref, gain2_ref,
                    wrh_ref, wrl_ref, br_ref, xo_ref, hs_ref, dest_ref, cnt_ref):
    s, ls, d = x_ref.shape
    tm = s * ls
    fo = fo_ref[...].astype(F32)
    fn = fo * lax.rsqrt(jnp.mean(fo * fo, axis=-1, keepdims=True) + EPS) * fgain_ref[...]
    mixed = (jnp.dot(hg_ref[...].reshape(tm, HG_WIDTH), w_ref[0:HG_WIDTH, :],
                     preferred_element_type=F32)
             + jnp.dot(fn.reshape(tm, FOX_WIDTH).astype(BF16), w_ref[HG_WIDTH:, :],
                       preferred_element_type=F32))
    x = x_ref[...] + mod_ref[:, 2:3, :] * mixed.reshape(s, ls, d)
    xo_ref[...] = x
    y = x * lax.rsqrt(jnp.mean(x * x, axis=-1, keepdims=True) + EPS) * gain2_ref[...]
    h2 = (y * (1.0 + mod_ref[:, 4:5, :]) + mod_ref[:, 3:4, :]).reshape(tm, d)
    h2b = h2.astype(BF16)

    nt = (((1,), (1,)), ((), ()))
    logits = lax.dot_general(wrh_ref[...], h2b, nt, preferred_element_type=F32) \
        + lax.dot_general(wrl_ref[...], h2b, nt, preferred_element_type=F32)
    z = jnp.exp(logits - jnp.max(logits, axis=0, keepdims=True))
    probs = z / jnp.sum(z, axis=0, keepdims=True)
    sel = probs + br_ref[...]
    rows = [sel[e:e + 1, :] for e in range(N_EXPERTS)]
    prow = [probs[e:e + 1, :] for e in range(N_EXPERTS)]

    def beats(a, ia, b_, ib):
        return jnp.where(a >= b_, 1.0, 0.0) if ia < ib else jnp.where(a > b_, 1.0, 0.0)

    top = []
    gscore = []
    for g in range(N_GROUPS):
        ids = range(g * GROUP_SIZE, (g + 1) * GROUP_SIZE)
        sc = jnp.zeros_like(rows[0])
        for e in ids:
            cnt = jnp.zeros_like(rows[0])
            for o in ids:
                if o != e:
                    cnt = cnt + beats(rows[o], o, rows[e], e)
            flag = jnp.where(cnt < TOP_K, 1.0, 0.0)
            top.append(flag)
            sc = sc + flag * rows[e]
        gscore.append(sc)
    chosen = []
    for g in range(N_GROUPS):
        cnt = jnp.zeros_like(rows[0])
        for o in range(N_GROUPS):
            if o != g:
                cnt = cnt + beats(gscore[o], o, gscore[g], g)
        chosen.append(jnp.where(cnt < 1.0, 1.0, 0.0))
    wts = [prow[e] * top[e] * chosen[e // GROUP_SIZE] for e in range(N_EXPERTS)]
    den = wts[0]
    for e in range(1, N_EXPERTS):
        den = den + wts[e]
    g4 = []
    for k in range(GROUP_SIZE):
        gk = wts[k]
        for g in range(1, N_GROUPS):
            gk = gk + wts[g * GROUP_SIZE + k]
        g4.append(gk / den)

    rr = hs_ref.shape[1]
    ki = lax.broadcasted_iota(jnp.int32, (tm, tm), 0)
    ji = lax.broadcasted_iota(jnp.int32, (tm, tm), 1)
    before = jnp.where(ki < ji, 1.0, 0.0).astype(BF16)
    chosen4 = jnp.concatenate(chosen, axis=0)
    rank = jnp.dot(chosen4.astype(BF16), before, preferred_element_type=F32)
    lane = lax.broadcasted_iota(jnp.int32, (1, LANES), 1)
    start = jnp.zeros((1, 1), F32)
    dest = jnp.zeros((1, tm), F32)
    cnt_row = jnp.zeros((1, LANES), F32)
    for g in range(N_GROUPS):
        n_g = jnp.sum(chosen[g], axis=1, keepdims=True)
        tiles_g = jnp.floor((n_g + (MOE_TILE - 1)) * (1.0 / MOE_TILE))
        dest = dest + chosen[g] * (start + rank[g:g + 1, :])
        cnt_row = cnt_row + jnp.where(lane == g, tiles_g, 0.0)
        start = start + tiles_g * MOE_TILE
    dest_i = dest.astype(jnp.int32)
    dest_ref[0] = dest_i
    cnt_ref[0] = cnt_row.astype(jnp.int32)
    perm = jnp.where(lax.broadcasted_iota(jnp.int32, (rr, tm), 0) == dest_i, 1.0, 0.0).astype(BF16)
    ghi = [g.astype(BF16) for g in g4]
    glo = [(g - h.astype(F32)).astype(BF16) for g, h in zip(g4, ghi)]
    gpk = jnp.concatenate(ghi + glo + [jnp.zeros((LANES - 2 * GROUP_SIZE, tm), BF16)], axis=0)
    hs_ref[0, :, 0:d] = jnp.dot(perm, h2b, preferred_element_type=F32).astype(BF16)
    hs_ref[0, :, d:d + LANES] = lax.dot_general(perm, gpk, nt, preferred_element_type=F32).astype(BF16)


def outproj(x, hg_out, fox_o, mod, fox_gain, w_out, gain2, wr_hi, wr_lo, b_router, s, ls):
    b, l, d = x.shape
    nblk = (b // s) * (l // ls)
    nl = l // ls
    tm = s * ls
    row = lambda i, j: (i, j, 0)
    const = lambda i, j: (0, 0)
    blk = lambda i, j: (i * nl + j, 0, 0)
    return pl.pallas_call(
        _outproj_kernel,
        out_shape=[jax.ShapeDtypeStruct((b, l, d), F32),
                   jax.ShapeDtypeStruct((nblk, MOE_ROWS, d + LANES), BF16),
                   jax.ShapeDtypeStruct((nblk, 1, tm), jnp.int32),
                   jax.ShapeDtypeStruct((nblk, 1, LANES), jnp.int32)],
        grid=(b // s, nl),
        in_specs=[pl.BlockSpec((s, ls, d), row),
                  pl.BlockSpec((s, ls, HG_WIDTH), row),
                  pl.BlockSpec((s, ls, FOX_WIDTH), row),
                  pl.BlockSpec((s, N_MOD, d), lambda i, j: (i, 0, 0)),
                  pl.BlockSpec((1, FOX_WIDTH), const),
                  pl.BlockSpec(w_out.shape, const, pipeline_mode=pl.Buffered(1)),
                  pl.BlockSpec((1, d), const),
                  pl.BlockSpec((N_EXPERTS, d), const),
                  pl.BlockSpec((N_EXPERTS, d), const),
                  pl.BlockSpec((N_EXPERTS, 1), const)],
        out_specs=[pl.BlockSpec((s, ls, d), row),
                   pl.BlockSpec((1, MOE_ROWS, d + LANES), blk),
                   pl.BlockSpec((1, 1, tm), blk),
                   pl.BlockSpec((1, 1, LANES), blk)],
        compiler_params=_params(("parallel", "parallel")),
        name="outproj",
    )(x, hg_out, fox_o, mod, fox_gain, w_out, gain2, wr_hi, wr_lo, b_router)


def _moe_tables(cnt, rows_per_step):
    nblk = cnt.shape[0]
    tiles_per_step = rows_per_step // MOE_TILE
    max_tiles = nblk * (BLOCK_TOKENS // MOE_TILE + N_GROUPS)
    n_steps = -(-max_tiles // tiles_per_step) + N_GROUPS
    npair = N_GROUPS * nblk
    start_gm = (jnp.cumsum(cnt, axis=1) - cnt).T.reshape(-1)
    cnt_gm = cnt.T.reshape(-1)
    csum = jnp.cumsum(cnt_gm)
    tau = jnp.arange(max_tiles, dtype=jnp.int32)
    pair = jnp.minimum(jnp.sum((csum[None, :] <= tau[:, None]).astype(jnp.int32), axis=1), npair - 1)
    hot = (pair[:, None] == jnp.arange(npair, dtype=jnp.int32)[None, :]).astype(jnp.int32)
    k = tau - jnp.sum(hot * (csum - cnt_gm)[None, :], axis=1)
    valid = tau < csum[-1]
    tile_blk = jnp.where(valid, pair % nblk, 0).astype(jnp.int32)
    tile_row = jnp.where(valid, (jnp.sum(hot * start_gm[None, :], axis=1) + k) * MOE_TILE, 0).astype(jnp.int32)
    tot = jnp.sum(cnt, axis=0)
    gstart = jnp.cumsum(tot) - tot
    nsteps = (tot + tiles_per_step - 1) // tiles_per_step
    send = jnp.cumsum(nsteps)
    s = jnp.arange(n_steps, dtype=jnp.int32)
    sg = jnp.minimum(jnp.sum((send[None, :] <= s[:, None]).astype(jnp.int32), axis=1), N_GROUPS - 1)
    ghot = (sg[:, None] == jnp.arange(N_GROUPS, dtype=jnp.int32)[None, :]).astype(jnp.int32)
    pick = lambda v: jnp.sum(ghot * v[None, :], axis=1)
    first = pick(gstart) + (s - pick(send - nsteps)) * tiles_per_step
    num = jnp.clip(pick(gstart + tot) - first, 0, tiles_per_step)
    first = jnp.where(num > 0, first, 0)
    return sg.astype(jnp.int32), first.astype(jnp.int32), num.astype(jnp.int32), tile_blk, tile_row


def _moe_kernel(sg_ref, first_ref, num_ref, tblk_ref, trow_ref,
                hs_a, hs_b, wg_ref, wu_ref, wd_ref, yinit_a, yinit_b, ys_a, ys_b,
                hbuf, ybuf, sem_in, sem_out):
    del sg_ref, yinit_a, yinit_b
    s = pl.program_id(0)
    ns = pl.num_programs(0)
    slot = lax.rem(s, 2)
    na = hs_a.shape[0]

    def tile(k):
        return pl.ds(pl.multiple_of(k * MOE_TILE, MOE_TILE), MOE_TILE)

    def src(step, k):
        t = first_ref[step] + k
        return tblk_ref[t], pl.ds(pl.multiple_of(trow_ref[t], MOE_TILE), MOE_TILE)

    class _Either:
        def __init__(self, blk, make):
            self.blk, self.make = blk, make

        def start(self):
            @pl.when(self.blk < na)
            def _():
                self.make(0, self.blk).start()

            @pl.when(self.blk >= na)
            def _():
                self.make(1, self.blk - na).start()

        def wait(self):
            self.make(0, 0).wait()

    def h_copy(step, sl, k):
        blk, rows = src(step, k)
        return _Either(blk, lambda g, b: pltpu.make_async_copy(
            (hs_a, hs_b)[g].at[b, rows, :], hbuf.at[sl, tile(k), :], sem_in.at[sl]))

    def y_copy(step, sl, k):
        blk, rows = src(step, k)
        return _Either(blk, lambda g, b: pltpu.make_async_copy(
            ybuf.at[sl, tile(k), :], (ys_a, ys_b)[g].at[b, rows, :], sem_out.at[sl]))

    def each_tile(step, fn):
        def body(k, c):
            fn(k)
            return c
        lax.fori_loop(0, num_ref[step], body, 0)

    def start_gather(step, sl):
        each_tile(step, lambda k: h_copy(step, sl, k).start())

    def wait_gather(step, sl):
        each_tile(step, lambda k: h_copy(step, sl, k).wait())

    @pl.when(s == 0)
    def _():
        hbuf[...] = jnp.zeros_like(hbuf)
        start_gather(0, 0)

    wait_gather(s, slot)

    @pl.when(s + 1 < ns)
    def _():
        start_gather(s + 1, 1 - slot)

    @pl.when(s >= 2)
    def _():
        each_tile(s - 2, lambda k: y_copy(s - 2, slot, k).wait())

    @pl.when(num_ref[s] > 0)
    def _():
        d = wg_ref.shape[1]
        h = hbuf[slot, :, 0:d]
        gp = hbuf[slot, :, d:d + LANES].astype(F32)
        acc = jnp.zeros((h.shape[0], wd_ref.shape[2]), F32)
        for k in range(GROUP_SIZE):
            a = jnp.dot(h, wg_ref[k].astype(BF16), preferred_element_type=F32)
            u = jnp.dot(h, wu_ref[k].astype(BF16), preferred_element_type=F32)
            gate = gp[:, k:k + 1] + gp[:, GROUP_SIZE + k:GROUP_SIZE + k + 1]
            he = (_silu(a) * u * gate).astype(BF16)
            acc = acc + jnp.dot(he, wd_ref[k].astype(BF16), preferred_element_type=F32)
        ybuf[slot] = acc.astype(ybuf.dtype)

    each_tile(s, lambda k: y_copy(s, slot, k).start())

    @pl.when(s == ns - 1)
    def _():
        each_tile(s, lambda k: y_copy(s, slot, k).wait())

        @pl.when(s >= 1)
        def _():
            each_tile(s - 1, lambda k: y_copy(s - 1, 1 - slot, k).wait())


def moe_sparse(hs_a, hs_b, cnt, wg, wu, wd, layer):
    d, de = wg.shape[1], wg.shape[2]
    ys_shape = lambda hs: hs.shape[:2] + (d,)
    rows = MOE_STEP_ROWS
    sg, first, num, tile_blk, tile_row = _moe_tables(cnt, rows)
    n_steps = sg.shape[0]
    wmap = lambda s, sg, *_: (layer * N_GROUPS + sg[s], 0, 0)
    grid_spec = pltpu.PrefetchScalarGridSpec(
        num_scalar_prefetch=5,
        grid=(n_steps,),
        in_specs=[pl.BlockSpec(memory_space=pl.ANY)] * 2
        + [pl.BlockSpec((GROUP_SIZE, d, de), wmap, pipeline_mode=pl.Buffered(1)),
           pl.BlockSpec((GROUP_SIZE, d, de), wmap, pipeline_mode=pl.Buffered(1)),
           pl.BlockSpec((GROUP_SIZE, de, d), wmap, pipeline_mode=pl.Buffered(1))]
        + [pl.BlockSpec(memory_space=pl.ANY)] * 2,
        out_specs=[pl.BlockSpec(memory_space=pl.ANY)] * 2,
        scratch_shapes=[pltpu.VMEM((2, rows, d + LANES), BF16),
                        pltpu.VMEM((2, rows, d), BF16),
                        pltpu.SemaphoreType.DMA((2,)),
                        pltpu.SemaphoreType.DMA((2,))])
    return pl.pallas_call(
        _moe_kernel,
        out_shape=[jax.ShapeDtypeStruct(ys_shape(hs_a), BF16), jax.ShapeDtypeStruct(ys_shape(hs_b), BF16)],
        grid_spec=grid_spec,
        input_output_aliases={10: 0, 11: 1},
        compiler_params=_params(("arbitrary",)),
        name="moe",
    )(sg, first, num, tile_blk, tile_row, hs_a, hs_b, wg, wu, wd,
      jnp.zeros(ys_shape(hs_a), BF16), jnp.zeros(ys_shape(hs_b), BF16))


def _combine_kernel(x_ref, ys_ref, dest_ref, mod_ref, fgain_ref, o_ref, *, final):
    s, ls, d = x_ref.shape
    tm = s * ls
    rr = ys_ref.shape[1]
    dcol = jnp.broadcast_to(dest_ref[0].astype(F32), (LANES, tm)).T
    lane = lax.broadcasted_iota(jnp.int32, (tm, LANES), 1).astype(F32)
    unperm = jnp.concatenate(
        [jnp.where(dcol == lane + float(c * LANES), 1.0, 0.0).astype(BF16) for c in range(rr // LANES)],
        axis=1)
    y = jnp.dot(unperm, ys_ref[0], preferred_element_type=F32)
    x = x_ref[...] + mod_ref[:, 5:6, :] * y.reshape(s, ls, d)
    if final:
        x = x * lax.rsqrt(jnp.mean(x * x, axis=-1, keepdims=True) + EPS) * fgain_ref[...]
    o_ref[...] = x


def combine(x, ys, dest, mod, final_gain, s, ls, final):
    b, l, d = x.shape
    nl = l // ls
    tm = s * ls
    row = lambda i, j: (i, j, 0)
    blk = lambda i, j: (i * nl + j, 0, 0)
    return pl.pallas_call(
        functools.partial(_combine_kernel, final=final),
        out_shape=jax.ShapeDtypeStruct((b, l, d), F32),
        grid=(b // s, nl),
        in_specs=[pl.BlockSpec((s, ls, d), row),
                  pl.BlockSpec((1, ys.shape[1], d), blk),
                  pl.BlockSpec((1, 1, tm), blk),
                  pl.BlockSpec((s, N_MOD, d), lambda i, j: (i, 0, 0)),
                  pl.BlockSpec((1, d), lambda i, j: (0, 0))],
        out_specs=pl.BlockSpec((s, ls, d), row),
        compiler_params=_params(("parallel", "parallel")),
        name="combine",
    )(x, ys, dest, mod, final_gain)


def _tile(b, l):
    ls = min(l, BLOCK_TOKENS)
    s = BLOCK_TOKENS // ls
    assert s * ls == BLOCK_TOKENS and b % s == 0 and l % ls == 0
    assert BLOCK_TOKENS + N_GROUPS * MOE_TILE <= MOE_ROWS
    return s, ls


def _mixer(x, mod, l, p, hg_state0, fox_past, kv_prev):
    depth = p['norm_mix_gain'].shape[0]
    b, seq, d = x.shape
    s, ls = _tile(b, seq)
    sample = fox_past is not None
    hg_in, fq, kbuf, vbuf, *k16v16, ff = inproj(x, mod, p['norm_mix_gain'][l], p['w_in'][l], s, ls,
                                                l, depth, kv_prev, sample)

    hg_out, hg_state = hgrn(hg_in, hg_state0, p['hg_lower_bounds'], p['hg_norm_gain'][l], l)

    if not sample:
        tq = min(seq, BLOCK_TOKENS)
        hg_out, fq = lax.optimization_barrier((hg_out, fq))
        qp, kp, vt, lf_pad = foxpack(fq, kbuf, vbuf, l, ff, p['fox_f_bias_pad'][l], tq)
        fox_o = fox_prompt(qp, kp, vt, tq)
        logf = lf_pad[:, :, :FOX_HEADS]
    else:
        kcache, vcache, plogf = fox_past
        past = kcache.shape[4]
        fft = jnp.swapaxes(ff[:, :, :FOX_HEADS], 1, 2)
        logft, _ = forget_cumsum(fft, p['fox_f_bias'][l], True, seq)
        tot = past + seq
        pad = (-tot) % LANES
        allt = jnp.concatenate([jnp.swapaxes(plogf[l], 1, 2), logft,
                                jnp.zeros((b, FOX_HEADS, pad), F32)], axis=2)
        _, ct = forget_cumsum(allt, p['fox_f_bias'][l], False, LANES)
        ct = ct * LOG2E
        cq = jnp.swapaxes(ct[:, :, past:tot], 1, 2)
        fox_o = fox_sample(fq, kcache, vcache, l, k16v16[0], k16v16[1], cq, ct)
        logf = jnp.swapaxes(logft, 1, 2)

    routed = outproj(x, hg_out, fox_o, mod, p['fox_out_gain'][l], p['w_out'][l],
                     p['norm_ffn_gain'][l], p['wr_hi'], p['wr_lo'], p['b_router'], s, ls)
    return routed, (kbuf, vbuf), hg_state, logf


def _ffn(routed_p, routed_s, mod_p, mod_s, l, p, final):
    xm_p, hs_p, dest_p, cnt_p = routed_p
    xm_s, hs_s, dest_s, cnt_s = routed_s
    cnt = jnp.concatenate([cnt_p, cnt_s], axis=0)[:, 0, :N_GROUPS]
    ys_p, ys_s = moe_sparse(hs_p, hs_s, cnt, p['w_exp_gate'], p['w_exp_up'], p['w_exp_down'], l)
    outs = []
    for xm, ys, dest, mod in ((xm_p, ys_p, dest_p, mod_p), (xm_s, ys_s, dest_s, mod_s)):
        s, ls = _tile(xm.shape[0], xm.shape[1])
        outs.append(combine(xm, ys, dest, mod, p['final_norm_gain'], s, ls, final))
    return outs


def kernel(x_prompt, x_sample, cache_fox_k, cache_fox_v, cache_fox_logf, state_hgrn, c_prompt, c_sample,
           norm_mix_gain, norm_ffn_gain, w_ada, b_ada, w_in, hg_lower_bounds, hg_norm_gain,
           fox_f_bias, fox_out_gain, w_out, w_router, b_router, w_exp_gate, w_exp_up, w_exp_down,
           final_norm_gain):
    depth, d = norm_mix_gain.shape
    bp = x_prompt.shape[0]
    n_in = w_in.shape[2]
    n_pad = 4 * HG_WIDTH + 3 * FOX_WIDTH + LANES - n_in
    wr_t = w_router.T
    wr_hi = wr_t.astype(BF16)
    p = {
        'norm_mix_gain': norm_mix_gain.reshape(depth, 1, d),
        'norm_ffn_gain': norm_ffn_gain.reshape(depth, 1, d),
        'w_in': jnp.pad(w_in, ((0, 0), (0, 0), (0, n_pad))).astype(BF16),
        'hg_lower_bounds': hg_lower_bounds,
        'hg_norm_gain': hg_norm_gain.reshape(depth, 1, HG_WIDTH),
        'fox_f_bias': fox_f_bias.reshape(depth, FOX_HEADS, 1),
        'fox_f_bias_pad': jnp.pad(fox_f_bias, ((0, 0), (0, LANES - FOX_HEADS))).reshape(depth, 1, LANES),
        'fox_out_gain': fox_out_gain.reshape(depth, 1, FOX_WIDTH),
        'w_out': w_out.astype(BF16),
        'wr_hi': wr_hi,
        'wr_lo': (wr_t - wr_hi.astype(F32)).astype(BF16),
        'b_router': b_router.reshape(N_EXPERTS, 1),
        'w_exp_gate': w_exp_gate.reshape((depth * N_EXPERTS,) + w_exp_gate.shape[2:]),
        'w_exp_up': w_exp_up.reshape((depth * N_EXPERTS,) + w_exp_up.shape[2:]),
        'w_exp_down': w_exp_down.reshape((depth * N_EXPERTS,) + w_exp_down.shape[2:]),
        'final_norm_gain': final_norm_gain.reshape(1, d),
    }
    mods = ada_mod(jnp.concatenate([c_prompt, c_sample], axis=0), w_ada, b_ada)
    mods = mods.reshape(depth, -1, N_MOD, d)
    bs, lsq = x_sample.shape[:2]
    lp = x_prompt.shape[1]
    past = cache_fox_k.shape[2]
    fox_past = (jnp.transpose(cache_fox_k, (0, 1, 3, 4, 2)), jnp.transpose(cache_fox_v, (0, 1, 3, 4, 2)),
                cache_fox_logf)

    xp, xs = x_prompt, x_sample
    kv_p = kv_s = None
    st_p, st_s, lf_p, lf_s = [], [], [], []
    zero_state = jnp.zeros((bp, HG_HEADS, HG_DIM, HG_DIM), F32)
    for l in range(depth):
        routed_p, kv_p, st, lf = _mixer(xp, mods[l, :bp], l, p, zero_state, None, kv_p)
        st_p.append(st)
        lf_p.append(lf)
        routed_s, kv_s, st, lf = _mixer(xs, mods[l, bp:], l, p, state_hgrn[l], fox_past, kv_s)
        st_s.append(st)
        lf_s.append(lf)
        xp, xs = _ffn(routed_p, routed_s, mods[l, :bp], mods[l, bp:], l, p, l == depth - 1)
    heads = lambda a, b, l: a.reshape(depth, b, l, FOX_HEADS, FOX_DIM)
    return (xp, xs,
            jnp.stack(st_p), heads(kv_p[0], bp, lp), heads(kv_p[1], bp, lp), jnp.stack(lf_p),
            jnp.stack(st_s), heads(kv_s[0], bs, lsq), heads(kv_s[1], bs, lsq), jnp.stack(lf_s))
```

```python
import functools
import math

import numpy as np
import jax
import jax.numpy as jnp
from jax import lax
from jax.experimental import pallas as pl
from jax.experimental.pallas import tpu as pltpu

HG_HEADS = 4
HG_DIM = 128
HG_WIDTH = HG_HEADS * HG_DIM
FOX_HEADS = 8
FOX_DIM = 64
FOX_WIDTH = FOX_HEADS * FOX_DIM
N_EXPERTS = 16
N_GROUPS = 4
GROUP_SIZE = N_EXPERTS // N_GROUPS
TOP_K = 2
N_MOD = 6
EPS = 1e-6

LANES = 128
HG_CHUNK = 128
HG_SUB = 16
HG_STREAMS = 8
MOE_TILE = 32
BLOCK_TOKENS = 512
MOE_ROWS = 640
MOE_STEP_ROWS = 1024
VMEM_LIMIT = 56 * 1024 * 1024

F32 = jnp.float32
BF16 = jnp.bfloat16
NEG = -1e30
LOG2E = math.log2(math.e)


def _params(sem, vmem=VMEM_LIMIT):
    return pltpu.CompilerParams(dimension_semantics=sem, vmem_limit_bytes=vmem)


def _split3(x):
    hi = x.astype(BF16)
    r1 = x - hi.astype(F32)
    mid = r1.astype(BF16)
    lo = (r1 - mid.astype(F32)).astype(BF16)
    return hi, mid, lo


def _sigmoid(x):
    return 1.0 / (1.0 + jnp.exp(-x))


def _log_sigmoid(x):
    return jnp.minimum(x, 0.0) - jnp.log(1.0 + jnp.exp(-jnp.abs(x)))


def _silu(x):
    return x * _sigmoid(x)


def _ada_kernel(c_ref, w_ref, b_ref, o_ref):
    s = _silu(c_ref[...]).astype(BF16)
    o_ref[0] = jnp.dot(s, w_ref[0].astype(BF16), preferred_element_type=F32) + b_ref[0]


def ada_mod(c, w_ada, b_ada):
    depth, d, n = w_ada.shape
    nb = c.shape[0]
    tn = 1536
    return pl.pallas_call(
        _ada_kernel,
        out_shape=jax.ShapeDtypeStruct((depth, nb, n), F32),
        grid=(depth, n // tn),
        in_specs=[pl.BlockSpec((nb, d), lambda l, j: (0, 0)),
                  pl.BlockSpec((1, d, tn), lambda l, j: (l, 0, j)),
                  pl.BlockSpec((1, 1, tn), lambda l, j: (l, 0, j))],
        out_specs=pl.BlockSpec((1, nb, tn), lambda l, j: (l, 0, j)),
        compiler_params=_params(("arbitrary", "arbitrary")),
        name="ada_mod",
    )(c, w_ada, b_ada.reshape(depth, 1, n))


def _inproj_kernel(x_ref, mod_ref, gain_ref, w_ref, *refs, has_prev, emit16):
    outs = refs[2:] if has_prev else refs
    hg_ref, fq_ref, fk32_ref, fv32_ref = outs[:4]
    ff_ref = outs[-1]
    s, ls, d = x_ref.shape
    x = x_ref[...]
    y = x * lax.rsqrt(jnp.mean(x * x, axis=-1, keepdims=True) + EPS) * gain_ref[...]
    h = y * (1.0 + mod_ref[:, 1:2, :]) + mod_ref[:, 0:1, :]
    hb = h.reshape(s * ls, d).astype(BF16)

    def proj(lo, hi):
        return jnp.dot(hb, w_ref[:, lo:hi], preferred_element_type=F32)

    c0 = 4 * HG_WIDTH
    hg_ref[...] = proj(0, c0).reshape(s, ls, c0)
    fq = proj(c0, c0 + FOX_WIDTH) * (FOX_DIM ** -0.5 * LOG2E)
    fq_ref[...] = fq.reshape(s, ls, FOX_WIDTH).astype(BF16)
    fk = proj(c0 + FOX_WIDTH, c0 + 2 * FOX_WIDTH).reshape(s, ls, FOX_WIDTH)
    fv = proj(c0 + 2 * FOX_WIDTH, c0 + 3 * FOX_WIDTH).reshape(s, ls, FOX_WIDTH)
    for slot in range(fk32_ref.shape[0]):
        fk32_ref[slot] = fk
        fv32_ref[slot] = fv
    if emit16:
        outs[4][...] = fk.astype(BF16)
        outs[5][...] = fv.astype(BF16)
    ff_ref[...] = proj(c0 + 3 * FOX_WIDTH, c0 + 3 * FOX_WIDTH + LANES).reshape(s, ls, LANES)


def inproj(x, mod, gain, w_pad, s, ls, layer, depth, kv_prev, emit16):
    b, l, d = x.shape
    n = w_pad.shape[1]
    row = lambda i, j: (i, j, 0)
    lrow = lambda i, j: (layer, i, j, 0)
    shapes = [jax.ShapeDtypeStruct((b, l, 4 * HG_WIDTH), F32),
              jax.ShapeDtypeStruct((b, l, FOX_WIDTH), BF16),
              jax.ShapeDtypeStruct((depth, b, l, FOX_WIDTH), F32),
              jax.ShapeDtypeStruct((depth, b, l, FOX_WIDTH), F32)]
    slots = depth if kv_prev is None else 1
    if kv_prev is None:
        assert layer == 0
    specs = [pl.BlockSpec((s, ls, 4 * HG_WIDTH), row),
             pl.BlockSpec((s, ls, FOX_WIDTH), row),
             pl.BlockSpec((slots, s, ls, FOX_WIDTH), lrow),
             pl.BlockSpec((slots, s, ls, FOX_WIDTH), lrow)]
    if emit16:
        shapes += [jax.ShapeDtypeStruct((b, l, FOX_WIDTH), BF16)] * 2
        specs += [pl.BlockSpec((s, ls, FOX_WIDTH), row)] * 2
    shapes.append(jax.ShapeDtypeStruct((b, l, LANES), F32))
    specs.append(pl.BlockSpec((s, ls, LANES), row))
    in_specs = [pl.BlockSpec((s, ls, d), row),
                pl.BlockSpec((s, N_MOD, d), lambda i, j: (i, 0, 0)),
                pl.BlockSpec((1, d), lambda i, j: (0, 0)),
                pl.BlockSpec((d, n), lambda i, j: (0, 0), pipeline_mode=pl.Buffered(1))]
    args = [x, mod, gain, w_pad]
    aliases = {}
    if kv_prev is not None:
        in_specs += [pl.BlockSpec(memory_space=pl.ANY)] * 2
        args += list(kv_prev)
        aliases = {4: 2, 5: 3}
    return pl.pallas_call(
        functools.partial(_inproj_kernel, has_prev=kv_prev is not None, emit16=emit16),
        out_shape=shapes,
        grid=(b // s, l // ls),
        in_specs=in_specs,
        out_specs=specs,
        input_output_aliases=aliases,
        compiler_params=_params(("parallel", "parallel")),
        name="inproj",
    )(*args)


def _cumsum_kernel(x_ref, bias_ref, lf_ref, c_ref, carry_ref, *, apply_ls):
    @pl.when(pl.program_id(1) == 0)
    def _():
        carry_ref[...] = jnp.zeros_like(carry_ref)

    bb, nh, tc = x_ref.shape
    x = x_ref[...].reshape(bb * nh, tc)
    lf = _log_sigmoid(x + bias_ref[...]) if apply_ls else x
    lf_ref[...] = lf.reshape(bb, nh, tc)
    r = lax.broadcasted_iota(jnp.int32, (tc, tc), 0)
    c = lax.broadcasted_iota(jnp.int32, (tc, tc), 1)
    tri = jnp.where(r <= c, 1.0, 0.0).astype(BF16)
    hi, mid, lo = _split3(lf)
    tot = (jnp.dot(hi, tri, preferred_element_type=F32)
           + jnp.dot(mid, tri, preferred_element_type=F32)
           + jnp.dot(lo, tri, preferred_element_type=F32)) + carry_ref[:, 0:1]
    c_ref[...] = tot.reshape(bb, nh, tc)
    carry_ref[...] = jnp.broadcast_to(tot[:, tc - 1:tc], carry_ref.shape)


def forget_cumsum(xt, bias, apply_ls, tc):
    b, h, l = xt.shape
    blk = pl.BlockSpec((b, h, tc), lambda i, j: (0, 0, j))
    return pl.pallas_call(
        functools.partial(_cumsum_kernel, apply_ls=apply_ls),
        out_shape=[jax.ShapeDtypeStruct((b, h, l), F32)] * 2,
        grid=(1, l // tc),
        in_specs=[blk, pl.BlockSpec((b * h, 1), lambda i, j: (0, 0))],
        out_specs=[blk, blk],
        scratch_shapes=[pltpu.VMEM((b * h, LANES), F32)],
        compiler_params=_params(("arbitrary", "arbitrary")),
        name="forget_cumsum",
    )(xt, jnp.tile(bias, (b, 1)))


HG_HALVES = (8, 4, 2, 1)


def _hgrn_sum_matrix():
    c = HG_CHUNK
    t = np.arange(c)[:, None]
    u = np.arange(c)[None, :]
    mats = [(u <= t)]
    for w in HG_HALVES:
        pos = t % (2 * w)
        mid = t - pos + w - 1
        upper = pos >= w
        mats.append(np.where(upper, (u > mid) & (u <= t), (u > t) & (u <= mid)))
    return jnp.asarray(np.concatenate(mats, axis=0).astype(np.float32), BF16)


def _hgrn_kernel(hin_ref, s0_ref, lbp_ref, gain_ref, sums_ref, out_ref, sfin_ref, st_ref, *, layer):
    t = pl.program_id(1)
    nt = pl.num_programs(1)
    c = HG_CHUNK
    nsub = c // HG_SUB
    nb = hin_ref.shape[0]
    rows_in = hin_ref.shape[1]

    def pad_rows(a):
        if rows_in == c:
            return a
        return jnp.concatenate([a, jnp.zeros((c - rows_in, a.shape[1]), a.dtype)], axis=0)

    @pl.when(t == 0)
    def _():
        for bi in range(nb):
            for h in range(HG_HEADS):
                st_ref[bi, h] = s0_ref[bi, h].T

    lbp = lbp_ref[...]
    e = jnp.exp(lbp - jnp.max(lbp, axis=0, keepdims=True))
    p = e / jnp.sum(e, axis=0, keepdims=True)
    acc = p[0:1]
    first = acc
    for i in range(1, layer + 1):
        acc = acc + p[i:i + 1]
    lb_all = acc - first

    ri = lax.broadcasted_iota(jnp.int32, (c, c), 0)
    ci = lax.broadcasted_iota(jnp.int32, (c, c), 1)
    ones = jnp.ones((HG_DIM, HG_DIM), BF16)
    nt_dims = (((1,), (1,)), ((), ()))

    streams = [(bi, h) for bi in range(nb) for h in range(HG_HEADS)]
    q, kk, v, b, sums, att = {}, {}, {}, {}, {}, {}
    for bi, h in streams:
        lb = lb_all[:, h * HG_DIM:(h + 1) * HG_DIM]
        hq = hin_ref[bi, :, h * HG_DIM:(h + 1) * HG_DIM]
        hf = hin_ref[bi, :, HG_WIDTH + h * HG_DIM:HG_WIDTH + (h + 1) * HG_DIM]
        hi = hin_ref[bi, :, 2 * HG_WIDTH + h * HG_DIM:2 * HG_WIDTH + (h + 1) * HG_DIM]
        q[bi, h] = pad_rows(_silu(hq) * (HG_DIM ** -0.5))
        a1 = jnp.log(lb)
        e = jnp.exp(-jnp.abs(hf))
        den = 1.0 + e
        a2 = jnp.log1p(-lb) + (jnp.minimum(hf, 0.0) - jnp.log(den))
        mx = jnp.maximum(a1, a2)
        lf = pad_rows(mx + jnp.log(jnp.exp(a1 - mx) + jnp.exp(a2 - mx)))
        kk[bi, h] = pad_rows((1.0 - lb) * (jnp.where(hf > 0.0, e, 1.0) / den))
        v[bi, h] = pad_rows(hi)
        lhi, lmid, llo = _split3(lf)
        sums[bi, h] = (jnp.dot(sums_ref[...], lhi, preferred_element_type=F32)
                       + jnp.dot(sums_ref[...], lmid, preferred_element_type=F32)
                       + jnp.dot(sums_ref[...], llo, preferred_element_type=F32))
        b[bi, h] = sums[bi, h][0:c]

    for key in streams:
        att[key] = jnp.where(ri == ci, jnp.dot((q[key] * kk[key]).astype(BF16), ones,
                                               preferred_element_type=F32), 0.0)
    for lvl, w in enumerate(HG_HALVES):
        shift = (2 * w).bit_length() - 1
        pair = ((ri >> shift) == (ci >> shift)) & ((ri & (2 * w - 1)) >= w) & ((ci & (2 * w - 1)) < w)
        for key in streams:
            dec = jnp.exp(sums[key][(lvl + 1) * c:(lvl + 2) * c])
            sc = lax.dot_general((q[key] * dec).astype(BF16), (kk[key] * dec).astype(BF16), nt_dims,
                                 preferred_element_type=F32)
            att[key] = att[key] + jnp.where(pair, sc, 0.0)

    khat = {key: None for key in streams}
    r_prev = {key: jnp.zeros((1, HG_DIM), F32) for key in streams}
    qhat_blocks = {key: [] for key in streams}
    cross = {key: [] for key in streams}
    for i in range(nsub):
        r0 = i * HG_SUB
        for key in streams:
            bb = b[key][r0:r0 + HG_SUB]
            r_next = b[key][r0 + HG_SUB - 1:r0 + HG_SUB]
            qt = q[key][r0:r0 + HG_SUB] * jnp.exp(bb - r_prev[key])
            if khat[key] is None:
                cross[key].append(jnp.zeros((HG_SUB, c), F32))
            else:
                kfull = jnp.concatenate([khat[key], jnp.zeros((c - r0, HG_DIM), F32)], axis=0)
                cross[key].append(lax.dot_general(qt.astype(BF16), kfull.astype(BF16), nt_dims,
                                                  preferred_element_type=F32))
            qhat_blocks[key].append(qt * jnp.exp(r_prev[key]))
            kt = kk[key][r0:r0 + HG_SUB] * jnp.exp(r_next - bb)
            if khat[key] is None:
                khat[key] = kt
            else:
                khat[key] = jnp.concatenate([khat[key] * jnp.exp(r_next - r_prev[key]), kt], axis=0)
            r_prev[key] = r_next

    for bi, h in streams:
        key = (bi, h)
        cols = slice(h * HG_DIM, (h + 1) * HG_DIM)
        st = st_ref[bi, h]
        a = att[key] + jnp.concatenate(cross[key], axis=0)
        qhat = jnp.concatenate(qhat_blocks[key], axis=0).astype(BF16)
        o = jnp.dot(a.astype(BF16), v[key].astype(BF16), preferred_element_type=F32)
        o = o + lax.dot_general(qhat, st.astype(BF16), nt_dims, preferred_element_type=F32)
        st_ref[bi, h] = st * jnp.exp(r_prev[key]) + jnp.dot(
            v[key].T.astype(BF16), khat[key].astype(BF16), preferred_element_type=F32)

        hg = hin_ref[bi, :, 3 * HG_WIDTH + h * HG_DIM:3 * HG_WIDTH + (h + 1) * HG_DIM]
        o = o[:rows_in]
        o = o * lax.rsqrt(jnp.mean(o * o, axis=-1, keepdims=True) + EPS)
        o = o * gain_ref[:, cols] * _silu(hg)
        out_ref[bi, :, cols] = o.astype(out_ref.dtype)

    @pl.when(t == nt - 1)
    def _():
        for bi, h in streams:
            sfin_ref[bi, h] = st_ref[bi, h].T


def hgrn(hin, s0, lbp, gain, layer):
    b, l, w = hin.shape
    c = min(HG_CHUNK, l)
    assert l % c == 0 and c % HG_SUB == 0
    nt = l // c
    nb = HG_STREAMS
    assert b % nb == 0
    sums = _hgrn_sum_matrix()
    return pl.pallas_call(
        functools.partial(_hgrn_kernel, layer=layer),
        out_shape=[jax.ShapeDtypeStruct((b, l, HG_WIDTH), BF16),
                   jax.ShapeDtypeStruct((b, HG_HEADS, HG_DIM, HG_DIM), F32)],
        grid=(b // nb, nt),
        in_specs=[pl.BlockSpec((nb, c, w), lambda i, j: (i, j, 0)),
                  pl.BlockSpec((nb, HG_HEADS, HG_DIM, HG_DIM), lambda i, j: (i, 0, 0, 0)),
                  pl.BlockSpec(lbp.shape, lambda i, j: (0, 0)),
                  pl.BlockSpec((1, HG_WIDTH), lambda i, j: (0, 0)),
                  pl.BlockSpec(sums.shape, lambda i, j: (0, 0))],
        out_specs=[pl.BlockSpec((nb, c, HG_WIDTH), lambda i, j: (i, j, 0)),
                   pl.BlockSpec((nb, HG_HEADS, HG_DIM, HG_DIM), lambda i, j: (i, 0, 0, 0))],
        scratch_shapes=[pltpu.VMEM((nb, HG_HEADS, HG_DIM, HG_DIM), F32)],
        compiler_params=_params(("parallel", "arbitrary")),
        name="hgrn",
    )(hin, s0, lbp, gain, sums)


AUG_ONE = FOX_DIM
AUG_NEG = FOX_DIM + 3
ONE_LANE = FOX_HEADS
FOX_GROUP = 4
FOX_STREAMS = 4
FOX_VROWS = FOX_DIM + 16


def _placement():
    pk = np.zeros((3 * LANES, FOX_HEADS * LANES), np.float32)
    pq = np.zeros((3 * LANES, FOX_HEADS * LANES), np.float32)
    for h in range(FOX_HEADS):
        for part in range(3):
            pk[part * LANES + h, h * LANES + AUG_NEG + part] = -1.0
            pq[part * LANES + h, h * LANES + AUG_ONE + part] = 1.0
            pk[ONE_LANE, h * LANES + AUG_ONE + part] = 1.0
            pq[ONE_LANE, h * LANES + AUG_NEG + part] = 1.0
    return jnp.asarray(pk, BF16), jnp.asarray(pq, BF16)


def _foxpack_kernel(q_ref, k_ref, v_ref, ff_ref, bias_ref, pk_ref, pq_ref,
                    qp_ref, kp_ref, vt_ref, lf_ref, carry_ref):
    @pl.when(pl.program_id(1) == 0)
    def _():
        carry_ref[...] = jnp.zeros_like(carry_ref)

    tm = q_ref.shape[1]
    lane = lax.broadcasted_iota(jnp.int32, (tm, LANES), 1)
    lf = jnp.where(lane < FOX_HEADS, _log_sigmoid(ff_ref[0] + bias_ref[...]), 0.0)
    lf_ref[0] = lf
    ri = lax.broadcasted_iota(jnp.int32, (tm, tm), 0)
    ci = lax.broadcasted_iota(jnp.int32, (tm, tm), 1)
    tril = jnp.where(ci <= ri, 1.0, 0.0).astype(BF16)
    hi, mid, lo = _split3(lf)
    c = (jnp.dot(tril, hi, preferred_element_type=F32)
         + jnp.dot(tril, mid, preferred_element_type=F32)
         + jnp.dot(tril, lo, preferred_element_type=F32)) + carry_ref[0:1, :]
    carry_ref[...] = jnp.broadcast_to(c[tm - 1:tm, :], carry_ref.shape)

    chi, cmid, clo = _split3(c * LOG2E)
    chi = jnp.where(lane == ONE_LANE, 1.0, chi.astype(F32)).astype(BF16)
    cterms = jnp.concatenate([chi, cmid, clo], axis=1)
    augk = jnp.dot(cterms, pk_ref[...], preferred_element_type=F32)
    augq = jnp.dot(cterms, pq_ref[...], preferred_element_type=F32)
    for h in range(FOX_HEADS):
        blk = slice((h // 2) * LANES, (h // 2 + 1) * LANES)
        hcols = slice(h * LANES, (h + 1) * LANES)
        kh = k_ref[0, 0, :, blk]
        qh = q_ref[0, :, blk].astype(F32)
        if h % 2:
            kh = pltpu.roll(kh, FOX_DIM, axis=1)
            qh = pltpu.roll(qh, FOX_DIM, axis=1)
        kp_ref[0, h] = jnp.where(lane < FOX_DIM, kh, augk[:, hcols]).astype(BF16)
        qp_ref[0, h] = jnp.where(lane < FOX_DIM, qh, augq[:, hcols]).astype(BF16)
    vt = v_ref[0, 0].T
    tail = jnp.where(lax.broadcasted_iota(jnp.int32, (FOX_VROWS - FOX_DIM, tm), 0) == 0, 1.0, 0.0)
    for h in range(FOX_HEADS):
        vt_ref[0, h] = jnp.concatenate([vt[h * FOX_DIM:(h + 1) * FOX_DIM], tail], axis=0).astype(BF16)


def foxpack(fq, kbuf, vbuf, layer, ff, bias_pad, tm):
    b, l, w = fq.shape
    pk, pq = _placement()
    row = lambda i, j: (i, j, 0)
    lrow = lambda i, j: (layer, i, j, 0)
    const = lambda i, j: (0, 0)
    head_blk = pl.BlockSpec((1, FOX_HEADS, tm, LANES), lambda i, j: (i, 0, j, 0))
    return pl.pallas_call(
        _foxpack_kernel,
        out_shape=[jax.ShapeDtypeStruct((b, FOX_HEADS, l, LANES), BF16),
                   jax.ShapeDtypeStruct((b, FOX_HEADS, l, LANES), BF16),
                   jax.ShapeDtypeStruct((b, FOX_HEADS, FOX_VROWS, l), BF16),
                   jax.ShapeDtypeStruct((b, l, LANES), F32)],
        grid=(b, l // tm),
        in_specs=[pl.BlockSpec((1, tm, w), row),
                  pl.BlockSpec((1, 1, tm, w), lrow),
                  pl.BlockSpec((1, 1, tm, w), lrow),
                  pl.BlockSpec((1, tm, LANES), row),
                  pl.BlockSpec((1, LANES), const),
                  pl.BlockSpec(pk.shape, const),
                  pl.BlockSpec(pq.shape, const)],
        out_specs=[head_blk, head_blk,
                   pl.BlockSpec((1, FOX_HEADS, FOX_VROWS, tm), lambda i, j: (i, 0, 0, j)),
                   pl.BlockSpec((1, tm, LANES), row)],
        scratch_shapes=[pltpu.VMEM((8, LANES), F32)],
        compiler_params=_params(("parallel", "arbitrary")),
        name="foxpack",
    )(fq, kbuf, vbuf, ff, bias_pad, pk, pq)


def _fox_prompt_kernel(qi_ref, kj_ref, q_ref, k_ref, vt_ref, o_ref, m_ref, acc_ref):
    t = pl.program_id(1)
    i = qi_ref[t]
    j = kj_ref[t]
    nb = q_ref.shape[0]
    tq = q_ref.shape[2]
    tk = k_ref.shape[2]
    assert tq == tk
    nt = (((1,), (1,)), ((), ()))

    @pl.when(j == 0)
    def _():
        m_ref[...] = jnp.full_like(m_ref, NEG)
        acc_ref[...] = jnp.zeros_like(acc_ref)

    def step(diagonal):
        if diagonal:
            keep = (lax.broadcasted_iota(jnp.int32, (tk, tq), 0)
                    <= lax.broadcasted_iota(jnp.int32, (tk, tq), 1))
        items = [(bi, h) for bi in range(nb) for h in range(FOX_HEADS)]
        for g0 in range(0, len(items), FOX_GROUP):
            group = items[g0:g0 + FOX_GROUP]
            st = {(bi, h): lax.dot_general(k_ref[bi, h], q_ref[bi, h], nt, preferred_element_type=F32)
                  for bi, h in group}
            if diagonal:
                st = {it: jnp.where(keep, s, NEG) for it, s in st.items()}
            m_old = {(bi, h): m_ref[bi, h:h + 1, :] for bi, h in group}
            m_new = {it: jnp.maximum(m_old[it], jnp.max(st[it], axis=0, keepdims=True)) for it in group}
            p = {it: jnp.exp2((st[it] - m_new[it]).astype(BF16)) for it in group}
            for bi, h in group:
                alpha = jnp.exp2(m_old[bi, h] - m_new[bi, h])
                acc_ref[bi, h] = alpha * acc_ref[bi, h] + jnp.dot(vt_ref[bi, h], p[bi, h],
                                                                  preferred_element_type=F32)
                m_ref[bi, h:h + 1, :] = m_new[bi, h]

    @pl.when(j < i)
    def _():
        step(False)

    @pl.when(j == i)
    def _():
        step(True)
        for bi in range(nb):
            outs = [acc_ref[bi, h, 0:FOX_DIM, :] / acc_ref[bi, h, FOX_DIM:FOX_DIM + 1, :]
                    for h in range(FOX_HEADS)]
            o_ref[bi] = jnp.concatenate(outs, axis=0).T.astype(o_ref.dtype)


def fox_prompt(qp, kp, vt, tq):
    b, nh, l, _ = qp.shape
    w = nh * FOX_DIM
    nq = l // tq
    tk = tq
    assert nq * tq == l
    qi = np.asarray([i for i in range(nq) for _ in range(i + 1)], np.int32)
    kj = np.asarray([j for i in range(nq) for j in range(i + 1)], np.int32)
    nb = FOX_STREAMS
    assert b % nb == 0
    grid_spec = pltpu.PrefetchScalarGridSpec(
        num_scalar_prefetch=2,
        grid=(b // nb, len(qi)),
        in_specs=[pl.BlockSpec((nb, nh, tq, LANES), lambda bi, t, qi, kj: (bi, 0, qi[t], 0)),
                  pl.BlockSpec((nb, nh, tk, LANES), lambda bi, t, qi, kj: (bi, 0, kj[t], 0)),
                  pl.BlockSpec((nb, nh, FOX_VROWS, tk), lambda bi, t, qi, kj: (bi, 0, 0, kj[t]))],
        out_specs=pl.BlockSpec((nb, tq, w), lambda bi, t, qi, kj: (bi, qi[t], 0)),
        scratch_shapes=[pltpu.VMEM((nb, FOX_HEADS, tq), F32),
                        pltpu.VMEM((nb, FOX_HEADS, FOX_VROWS, tq), F32)])
    return pl.pallas_call(
        _fox_prompt_kernel,
        out_shape=jax.ShapeDtypeStruct((b, l, w), BF16),
        grid_spec=grid_spec,
        compiler_params=_params(("parallel", "arbitrary")),
        name="fox_prompt",
    )(jnp.asarray(qi), jnp.asarray(kj), qp, kp, vt)


def _fox_sample_kernel(q_ref, kp_ref, vp_ref, kn_ref, vn_ref, cq_ref, ck_ref, o_ref):
    lq = q_ref.shape[1]
    past = kp_ref.shape[4]
    ri = lax.broadcasted_iota(jnp.int32, (lq, lq), 0)
    ci = lax.broadcasted_iota(jnp.int32, (lq, lq), 1)
    nt = (((1,), (1,)), ((), ()))
    for h in range(FOX_HEADS):
        cols = slice(h * FOX_DIM, (h + 1) * FOX_DIM)
        q = q_ref[0, :, cols]
        cq = cq_ref[0, :, h:h + 1]
        sp = jnp.dot(q, kp_ref[0, 0, h].astype(BF16), preferred_element_type=F32)
        sp = sp + cq - ck_ref[0, h:h + 1, 0:past]
        sn = lax.dot_general(q, kn_ref[0, :, cols], nt, preferred_element_type=F32)
        sn = sn + cq - ck_ref[0, h:h + 1, past:past + lq]
        sn = jnp.where(ci <= ri, sn, NEG)
        m = jnp.maximum(jnp.max(sp, axis=-1, keepdims=True), jnp.max(sn, axis=-1, keepdims=True))
        pp = jnp.exp2(sp - m)
        pn = jnp.exp2(sn - m)
        den = jnp.sum(pp, axis=-1, keepdims=True) + jnp.sum(pn, axis=-1, keepdims=True)
        o = (lax.dot_general(pp.astype(BF16), vp_ref[0, 0, h].astype(BF16), nt, preferred_element_type=F32)
             + jnp.dot(pn.astype(BF16), vn_ref[0, :, cols], preferred_element_type=F32))
        o_ref[0, :, cols] = (o / den).astype(o_ref.dtype)


def fox_sample(q, kcache_t, vcache_t, layer, kn, vn, cq, ckt):
    b, lq, w = q.shape
    past = kcache_t.shape[4]
    lc = ckt.shape[2]
    new = lambda i: (i, 0, 0)
    old = lambda i: (layer, i, 0, 0, 0)
    cache_blk = pl.BlockSpec((1, 1, FOX_HEADS, FOX_DIM, past), old)
    return pl.pallas_call(
        _fox_sample_kernel,
        out_shape=jax.ShapeDtypeStruct((b, lq, w), BF16),
        grid=(b,),
        in_specs=[pl.BlockSpec((1, lq, w), new),
                  cache_blk,
                  cache_blk,
                  pl.BlockSpec((1, lq, w), new),
                  pl.BlockSpec((1, lq, w), new),
                  pl.BlockSpec((1, lq, FOX_HEADS), new),
                  pl.BlockSpec((1, FOX_HEADS, lc), new)],
        out_specs=pl.BlockSpec((1, lq, w), new),
        compiler_params=_params(("parallel",)),
        name="fox_sample",
    )(q, kcache_t, vcache_t, kn, vn, cq, ckt)


def _outproj_kernel(x_ref, hg_ref, fo_ref, mod_ref, fgain_ref, w_ref, gain2_ref,
                    wrh_ref, wrl_ref, br_ref, xo_ref, hs_ref, dest_ref, cnt_ref):
    s, ls, d = x_ref.shape
    tm = s * ls
    fo = fo_ref[...].astype(F32)
    fn = fo * lax.rsqrt(jnp.mean(fo * fo, axis=-1, keepdims=True) + EPS) * fgain_ref[...]
    mixed = (jnp.dot(hg_ref[...].reshape(tm, HG_WIDTH), w_ref[0:HG_WIDTH, :],
                     preferred_element_type=F32)
             + jnp.dot(fn.reshape(tm, FOX_WIDTH).astype(BF16), w_ref[HG_WIDTH:, :],
                       preferred_element_type=F32))
    x = x_ref[...] + mod_ref[:, 2:3, :] * mixed.reshape(s, ls, d)
    xo_ref[...] = x
    y = x * lax.rsqrt(jnp.mean(x * x, axis=-1, keepdims=True) + EPS) * gain2_ref[...]
    h2 = (y * (1.0 + mod_ref[:, 4:5, :]) + mod_ref[:, 3:4, :]).reshape(tm, d)
    h2b = h2.astype(BF16)

    nt = (((1,), (1,)), ((), ()))
    logits = lax.dot_general(wrh_ref[...], h2b, nt, preferred_element_type=F32) \
        + lax.dot_general(wrl_ref[...], h2b, nt, preferred_element_type=F32)
    z = jnp.exp(logits - jnp.max(logits, axis=0, keepdims=True))
    probs = z / jnp.sum(z, axis=0, keepdims=True)
    sel = probs + br_ref[...]
    rows = [sel[e:e + 1, :] for e in range(N_EXPERTS)]
    prow = [probs[e:e + 1, :] for e in range(N_EXPERTS)]

    def beats(a, ia, b_, ib):
        return jnp.where(a >= b_, 1.0, 0.0) if ia < ib else jnp.where(a > b_, 1.0, 0.0)

    top = []
    gscore = []
    for g in range(N_GROUPS):
        ids = range(g * GROUP_SIZE, (g + 1) * GROUP_SIZE)
        sc = jnp.zeros_like(rows[0])
        for e in ids:
            cnt = jnp.zeros_like(rows[0])
            for o in ids:
                if o != e:
                    cnt = cnt + beats(rows[o], o, rows[e], e)
            flag = jnp.where(cnt < TOP_K, 1.0, 0.0)
            top.append(flag)
            sc = sc + flag * rows[e]
        gscore.append(sc)
    chosen = []
    for g in range(N_GROUPS):
        cnt = jnp.zeros_like(rows[0])
        for o in range(N_GROUPS):
            if o != g:
                cnt = cnt + beats(gscore[o], o, gscore[g], g)
        chosen.append(jnp.where(cnt < 1.0, 1.0, 0.0))
    wts = [prow[e] * top[e] * chosen[e // GROUP_SIZE] for e in range(N_EXPERTS)]
    den = wts[0]
    for e in range(1, N_EXPERTS):
        den = den + wts[e]
    g4 = []
    for k in range(GROUP_SIZE):
        gk = wts[k]
        for g in range(1, N_GROUPS):
            gk = gk + wts[g * GROUP_SIZE + k]
        g4.append(gk / den)

    rr = hs_ref.shape[1]
    ki = lax.broadcasted_iota(jnp.int32, (tm, tm), 0)
    ji = lax.broadcasted_iota(jnp.int32, (tm, tm), 1)
    before = jnp.where(ki < ji, 1.0, 0.0).astype(BF16)
    chosen4 = jnp.concatenate(chosen, axis=0)
    rank = jnp.dot(chosen4.astype(BF16), before, preferred_element_type=F32)
    lane = lax.broadcasted_iota(jnp.int32, (1, LANES), 1)
    start = jnp.zeros((1, 1), F32)
    dest = jnp.zeros((1, tm), F32)
    cnt_row = jnp.zeros((1, LANES), F32)
    for g in range(N_GROUPS):
        n_g = jnp.sum(chosen[g], axis=1, keepdims=True)
        tiles_g = jnp.floor((n_g + (MOE_TILE - 1)) * (1.0 / MOE_TILE))
        dest = dest + chosen[g] * (start + rank[g:g + 1, :])
        cnt_row = cnt_row + jnp.where(lane == g, tiles_g, 0.0)
        start = start + tiles_g * MOE_TILE
    dest_i = dest.astype(jnp.int32)
    dest_ref[0] = dest_i
    cnt_ref[0] = cnt_row.astype(jnp.int32)
    perm = jnp.where(lax.broadcasted_iota(jnp.int32, (rr, tm), 0) == dest_i, 1.0, 0.0).astype(BF16)
    ghi = [g.astype(BF16) for g in g4]
    glo = [(g - h.astype(F32)).astype(BF16) for g, h in zip(g4, ghi)]
    gpk = jnp.concatenate(ghi + glo + [jnp.zeros((LANES - 2 * GROUP_SIZE, tm), BF16)], axis=0)
    hs_ref[0, :, 0:d] = jnp.dot(perm, h2b, preferred_element_type=F32).astype(BF16)
    hs_ref[0, :, d:d + LANES] = lax.dot_general(perm, gpk, nt, preferred_element_type=F32).astype(BF16)


def outproj(x, hg_out, fox_o, mod, fox_gain, w_out, gain2, wr_hi, wr_lo, b_router, s, ls):
    b, l, d = x.shape
    nblk = (b // s) * (l // ls)
    nl = l // ls
    tm = s * ls
    row = lambda i, j: (i, j, 0)
    const = lambda i, j: (0, 0)
    blk = lambda i, j: (i * nl + j, 0, 0)
    return pl.pallas_call(
        _outproj_kernel,
        out_shape=[jax.ShapeDtypeStruct((b, l, d), F32),
                   jax.ShapeDtypeStruct((nblk, MOE_ROWS, d + LANES), BF16),
                   jax.ShapeDtypeStruct((nblk, 1, tm), jnp.int32),
                   jax.ShapeDtypeStruct((nblk, 1, LANES), jnp.int32)],
        grid=(b // s, nl),
        in_specs=[pl.BlockSpec((s, ls, d), row),
                  pl.BlockSpec((s, ls, HG_WIDTH), row),
                  pl.BlockSpec((s, ls, FOX_WIDTH), row),
                  pl.BlockSpec((s, N_MOD, d), lambda i, j: (i, 0, 0)),
                  pl.BlockSpec((1, FOX_WIDTH), const),
                  pl.BlockSpec(w_out.shape, const, pipeline_mode=pl.Buffered(1)),
                  pl.BlockSpec((1, d), const),
                  pl.BlockSpec((N_EXPERTS, d), const),
                  pl.BlockSpec((N_EXPERTS, d), const),
                  pl.BlockSpec((N_EXPERTS, 1), const)],
        out_specs=[pl.BlockSpec((s, ls, d), row),
                   pl.BlockSpec((1, MOE_ROWS, d + LANES), blk),
                   pl.BlockSpec((1, 1, tm), blk),
                   pl.BlockSpec((1, 1, LANES), blk)],
        compiler_params=_params(("parallel", "parallel")),
        name="outproj",
    )(x, hg_out, fox_o, mod, fox_gain, w_out, gain2, wr_hi, wr_lo, b_router)


def _moe_tables(cnt, rows_per_step):
    nblk = cnt.shape[0]
    tiles_per_step = rows_per_step // MOE_TILE
    max_tiles = nblk * (BLOCK_TOKENS // MOE_TILE + N_GROUPS)
    n_steps = -(-max_tiles // tiles_per_step) + N_GROUPS
    npair = N_GROUPS * nblk
    start_gm = (jnp.cumsum(cnt, axis=1) - cnt).T.reshape(-1)
    cnt_gm = cnt.T.reshape(-1)
    csum = jnp.cumsum(cnt_gm)
    tau = jnp.arange(max_tiles, dtype=jnp.int32)
    pair = jnp.minimum(jnp.sum((csum[None, :] <= tau[:, None]).astype(jnp.int32), axis=1), npair - 1)
    hot = (pair[:, None] == jnp.arange(npair, dtype=jnp.int32)[None, :]).astype(jnp.int32)
    k = tau - jnp.sum(hot * (csum - cnt_gm)[None, :], axis=1)
    valid = tau < csum[-1]
    tile_blk = jnp.where(valid, pair % nblk, 0).astype(jnp.int32)
    tile_row = jnp.where(valid, (jnp.sum(hot * start_gm[None, :], axis=1) + k) * MOE_TILE, 0).astype(jnp.int32)
    tot = jnp.sum(cnt, axis=0)
    gstart = jnp.cumsum(tot) - tot
    nsteps = (tot + tiles_per_step - 1) // tiles_per_step
    send = jnp.cumsum(nsteps)
    s = jnp.arange(n_steps, dtype=jnp.int32)
    sg = jnp.minimum(jnp.sum((send[None, :] <= s[:, None]).astype(jnp.int32), axis=1), N_GROUPS - 1)
    ghot = (sg[:, None] == jnp.arange(N_GROUPS, dtype=jnp.int32)[None, :]).astype(jnp.int32)
    pick = lambda v: jnp.sum(ghot * v[None, :], axis=1)
    first = pick(gstart) + (s - pick(send - nsteps)) * tiles_per_step
    num = jnp.clip(pick(gstart + tot) - first, 0, tiles_per_step)
    first = jnp.where(num > 0, first, 0)
    return sg.astype(jnp.int32), first.astype(jnp.int32), num.astype(jnp.int32), tile_blk, tile_row


def _moe_kernel(sg_ref, first_ref, num_ref, tblk_ref, trow_ref,
                hs_a, hs_b, wg_ref, wu_ref, wd_ref, yinit_a, yinit_b, ys_a, ys_b,
                hbuf, ybuf, sem_in, sem_out):
    del sg_ref, yinit_a, yinit_b
    s = pl.program_id(0)
    ns = pl.num_programs(0)
    slot = lax.rem(s, 2)
    na = hs_a.shape[0]

    def tile(k):
        return pl.ds(pl.multiple_of(k * MOE_TILE, MOE_TILE), MOE_TILE)

    def src(step, k):
        t = first_ref[step] + k
        return tblk_ref[t], pl.ds(pl.multiple_of(trow_ref[t], MOE_TILE), MOE_TILE)

    class _Either:
        def __init__(self, blk, make):
            self.blk, self.make = blk, make

        def start(self):
            @pl.when(self.blk < na)
            def _():
                self.make(0, self.blk).start()

            @pl.when(self.blk >= na)
            def _():
                self.make(1, self.blk - na).start()

        def wait(self):
            self.make(0, 0).wait()

    def h_copy(step, sl, k):
        blk, rows = src(step, k)
        return _Either(blk, lambda g, b: pltpu.make_async_copy(
            (hs_a, hs_b)[g].at[b, rows, :], hbuf.at[sl, tile(k), :], sem_in.at[sl]))

    def y_copy(step, sl, k):
        blk, rows = src(step, k)
        return _Either(blk, lambda g, b: pltpu.make_async_copy(
            ybuf.at[sl, tile(k), :], (ys_a, ys_b)[g].at[b, rows, :], sem_out.at[sl]))

    def each_tile(step, fn):
        def body(k, c):
            fn(k)
            return c
        lax.fori_loop(0, num_ref[step], body, 0)

    def start_gather(step, sl):
        each_tile(step, lambda k: h_copy(step, sl, k).start())

    def wait_gather(step, sl):
        each_tile(step, lambda k: h_copy(step, sl, k).wait())

    @pl.when(s == 0)
    def _():
        hbuf[...] = jnp.zeros_like(hbuf)
        start_gather(0, 0)

    wait_gather(s, slot)

    @pl.when(s + 1 < ns)
    def _():
        start_gather(s + 1, 1 - slot)

    @pl.when(s >= 2)
    def _():
        each_tile(s - 2, lambda k: y_copy(s - 2, slot, k).wait())

    @pl.when(num_ref[s] > 0)
    def _():
        d = wg_ref.shape[1]
        h = hbuf[slot, :, 0:d]
        gp = hbuf[slot, :, d:d + LANES].astype(F32)
        acc = jnp.zeros((h.shape[0], wd_ref.shape[2]), F32)
        for k in range(GROUP_SIZE):
            a = jnp.dot(h, wg_ref[k].astype(BF16), preferred_element_type=F32)
            u = jnp.dot(h, wu_ref[k].astype(BF16), preferred_element_type=F32)
            gate = gp[:, k:k + 1] + gp[:, GROUP_SIZE + k:GROUP_SIZE + k + 1]
            he = (_silu(a) * u * gate).astype(BF16)
            acc = acc + jnp.dot(he, wd_ref[k].astype(BF16), preferred_element_type=F32)
        ybuf[slot] = acc.astype(ybuf.dtype)

    each_tile(s, lambda k: y_copy(s, slot, k).start())

    @pl.when(s == ns - 1)
    def _():
        each_tile(s, lambda k: y_copy(s, slot, k).wait())

        @pl.when(s >= 1)
        def _():
            each_tile(s - 1, lambda k: y_copy(s - 1, 1 - slot, k).wait())


def moe_sparse(hs_a, hs_b, cnt, wg, wu, wd, layer):
    d, de = wg.shape[1], wg.shape[2]
    ys_shape = lambda hs: hs.shape[:2] + (d,)
    rows = MOE_STEP_ROWS
    sg, first, num, tile_blk, tile_row = _moe_tables(cnt, rows)
    n_steps = sg.shape[0]
    wmap = lambda s, sg, *_: (layer * N_GROUPS + sg[s], 0, 0)
    grid_spec = pltpu.PrefetchScalarGridSpec(
        num_scalar_prefetch=5,
        grid=(n_steps,),
        in_specs=[pl.BlockSpec(memory_space=pl.ANY)] * 2
        + [pl.BlockSpec((GROUP_SIZE, d, de), wmap, pipeline_mode=pl.Buffered(1)),
           pl.BlockSpec((GROUP_SIZE, d, de), wmap, pipeline_mode=pl.Buffered(1)),
           pl.BlockSpec((GROUP_SIZE, de, d), wmap, pipeline_mode=pl.Buffered(1))]
        + [pl.BlockSpec(memory_space=pl.ANY)] * 2,
        out_specs=[pl.BlockSpec(memory_space=pl.ANY)] * 2,
        scratch_shapes=[pltpu.VMEM((2, rows, d + LANES), BF16),
                        pltpu.VMEM((2, rows, d), BF16),
                        pltpu.SemaphoreType.DMA((2,)),
                        pltpu.SemaphoreType.DMA((2,))])
    return pl.pallas_call(
        _moe_kernel,
        out_shape=[jax.ShapeDtypeStruct(ys_shape(hs_a), BF16), jax.ShapeDtypeStruct(ys_shape(hs_b), BF16)],
        grid_spec=grid_spec,
        input_output_aliases={10: 0, 11: 1},
        compiler_params=_params(("arbitrary",)),
        name="moe",
    )(sg, first, num, tile_blk, tile_row, hs_a, hs_b, wg, wu, wd,
      jnp.zeros(ys_shape(hs_a), BF16), jnp.zeros(ys_shape(hs_b), BF16))


def _combine_kernel(x_ref, ys_ref, dest_ref, mod_ref, fgain_ref, o_ref, *, final):
    s, ls, d = x_ref.shape
    tm = s * ls
    rr = ys_ref.shape[1]
    dcol = jnp.broadcast_to(dest_ref[0].astype(F32), (LANES, tm)).T
    lane = lax.broadcasted_iota(jnp.int32, (tm, LANES), 1).astype(F32)
    unperm = jnp.concatenate(
        [jnp.where(dcol == lane + float(c * LANES), 1.0, 0.0).astype(BF16) for c in range(rr // LANES)],
        axis=1)
    y = jnp.dot(unperm, ys_ref[0], preferred_element_type=F32)
    x = x_ref[...] + mod_ref[:, 5:6, :] * y.reshape(s, ls, d)
    if final:
        x = x * lax.rsqrt(jnp.mean(x * x, axis=-1, keepdims=True) + EPS) * fgain_ref[...]
    o_ref[...] = x


def combine(x, ys, dest, mod, final_gain, s, ls, final):
    b, l, d = x.shape
    nl = l // ls
    tm = s * ls
    row = lambda i, j: (i, j, 0)
    blk = lambda i, j: (i * nl + j, 0, 0)
    return pl.pallas_call(
        functools.partial(_combine_kernel, final=final),
        out_shape=jax.ShapeDtypeStruct((b, l, d), F32),
        grid=(b // s, nl),
        in_specs=[pl.BlockSpec((s, ls, d), row),
                  pl.BlockSpec((1, ys.shape[1], d), blk),
                  pl.BlockSpec((1, 1, tm), blk),
                  pl.BlockSpec((s, N_MOD, d), lambda i, j: (i, 0, 0)),
                  pl.BlockSpec((1, d), lambda i, j: (0, 0))],
        out_specs=pl.BlockSpec((s, ls, d), row),
        compiler_params=_params(("parallel", "parallel")),
        name="combine",
    )(x, ys, dest, mod, final_gain)


def _tile(b, l):
    ls = min(l, BLOCK_TOKENS)
    s = BLOCK_TOKENS // ls
    assert s * ls == BLOCK_TOKENS and b % s == 0 and l % ls == 0
    assert BLOCK_TOKENS + N_GROUPS * MOE_TILE <= MOE_ROWS
    return s, ls


def _mixer(x, mod, l, p, hg_state0, fox_past, kv_prev):
    depth = p['norm_mix_gain'].shape[0]
    b, seq, d = x.shape
    s, ls = _tile(b, seq)
    sample = fox_past is not None
    hg_in, fq, kbuf, vbuf, *k16v16, ff = inproj(x, mod, p['norm_mix_gain'][l], p['w_in'][l], s, ls,
                                                l, depth, kv_prev, sample)

    hg_out, hg_state = hgrn(hg_in, hg_state0, p['hg_lower_bounds'], p['hg_norm_gain'][l], l)

    if not sample:
        tq = min(seq, BLOCK_TOKENS)
        hg_out, fq = lax.optimization_barrier((hg_out, fq))
        qp, kp, vt, lf_pad = foxpack(fq, kbuf, vbuf, l, ff, p['fox_f_bias_pad'][l], tq)
        fox_o = fox_prompt(qp, kp, vt, tq)
        logf = lf_pad[:, :, :FOX_HEADS]
    else:
        kcache, vcache, plogf = fox_past
        past = kcache.shape[4]
        fft = jnp.swapaxes(ff[:, :, :FOX_HEADS], 1, 2)
        logft, _ = forget_cumsum(fft, p['fox_f_bias'][l], True, seq)
        tot = past + seq
        pad = (-tot) % LANES
        allt = jnp.concatenate([jnp.swapaxes(plogf[l], 1, 2), logft,
                                jnp.zeros((b, FOX_HEADS, pad), F32)], axis=2)
        _, ct = forget_cumsum(allt, p['fox_f_bias'][l], False, LANES)
        ct = ct * LOG2E
        cq = jnp.swapaxes(ct[:, :, past:tot], 1, 2)
        fox_o = fox_sample(fq, kcache, vcache, l, k16v16[0], k16v16[1], cq, ct)
        logf = jnp.swapaxes(logft, 1, 2)

    routed = outproj(x, hg_out, fox_o, mod, p['fox_out_gain'][l], p['w_out'][l],
                     p['norm_ffn_gain'][l], p['wr_hi'], p['wr_lo'], p['b_router'], s, ls)
    return routed, (kbuf, vbuf), hg_state, logf


def _ffn(routed_p, routed_s, mod_p, mod_s, l, p, final):
    xm_p, hs_p, dest_p, cnt_p = routed_p
    xm_s, hs_s, dest_s, cnt_s = routed_s
    cnt = jnp.concatenate([cnt_p, cnt_s], axis=0)[:, 0, :N_GROUPS]
    ys_p, ys_s = moe_sparse(hs_p, hs_s, cnt, p['w_exp_gate'], p['w_exp_up'], p['w_exp_down'], l)
    outs = []
    for xm, ys, dest, mod in ((xm_p, ys_p, dest_p, mod_p), (xm_s, ys_s, dest_s, mod_s)):
        s, ls = _tile(xm.shape[0], xm.shape[1])
        outs.append(combine(xm, ys, dest, mod, p['final_norm_gain'], s, ls, final))
    return outs


def kernel(x_prompt, x_sample, cache_fox_k, cache_fox_v, cache_fox_logf, state_hgrn, c_prompt, c_sample,
           norm_mix_gain, norm_ffn_gain, w_ada, b_ada, w_in, hg_lower_bounds, hg_norm_gain,
           fox_f_bias, fox_out_gain, w_out, w_router, b_router, w_exp_gate, w_exp_up, w_exp_down,
           final_norm_gain):
    depth, d = norm_mix_gain.shape
    bp = x_prompt.shape[0]
    n_in = w_in.shape[2]
    n_pad = 4 * HG_WIDTH + 3 * FOX_WIDTH + LANES - n_in
    wr_t = w_router.T
    wr_hi = wr_t.astype(BF16)
    p = {
        'norm_mix_gain': norm_mix_gain.reshape(depth, 1, d),
        'norm_ffn_gain': norm_ffn_gain.reshape(depth, 1, d),
        'w_in': jnp.pad(w_in, ((0, 0), (0, 0), (0, n_pad))).astype(BF16),
        'hg_lower_bounds': hg_lower_bounds,
        'hg_norm_gain': hg_norm_gain.reshape(depth, 1, HG_WIDTH),
        'fox_f_bias': fox_f_bias.reshape(depth, FOX_HEADS, 1),
        'fox_f_bias_pad': jnp.pad(fox_f_bias, ((0, 0), (0, LANES - FOX_HEADS))).reshape(depth, 1, LANES),
        'fox_out_gain': fox_out_gain.reshape(depth, 1, FOX_WIDTH),
        'w_out': w_out.astype(BF16),
        'wr_hi': wr_hi,
        'wr_lo': (wr_t - wr_hi.astype(F32)).astype(BF16),
        'b_router': b_router.reshape(N_EXPERTS, 1),
        'w_exp_gate': w_exp_gate.reshape((depth * N_EXPERTS,) + w_exp_gate.shape[2:]),
        'w_exp_up': w_exp_up.reshape((depth * N_EXPERTS,) + w_exp_up.shape[2:]),
        'w_exp_down': w_exp_down.reshape((depth * N_EXPERTS,) + w_exp_down.shape[2:]),
        'final_norm_gain': final_norm_gain.reshape(1, d),
    }
    mods = ada_mod(jnp.concatenate([c_prompt, c_sample], axis=0), w_ada, b_ada)
    mods = mods.reshape(depth, -1, N_MOD, d)
    bs, lsq = x_sample.shape[:2]
    lp = x_prompt.shape[1]
    past = cache_fox_k.shape[2]
    fox_past = (jnp.transpose(cache_fox_k, (0, 1, 3, 4, 2)), jnp.transpose(cache_fox_v, (0, 1, 3, 4, 2)),
                cache_fox_logf)

    xp, xs = x_prompt, x_sample
    kv_p = kv_s = None
    st_p, st_s, lf_p, lf_s = [], [], [], []
    zero_state = jnp.zeros((bp, HG_HEADS, HG_DIM, HG_DIM), F32)
    for l in range(depth):
        routed_p, kv_p, st, lf = _mixer(xp, mods[l, :bp], l, p, zero_state, None, kv_p)
        st_p.append(st)
        lf_p.append(lf)
        routed_s, kv_s, st, lf = _mixer(xs, mods[l, bp:], l, p, state_hgrn[l], fox_past, kv_s)
        st_s.append(st)
        lf_s.append(lf)
        xp, xs = _ffn(routed_p, routed_s, mods[l, :bp], mods[l, bp:], l, p, l == depth - 1)
    heads = lambda a, b, l: a.reshape(depth, b, l, FOX_HEADS, FOX_DIM)
    return (xp, xs,
            jnp.stack(st_p), heads(kv_p[0], bp, lp), heads(kv_p[1], bp, lp), jnp.stack(lf_p),
            jnp.stack(st_s), heads(kv_s[0], bs, lsq), heads(kv_s[1], bs, lsq), jnp.stack(lf_s))
```
